```python
import math
import jax
import jax.numpy as jnp
from jax import lax
import numpy as np

D_MODEL = 1024
BATCH = 32
SEQ = 2048
DEPTH = 4

N_MIXERS = 4
N_HEADS = 16
HEAD_DIM = 64
ATTN_WIDTH = N_HEADS * HEAD_DIM
NUM_BUCKETS = 32
MAX_DISTANCE = 2048
DIL_GROUPS = ((128, 1), (512, 4), (2048, 16))
CONV_WIDTH = 3
MOBA_BLOCK = 256
MOBA_TOPK = 3
MOBA_QGROUP = 128
MOBA_UNITS_PER_STEP = 256
GMLP_CHUNK = 128
GMLP_GROUPS = 8
GMLP_WIDTH = D_MODEL
D_FF = 2816
N_EXPERTS = 8
TOP_K = 2
D_FF_EXPERT = 3584
MOE_UNIT = 512
ALPHA = (2 * DEPTH) ** 0.25
BETA = (8 * DEPTH) ** -0.25
LN_EPS = 1e-5
NEG_INF = -1e30
N_A = len(range(0, DEPTH, N_MIXERS))
N_B = len(range(1, DEPTH, N_MIXERS))
N_C = len(range(2, DEPTH, N_MIXERS))
N_D = len(range(3, DEPTH, N_MIXERS))
N_DENSE = len(range(0, DEPTH, 2))
N_MOE = len(range(1, DEPTH, 2))

kernel_name = 'hybrid_dilated_conv_moba_gmlp_moe'

F32 = jnp.float32


def _layernorm(x, g, b):
    xf = x.astype(F32)
    mu = xf.mean(-1, keepdims=True)
    var = jnp.square(xf - mu).mean(-1, keepdims=True)
    return ((xf - mu) * lax.rsqrt(var + LN_EPS) * g + b).astype(x.dtype)


def _t5_bucket(dist):
    max_exact = NUM_BUCKETS // 2
    d = jnp.maximum(dist, 0)
    log_ratio = jnp.log(jnp.maximum(d, 1).astype(F32) / max_exact) / math.log(MAX_DISTANCE / max_exact)
    large = max_exact + (log_ratio * (NUM_BUCKETS - max_exact)).astype(jnp.int32)
    large = jnp.clip(large, max_exact, NUM_BUCKETS - 1)
    return jnp.where(d < max_exact, d, large)


def _rel_bias(rel_bias, dist):
    return jnp.moveaxis(rel_bias[_t5_bucket(dist)], -1, 0).astype(F32)


def _partial_softmax(s, v, mask=None):
    if mask is not None:
        s = jnp.where(mask, s, NEG_INF)
    m = s.max(-1)
    p = jnp.exp(s - m[..., None])
    l = p.sum(-1)
    o = jnp.einsum('...qk,...ke->...qe', p.astype(v.dtype), v)
    return m, l, o


def _merge_partials(ms, ls, os_):
    m = ms[0]
    for mi in ms[1:]:
        m = jnp.maximum(m, mi)
    w = jnp.exp(ms[0] - m)
    num = w[..., None] * os_[0].astype(F32)
    den = w * ls[0]
    for mi, li, oi in zip(ms[1:], ls[1:], os_[1:]):
        w = jnp.exp(mi - m)
        num = num + w[..., None] * oi.astype(F32)
        den = den + w * li
    return num / den[..., None]


def _group_slots(gid, n_groups, unit, round_units):
    n_items = gid.shape[0]
    n_units = -(-n_items // unit) + n_groups
    n_units = -(-n_units // round_units) * round_units
    counts = jnp.bincount(gid, length=n_groups + 1)[:n_groups]
    padded = (counts + unit - 1) // unit * unit
    pad_end = jnp.cumsum(padded)
    pad_start = pad_end - padded
    start = jnp.cumsum(counts) - counts
    order = jnp.argsort(gid).astype(jnp.int32)
    g_sorted = gid[order]
    g_clip = jnp.minimum(g_sorted, n_groups - 1)
    rank = jnp.arange(n_items, dtype=jnp.int32) - start[g_clip]
    dest = jnp.where(g_sorted < n_groups, pad_start[g_clip] + rank, n_units * unit)
    slot_item = jnp.full((n_units * unit,), n_items, jnp.int32).at[dest].set(order, mode='drop')
    unit_group = jnp.searchsorted(pad_end, jnp.arange(n_units, dtype=jnp.int32) * unit, side='right')
    unit_group = jnp.minimum(unit_group, n_groups - 1).astype(jnp.int32)
    return slot_item.reshape(n_units, unit), unit_group


def _dilated_group_attention(q, k, v, rel_bias, window, dil):
    bsz, seq, heads, hd = q.shape
    n_back = window // dil
    sub_len = seq // dil
    n_blk = -(-sub_len // n_back)
    pad_len = n_blk * n_back - sub_len

    def to_sub(t):
        t = t.reshape(bsz, sub_len, dil, heads, hd).transpose(0, 2, 3, 1, 4)
        t = jnp.pad(t, ((0, 0), (0, 0), (0, 0), (0, pad_len), (0, 0)))
        return t.reshape(bsz, dil, heads, n_blk, n_back, hd)

    def with_prev(t):
        prev = jnp.pad(t, ((0, 0), (0, 0), (0, 0), (1, 0), (0, 0), (0, 0)))[:, :, :, :-1]
        return jnp.concatenate([prev, t], axis=4)

    def from_sub(t):
        t = t.reshape((bsz, dil, heads, n_blk * n_back) + t.shape[5:])[:, :, :, :sub_len]
        t = jnp.moveaxis(t, 3, 1)
        return t.reshape((bsz, seq, heads) + t.shape[4:])

    qb, kw, vw = to_sub(q), with_prev(to_sub(k)), with_prev(to_sub(v))
    qi = jnp.arange(n_back)[:, None]
    kj = jnp.arange(2 * n_back)[None, :]
    delta = qi + n_back - kj
    band = (delta >= 0) & (delta <= n_back)
    not_front = (jnp.arange(n_blk)[:, None, None] > 0) | (kj >= n_back)[None]
    mask = band[None] & not_front
    bias = _rel_bias(rel_bias, delta * dil)
    s = jnp.einsum('bchnqe,bchnke->bchnqk', qb * (hd ** -0.5), kw).astype(F32) + bias[None, None, :, None]
    m, l, o = _partial_softmax(s, vw, mask)
    return from_sub(m), from_sub(l), from_sub(o)


def _dilated_mixer(x, w_in, w_out, rel_bias):
    bsz, seq, dm = x.shape
    group_cols = 3 * ATTN_WIDTH
    ms, ls, os_ = [], [], []
    for g, (window, dil) in enumerate(DIL_GROUPS):
        w_g = w_in[:, g * group_cols:(g + 1) * group_cols].reshape(dm, 3, N_HEADS, HEAD_DIM)
        qkv = jnp.einsum('bsd,dthe->tbshe', x, w_g)
        m, l, o = _dilated_group_attention(qkv[0], qkv[1], qkv[2], rel_bias, window, dil)
        ms.append(m)
        ls.append(l)
        os_.append(o)
    out = _merge_partials(ms, ls, os_)
    return out.reshape(bsz, seq, ATTN_WIDTH).astype(x.dtype) @ w_out


def _short_conv_mixer(x, w_in, conv_w, w_out):
    dm = x.shape[-1]
    gate_b, gate_c, x_in = jnp.split(x @ w_in, 3, axis=-1)
    z = lax.conv_general_dilated(gate_c * x_in, conv_w[:, None, :], window_strides=(1,),
                                 padding=((CONV_WIDTH - 1, 0),),
                                 dimension_numbers=('NWC', 'WIO', 'NWC'), feature_group_count=dm)
    return (gate_b * z) @ w_out


def _moba_mixer(x, w_in, w_out, rel_bias):
    bsz, seq, dm = x.shape
    H, hd, blk = N_HEADS, HEAD_DIM, MOBA_BLOCK
    n_blk = -(-seq // blk)
    seq_p = n_blk * blk
    qkv = jnp.einsum('bsd,dthe->tbhse', x, w_in.reshape(dm, 3, H, hd))
    qkv = jnp.pad(qkv, ((0, 0), (0, 0), (0, 0), (0, seq_p - seq), (0, 0))).reshape(3, bsz, H, n_blk, blk, hd)
    q, k, v = qkv[0] * (hd ** -0.5), qkv[1], qkv[2]
    pos = jnp.arange(blk)
    own_bias = _rel_bias(rel_bias, pos[:, None] - pos[None, :])
    s_own = jnp.einsum('bhnqe,bhnke->bhnqk', q, k).astype(F32) + own_bias[None, :, None]
    m_own, l_own, o_own = _partial_softmax(s_own, v, pos[:, None] >= pos[None, :])
    ms, ls, os_ = [m_own], [l_own], [o_own]
    n_sel = min(MOBA_TOPK, n_blk - 1)
    if n_sel > 0:
        k_mean = k.mean(axis=3)
        gate = jnp.einsum('bhnqe,bhme->bhnqm', q, k_mean).astype(F32)
        blk_id = jnp.arange(n_blk)
        gate = jnp.where((blk_id[None, :] < blk_id[:, None])[:, None, :], gate, NEG_INF)
        _, sel = lax.top_k(gate, n_sel)
        t_pos = blk_id[:, None] * blk + pos[None, :]
        valid = (jnp.arange(n_sel)[None, None, :] < blk_id[:, None, None]) & (t_pos < seq)[:, :, None]
        n_groups = bsz * H * n_blk
        bh = jnp.arange(bsz * H, dtype=jnp.int32).reshape(bsz, H, 1, 1, 1)
        gid = jnp.where(valid, bh * n_blk + sel.astype(jnp.int32), n_groups).reshape(-1)
        n_items = gid.shape[0]
        slot_item, unit_group = _group_slots(gid, n_groups, MOBA_QGROUP, MOBA_UNITS_PER_STEP)
        q_rows = q.reshape(-1, hd)
        k_blocks = k.reshape(n_groups, blk, hd)
        v_blocks = v.reshape(n_groups, blk, hd)

        def unit_step(args):
            items, groups = args
            row = jnp.minimum(items, n_items - 1) // n_sel
            t = row % seq_p
            head = (row // seq_p) % H
            key_pos = (groups % n_blk)[:, None] * blk + pos[None, :]
            bias = rel_bias[_t5_bucket(t[:, :, None] - key_pos[:, None, :]), head[:, :, None]]
            s = jnp.einsum('cqe,cke->cqk', q_rows[row], k_blocks[groups]).astype(F32) + bias.astype(F32)
            return _partial_softmax(s, v_blocks[groups])

        n_steps = slot_item.shape[0] // MOBA_UNITS_PER_STEP
        m_u, l_u, o_u = lax.map(unit_step, (slot_item.reshape(n_steps, MOBA_UNITS_PER_STEP, MOBA_QGROUP),
                                            unit_group.reshape(n_steps, MOBA_UNITS_PER_STEP)))
        flat = slot_item.reshape(-1)
        m_it = jnp.full((n_items,), NEG_INF, F32).at[flat].set(m_u.reshape(-1), mode='drop')
        l_it = jnp.zeros((n_items,), F32).at[flat].set(l_u.reshape(-1), mode='drop')
        o_it = jnp.zeros((n_items, hd), F32).at[flat].set(o_u.reshape(-1, hd).astype(F32), mode='drop')
        shp = (bsz, H, n_blk, blk, n_sel)
        m_it, l_it, o_it = m_it.reshape(shp), l_it.reshape(shp), o_it.reshape(shp + (hd,))
        for j in range(n_sel):
            ms.append(m_it[..., j])
            ls.append(l_it[..., j])
            os_.append(o_it[..., j, :])
    out = _merge_partials(ms, ls, os_)
    out = out.reshape(bsz, H, seq_p, hd)[:, :, :seq].transpose(0, 2, 1, 3).reshape(bsz, seq, ATTN_WIDTH)
    return out.astype(x.dtype) @ w_out


def _gmlp_mixer(x, w_in, ln_g, ln_b, w_s, b_s, w_out):
    bsz, seq, _ = x.shape
    u, v = jnp.split(jax.nn.gelu(x @ w_in, approximate=False), 2, axis=-1)
    v = _layernorm(v, ln_g, ln_b)
    n_chunk = seq // GMLP_CHUNK
    vc = v.reshape(bsz, n_chunk, GMLP_CHUNK, GMLP_GROUPS, GMLP_WIDTH // GMLP_GROUPS)
    causal = jnp.tril(jnp.ones((GMLP_CHUNK, GMLP_CHUNK), dtype=bool))
    ws = jnp.where(causal[None], w_s, 0)
    mixed = jnp.einsum('gts,bnsgc->bntgc', ws, vc) + b_s.T[None, None, :, :, None]
    return (u * mixed.reshape(bsz, seq, GMLP_WIDTH)) @ w_out


def _swiglu(x, w_gate, w_up, w_down):
    return (jax.nn.silu(x @ w_gate) * (x @ w_up)) @ w_down


def _moe_swiglu(x, w_router, w_gate, w_up, w_down):
    bsz, seq, dm = x.shape
    xt = x.reshape(-1, dm)
    logits = (xt @ w_router).astype(F32)
    top_val, top_idx = lax.top_k(logits, TOP_K)
    gates = jax.nn.softmax(top_val, axis=-1)
    n_items = xt.shape[0] * TOP_K
    slot_item, unit_expert = _group_slots(top_idx.reshape(-1).astype(jnp.int32), N_EXPERTS, MOE_UNIT, 1)
    it = jnp.minimum(slot_item, n_items - 1)
    tok = it // TOP_K
    g = jnp.where(slot_item < n_items, gates.reshape(-1)[it], 0.0)

    def expert_step(args):
        xb, e = args
        return _swiglu(xb, w_gate[e], w_up[e], w_down[e])

    ys = lax.map(expert_step, (xt[tok], unit_expert))
    contrib = (ys.astype(F32) * g[..., None]).reshape(-1, dm)
    out = jnp.zeros(xt.shape, F32).at[tok.reshape(-1)].add(contrib)
    return out.astype(x.dtype).reshape(bsz, seq, dm)


def setup_inputs(seed: int = 0) -> dict:
    key = jax.random.key(seed)
    keys = iter(jax.random.split(key, 32))

    def nrm(shape, scale):
        return jax.random.normal(next(keys), shape, F32) * scale

    D = D_MODEL
    n_dil = len(DIL_GROUPS)
    return {
        'x': nrm((BATCH, SEQ, D), 1.0),
        'rel_bias': nrm((NUM_BUCKETS, N_HEADS), 0.3),
        'ln_gain': 1.0 + nrm((DEPTH, 2, D), 0.02),
        'ln_bias': nrm((DEPTH, 2, D), 0.02),
        'a_w_in': nrm((N_A, D, n_dil * 3 * ATTN_WIDTH), D ** -0.5),
        'a_w_out': nrm((N_A, ATTN_WIDTH, D), BETA * ATTN_WIDTH ** -0.5),
        'b_w_in': nrm((N_B, D, 3 * D), D ** -0.5),
        'b_conv': nrm((N_B, CONV_WIDTH, D), CONV_WIDTH ** -0.5),
        'b_w_out': nrm((N_B, D, D), BETA * D ** -0.5),
        'c_w_in': nrm((N_C, D, 3 * ATTN_WIDTH), D ** -0.5),
        'c_w_out': nrm((N_C, ATTN_WIDTH, D), BETA * ATTN_WIDTH ** -0.5),
        'd_w_in': nrm((N_D, D, 2 * GMLP_WIDTH), D ** -0.5),
        'd_ln_gain': 1.0 + nrm((N_D, GMLP_WIDTH), 0.02),
        'd_ln_bias': nrm((N_D, GMLP_WIDTH), 0.02),
        'd_w_s': nrm((N_D, GMLP_GROUPS, GMLP_CHUNK, GMLP_CHUNK), GMLP_CHUNK ** -0.5),
        'd_b_s': 1.0 + nrm((N_D, GMLP_GROUPS, GMLP_CHUNK), 0.02),
        'd_w_out': nrm((N_D, GMLP_WIDTH, D), BETA * GMLP_WIDTH ** -0.5),
        'f_w_gate': nrm((N_DENSE, D, D_FF), D ** -0.5),
        'f_w_up': nrm((N_DENSE, D, D_FF), D ** -0.5),
        'f_w_down': nrm((N_DENSE, D_FF, D), BETA * D_FF ** -0.5),
        'e_w_router': nrm((N_MOE, D, N_EXPERTS), D ** -0.5),
        'e_w_gate': nrm((N_MOE, N_EXPERTS, D, D_FF_EXPERT), D ** -0.5),
        'e_w_up': nrm((N_MOE, N_EXPERTS, D, D_FF_EXPERT), D ** -0.5),
        'e_w_down': nrm((N_MOE, N_EXPERTS, D_FF_EXPERT, D), BETA * D_FF_EXPERT ** -0.5),
    }


def reference(x, rel_bias, ln_gain, ln_bias, a_w_in, a_w_out, b_w_in, b_conv, b_w_out,
              c_w_in, c_w_out, d_w_in, d_ln_gain, d_ln_bias, d_w_s, d_b_s, d_w_out,
              f_w_gate, f_w_up, f_w_down, e_w_router, e_w_gate, e_w_up, e_w_down):
    h = x
    for i in range(DEPTH):
        kind, j = i % N_MIXERS, i // N_MIXERS
        if kind == 0:
            y = _dilated_mixer(h, a_w_in[j], a_w_out[j], rel_bias)
        elif kind == 1:
            y = _short_conv_mixer(h, b_w_in[j], b_conv[j], b_w_out[j])
        elif kind == 2:
            y = _moba_mixer(h, c_w_in[j], c_w_out[j], rel_bias)
        else:
            y = _gmlp_mixer(h, d_w_in[j], d_ln_gain[j], d_ln_bias[j], d_w_s[j], d_b_s[j], d_w_out[j])
        h = _layernorm(ALPHA * h + y.astype(h.dtype), ln_gain[i, 0], ln_bias[i, 0])
        f = i // 2
        if i % 2 == 0:
            y = _swiglu(h, f_w_gate[f], f_w_up[f], f_w_down[f])
        else:
            y = _moe_swiglu(h, e_w_router[f], e_w_gate[f], e_w_up[f], e_w_down[f])
        h = _layernorm(ALPHA * h + y.astype(h.dtype), ln_gain[i, 1], ln_bias[i, 1])
    return h
```

```python
import functools
import math

import jax
import jax.numpy as jnp
from jax import lax
from jax.experimental import pallas as pl
from jax.experimental.pallas import tpu as pltpu

F32 = jnp.float32
BF16 = jnp.bfloat16
I32 = jnp.int32

N_HEADS = 16
HEAD_DIM = 64
LANES = 128
HEADS_PER_TILE = LANES // HEAD_DIM
N_HEAD_TILES = N_HEADS // HEADS_PER_TILE
ATTN_WIDTH = N_HEADS * HEAD_DIM
NUM_BUCKETS = 32
MAX_DISTANCE = 2048
DIL_GROUPS = ((128, 1), (512, 4), (2048, 16))
DIL_BACK = 128
MOBA_BLOCK = 256
MOBA_TOPK = 3
GMLP_CHUNK = 128
GMLP_GROUPS = 8
N_EXPERTS = 8
DEPTH = 4
ALPHA = (2 * DEPTH) ** 0.25
LN_EPS = 1e-5
NEG_INF = -1e30

TOKEN_TILE = 512
MOE_TILE = 1024
MOE_FF_TILE = 512
VMEM_LIMIT = 56 * 1024 * 1024


def _params(semantics, **kw):
    return pltpu.CompilerParams(dimension_semantics=semantics, vmem_limit_bytes=VMEM_LIMIT, **kw)


def _resident(shape):
    return pl.BlockSpec(shape, lambda *_: (0,) * len(shape), pipeline_mode=pl.Buffered(1))


def _layernorm(z, gain, bias):
    mu = jnp.mean(z, axis=-1, keepdims=True)
    zc = z - mu
    var = jnp.mean(zc * zc, axis=-1, keepdims=True)
    return zc * lax.rsqrt(var + LN_EPS) * gain + bias


def _gelu(x):
    return 0.5 * x * (1.0 + lax.erf(x * math.sqrt(0.5)))


def _dot(a, b):
    return jnp.dot(a, b, preferred_element_type=F32)


def _dot_nt(a, b):
    return lax.dot_general(a, b, (((1,), (1,)), ((), ())), preferred_element_type=F32)


def _split3(x):
    x1 = x.astype(BF16)
    r1 = x - x1.astype(F32)
    x2 = r1.astype(BF16)
    x3 = (r1 - x2.astype(F32)).astype(BF16)
    return x1, x2, x3


def _t5_bucket(dist):
    max_exact = NUM_BUCKETS // 2
    d = jnp.maximum(dist, 0)
    log_ratio = jnp.log(jnp.maximum(d, 1).astype(F32) / max_exact) / math.log(MAX_DISTANCE / max_exact)
    large = max_exact + (log_ratio * (NUM_BUCKETS - max_exact)).astype(I32)
    large = jnp.clip(large, max_exact, NUM_BUCKETS - 1)
    return jnp.where(d < max_exact, d, large)


def _bias_lookup(rel_bias, dist):
    onehot = (_t5_bucket(dist)[..., None] == jnp.arange(NUM_BUCKETS, dtype=I32)).astype(F32)
    out = jnp.einsum('...k,kh->...h', onehot, rel_bias.astype(F32), precision=lax.Precision.HIGHEST)
    return jnp.moveaxis(out, -1, 0)


def _dilated_bias_table(rel_bias, dil):
    qi = jnp.arange(DIL_BACK, dtype=I32)[:, None]
    kj = jnp.arange(2 * DIL_BACK, dtype=I32)[None, :]
    delta = qi + DIL_BACK - kj
    band = (delta >= 0) & (delta <= DIL_BACK)
    bias = _bias_lookup(rel_bias, delta * dil)
    return jnp.where(band[None], bias, NEG_INF)


def _moba_bias_table(rel_bias, n_blk):
    pos = jnp.arange(MOBA_BLOCK, dtype=I32)
    diff = pos[:, None] - pos[None, :]
    dist = jnp.arange(n_blk, dtype=I32)[:, None, None] * MOBA_BLOCK + diff[None]
    bias = _bias_lookup(rel_bias, dist)
    causal = (dist >= 0)[None]
    return jnp.where(causal, bias, NEG_INF)


def _proj_kernel(*refs, dil, n_split):
    x_refs, w_ref, o_ref = refs[:-2], refs[-2], refs[-1]
    tm = x_refs[0].shape[0]
    rows = tm // dil
    if dil == 1:
        xb = jnp.concatenate([x_ref[...] for x_ref in x_refs], axis=1).astype(BF16)
    else:
        xb = jnp.concatenate(
            [jnp.concatenate([x_ref[pl.ds(r, rows, stride=dil), :] for x_ref in x_refs], axis=1)
             for r in range(dil)], axis=0).astype(BF16)
    width = w_ref.shape[1] // n_split
    for c in range(n_split):
        y = _dot(xb, w_ref[:, c * width:(c + 1) * width]).astype(BF16)
        for r in range(dil):
            o_ref[r, :, c * width:(c + 1) * width] = y[r * rows:(r + 1) * rows]


def _project_dilated(x, w, dil):
    bsz, seq, dm = x.shape
    n = w.shape[1]
    tm = TOKEN_TILE
    return pl.pallas_call(
        functools.partial(_proj_kernel, dil=dil, n_split=3),
        grid=(bsz, seq // tm),
        in_specs=[pl.BlockSpec((None, tm, LANES), lambda b, j, c=c: (b, j, c)) for c in range(dm // LANES)]
        + [_resident((dm, n))],
        out_specs=pl.BlockSpec((None, dil, tm // dil, n), lambda b, j: (b, 0, j, 0)),
        out_shape=jax.ShapeDtypeStruct((bsz, dil, seq // dil, n), BF16),
        compiler_params=_params(("parallel", "parallel")),
        name=f"proj_dil{dil}",
    )(*([x] * (dm // LANES)), w)


def _proj_kmean_kernel(x_ref, w_ref, o_ref, km_ref):
    tm = x_ref.shape[0]
    per_step = tm // MOBA_BLOCK
    j = pl.program_id(1)
    xb = x_ref[...].astype(BF16)
    width = ATTN_WIDTH
    for c in range(3):
        y = _dot(xb, w_ref[:, c * width:(c + 1) * width])
        o_ref[:, c * width:(c + 1) * width] = y.astype(BF16)
        if c == 1:
            for jj in range(per_step):
                km_ref[pl.ds(j * per_step + jj, 1), :] = jnp.mean(
                    y[jj * MOBA_BLOCK:(jj + 1) * MOBA_BLOCK], axis=0, keepdims=True)


def _project_moba(x, w):
    bsz, seq, dm = x.shape
    n = w.shape[1]
    tm = TOKEN_TILE
    n_blk = seq // MOBA_BLOCK
    return pl.pallas_call(
        _proj_kmean_kernel,
        grid=(bsz, seq // tm),
        in_specs=[pl.BlockSpec((None, tm, dm), lambda b, j: (b, j, 0)),
                  _resident((dm, n))],
        out_specs=[pl.BlockSpec((None, tm, n), lambda b, j: (b, j, 0)),
                   pl.BlockSpec((None, n_blk, ATTN_WIDTH), lambda b, j: (b, 0, 0))],
        out_shape=[jax.ShapeDtypeStruct((bsz, seq, n), BF16),
                   jax.ShapeDtypeStruct((bsz, n_blk, ATTN_WIDTH), F32)],
        compiler_params=_params(("parallel", "arbitrary")),
        name="proj_moba",
    )(x, w)


def _two_head_partial(q, kw, vw, bias_a, bias_b, is_a):
    zero = jnp.zeros_like(q)

    def one(qh, bias):
        s = _dot_nt(qh, kw) + bias
        m = jnp.max(s, axis=-1, keepdims=True)
        p = jnp.exp(s - m)
        l = jnp.sum(p, axis=-1, keepdims=True)
        return m, l, _dot(p.astype(BF16), vw)

    m_a, l_a, o_a = one(jnp.where(is_a, q, zero), bias_a)
    m_b, l_b, o_b = one(jnp.where(is_a, zero, q), bias_b)
    return jnp.where(is_a, m_a, m_b), jnp.where(is_a, l_a, l_b), jnp.where(is_a, o_a, o_b)


def _dilated_attn_kernel(q0, k0, v0, q1, k1, v1, q2, k2, v2, b0, b1, b2, o_ref, m_s, l_s, a_s):
    nb = DIL_BACK
    is_a = lax.broadcasted_iota(I32, (nb, LANES), 1) < HEAD_DIM
    n_res2, n_res1 = q2.shape[0], q1.shape[0]
    n_blk1 = q1.shape[1] // nb
    n_blk0 = q0.shape[0] // nb

    def merge(rows, m, l, o):
        m_old, l_old, a_old = m_s[rows, :], l_s[rows, :], a_s[rows, :]
        m_new = jnp.maximum(m_old, m)
        w_old, w_new = jnp.exp(m_old - m_new), jnp.exp(m - m_new)
        return m_new, w_old * l_old + w_new * l, w_old * a_old + w_new * o

    def group2(r, carry):
        m, l, o = _two_head_partial(q2[r], k2[r], v2[r], b2[0], b2[1], is_a)
        rows = pl.ds(r, nb, stride=n_res2)
        m_s[rows, :] = m
        l_s[rows, :] = l
        a_s[rows, :] = o
        return carry

    lax.fori_loop(0, n_res2, group2, 0)

    def group1(r, carry):
        for n in range(n_blk1):
            q = q1[r, n * nb:(n + 1) * nb, :]
            if n == 0:
                m, l, o = _two_head_partial(q, k1[r, 0:nb, :], v1[r, 0:nb, :],
                                            b1[0, :, nb:], b1[1, :, nb:], is_a)
            else:
                m, l, o = _two_head_partial(q, k1[r, (n - 1) * nb:(n + 1) * nb, :],
                                            v1[r, (n - 1) * nb:(n + 1) * nb, :], b1[0], b1[1], is_a)
            rows = pl.ds(n * nb * n_res1 + r, nb, stride=n_res1)
            m, l, o = merge(rows, m, l, o)
            m_s[rows, :] = m
            l_s[rows, :] = l
            a_s[rows, :] = o
        return carry

    lax.fori_loop(0, n_res1, group1, 0)

    def finish(rows, m, l, o):
        _, l, o = merge(rows, m, l, o)
        o_ref[rows, :] = (o / l).astype(o_ref.dtype)

    m, l, o = _two_head_partial(q0[0:nb, :], k0[0:nb, :], v0[0:nb, :], b0[0, :, nb:], b0[1, :, nb:], is_a)
    finish(pl.ds(0, nb), m, l, o)

    def group0(n, carry):
        q_rows = pl.ds(pl.multiple_of(n * nb, nb), nb)
        k_rows = pl.ds(pl.multiple_of((n - 1) * nb, nb), 2 * nb)
        m, l, o = _two_head_partial(q0[q_rows, :], k0[k_rows, :], v0[k_rows, :], b0[0], b0[1], is_a)
        finish(q_rows, m, l, o)
        return carry

    lax.fori_loop(1, n_blk0, group0, 0)


def _dilated_attention(qkv, tables):
    bsz = qkv[0].shape[0]
    seq = qkv[0].shape[1] * qkv[0].shape[2]
    nt = N_HEAD_TILES
    in_specs, args = [], []
    for g, (_, dil) in enumerate(DIL_GROUPS):
        sub = seq // dil
        for part in range(3):
            if dil == 1:
                spec = pl.BlockSpec((None, None, sub, LANES), lambda t, b, part=part: (b, 0, 0, part * nt + t))
            else:
                spec = pl.BlockSpec((None, dil, sub, LANES), lambda t, b, part=part: (b, 0, 0, part * nt + t))
            in_specs.append(spec)
            args.append(qkv[g])
    for g in range(3):
        width = tables[g].shape[2]
        in_specs.append(pl.BlockSpec((HEADS_PER_TILE, DIL_BACK, width), lambda t, b: (t, 0, 0)))
        args.append(tables[g])
    return pl.pallas_call(
        _dilated_attn_kernel,
        grid=(nt, bsz),
        in_specs=in_specs,
        out_specs=pl.BlockSpec((None, seq, LANES), lambda t, b: (b, 0, t)),
        out_shape=jax.ShapeDtypeStruct((bsz, seq, ATTN_WIDTH), BF16),
        scratch_shapes=[pltpu.VMEM((seq, LANES), F32)] * 3,
        compiler_params=_params(("parallel", "parallel")),
        name="dilated_attn",
    )(*args)


def _moba_kernel(q_ref, k_ref, v_ref, km_ref, bias_ref, o_ref, kaug_s, nsel_s):
    seq = q_ref.shape[0]
    blk = MOBA_BLOCK
    n_blk = seq // blk
    is_a_row = lax.broadcasted_iota(I32, (n_blk, LANES), 1) < HEAD_DIM
    is_a = lax.broadcasted_iota(I32, (blk, LANES), 1) < HEAD_DIM

    kaug_s[:, :LANES] = k_ref[...]
    key_blk = lax.broadcasted_iota(I32, (seq, LANES), 0) // blk
    col = lax.broadcasted_iota(I32, (seq, LANES), 1)
    kaug_s[:, LANES:] = jnp.where(col == key_blk, NEG_INF, 0.0).astype(BF16)

    q_all = q_ref[...]
    km = km_ref[...]
    q_blk = lax.broadcasted_iota(I32, (n_blk, seq), 1) // blk
    m_idx = lax.broadcasted_iota(I32, (n_blk, seq), 0)
    past = m_idx < q_blk
    n_sel = min(MOBA_TOPK, n_blk - 1)
    pad_rows = jnp.zeros((LANES - n_blk, LANES), F32)
    for hd in range(HEADS_PER_TILE):
        kmh = jnp.where(is_a_row if hd == 0 else jnp.logical_not(is_a_row), km, 0.0)
        gate = sum(_dot_nt(piece, q_all) for piece in _split3(kmh))
        gate = jnp.where(past, gate, NEG_INF)
        rank = jnp.zeros((n_blk, seq), F32)
        for mp in range(n_blk - 1):
            row = gate[mp:mp + 1, :]
            beats = jnp.where(row > gate, 1.0, jnp.where(row == gate, jnp.where(mp < m_idx, 1.0, 0.0), 0.0))
            rank = rank + jnp.where(mp < q_blk, beats, 0.0)
        unselected = jnp.where(past, jnp.where(rank < n_sel, 0.0, 1.0), 1.0)
        unselected = jnp.where(m_idx == q_blk, 0.0, unselected)
        for c in range(seq // LANES):
            tile = jnp.concatenate([unselected[:, c * LANES:(c + 1) * LANES], pad_rows], axis=0)
            nsel_s[hd, c * LANES:(c + 1) * LANES, :] = tile.T.astype(BF16)

    def q_block(n, carry):
        rows = pl.ds(pl.multiple_of(n * blk, blk), blk)
        qn = q_ref[rows, :]
        zero = jnp.zeros_like(qn)
        outs = []
        for hd in range(HEADS_PER_TILE):
            qh = jnp.where(is_a, qn, zero) if hd == 0 else jnp.where(is_a, zero, qn)
            qa = jnp.concatenate([qh, nsel_s[hd, rows, :]], axis=1)

            def k_block(i, state, hd=hd, qa=qa):
                m_run, l_run, acc = state
                m = n - i
                k_rows = pl.ds(pl.multiple_of(m * blk, blk), blk)
                s = _dot_nt(qa, kaug_s[k_rows, :]) + bias_ref[hd, i]
                m_new = jnp.maximum(m_run, jnp.max(s, axis=-1, keepdims=True))
                scale = jnp.exp(m_run - m_new)
                p = jnp.exp(s - m_new)
                l_new = scale * l_run + jnp.sum(p, axis=-1, keepdims=True)
                acc_new = scale * acc + _dot(p.astype(BF16), v_ref[k_rows, :])
                return m_new, l_new, acc_new

            init = (jnp.full((blk, 1), NEG_INF, F32), jnp.zeros((blk, 1), F32), jnp.zeros((blk, LANES), F32))
            _, l_fin, acc_fin = lax.fori_loop(0, n + 1, k_block, init)
            outs.append(acc_fin / l_fin)
        o_ref[rows, :] = jnp.where(is_a, outs[0], outs[1]).astype(o_ref.dtype)
        return carry

    lax.fori_loop(0, n_blk, q_block, 0)


def _moba_attention(qkv, kmean, table):
    bsz, seq, _ = qkv.shape
    nt = N_HEAD_TILES
    n_blk = seq // MOBA_BLOCK
    return pl.pallas_call(
        _moba_kernel,
        grid=(nt, bsz),
        in_specs=[pl.BlockSpec((None, seq, LANES), lambda t, b: (b, 0, t)),
                  pl.BlockSpec((None, seq, LANES), lambda t, b: (b, 0, nt + t)),
                  pl.BlockSpec((None, seq, LANES), lambda t, b: (b, 0, 2 * nt + t)),
                  pl.BlockSpec((None, n_blk, LANES), lambda t, b: (b, 0, t)),
                  pl.BlockSpec((HEADS_PER_TILE, n_blk, MOBA_BLOCK, MOBA_BLOCK), lambda t, b: (t, 0, 0, 0))],
        out_specs=pl.BlockSpec((None, seq, LANES), lambda t, b: (b, 0, t)),
        out_shape=jax.ShapeDtypeStruct((bsz, seq, ATTN_WIDTH), BF16),
        scratch_shapes=[pltpu.VMEM((seq, 2 * LANES), BF16),
                        pltpu.VMEM((HEADS_PER_TILE, seq, LANES), BF16)],
        compiler_params=_params(("parallel", "parallel")),
        name="moba_attn",
    )(qkv, qkv, qkv, kmean, table)


def _attn_out_ffn_kernel(x_ref, a_ref, wo_ref, wg_ref, wu_ref, wd_ref, ln_ref, o_ref, *, n_split):
    x = x_ref[...]
    h = _layernorm(ALPHA * x + _dot(a_ref[...], wo_ref[...]), ln_ref[0:1, :], ln_ref[1:2, :])
    hb = h.astype(BF16)
    width = wg_ref.shape[1] // n_split
    y = None
    for c in range(n_split):
        cols = slice(c * width, (c + 1) * width)
        g = _dot(hb, wg_ref[:, cols])
        u = _dot(hb, wu_ref[:, cols])
        part = _dot((g * jax.nn.sigmoid(g) * u).astype(BF16), wd_ref[cols, :])
        y = part if y is None else y + part
    o_ref[...] = _layernorm(ALPHA * h + y, ln_ref[2:3, :], ln_ref[3:4, :])


def _attn_out_ffn(x, attn, w_out, w_gate, w_up, w_down, ln):
    t, dm = x.shape
    ff = w_gate.shape[1]
    tm = TOKEN_TILE
    n_split = 2 if (ff // 2) % LANES == 0 else 1
    return pl.pallas_call(
        functools.partial(_attn_out_ffn_kernel, n_split=n_split),
        grid=(t // tm,),
        in_specs=[pl.BlockSpec((tm, dm), lambda i: (i, 0)),
                  pl.BlockSpec((tm, attn.shape[1]), lambda i: (i, 0)),
                  _resident(w_out.shape), _resident(w_gate.shape), _resident(w_up.shape),
                  _resident(w_down.shape), _resident(ln.shape)],
        out_specs=pl.BlockSpec((tm, dm), lambda i: (i, 0)),
        out_shape=jax.ShapeDtypeStruct((t, dm), F32),
        compiler_params=_params(("parallel",)),
        name="attn_out_ffn",
    )(x, attn, w_out, w_gate, w_up, w_down, ln)


def _route(h, wr_ref, tri_ref, cnt_s, rec_ref, cnt_ref):
    tm = h.shape[0]
    h1, h2, _ = _split3(h)
    logits = _dot_nt(wr_ref[0], h1) + _dot_nt(wr_ref[1], h1) + _dot_nt(wr_ref[0], h2)
    e_idx = lax.broadcasted_iota(I32, (N_EXPERTS, tm), 0).astype(F32)
    top1 = jnp.max(logits, axis=0, keepdims=True)
    idx1 = jnp.min(jnp.where(logits == top1, e_idx, float(N_EXPERTS)), axis=0, keepdims=True)
    rest = jnp.where(e_idx == idx1, -jnp.inf, logits)
    top2 = jnp.max(rest, axis=0, keepdims=True)
    idx2 = jnp.min(jnp.where(rest == top2, e_idx, float(N_EXPERTS)), axis=0, keepdims=True)
    z = jnp.exp(top2 - top1)
    gate1 = 1.0 / (1.0 + z)
    gate2 = z / (1.0 + z)
    hot1 = jnp.where(e_idx == idx1, 1.0, 0.0)
    hot2 = jnp.where(e_idx == idx2, 1.0, 0.0)
    hot = hot1 + hot2
    before = _dot(hot.astype(BF16), tri_ref[...]) + cnt_s[:, 0:1]
    pos1 = jnp.sum(hot1 * before, axis=0, keepdims=True)
    pos2 = jnp.sum(hot2 * before, axis=0, keepdims=True)
    cnt_s[...] = cnt_s[...] + jnp.sum(hot, axis=1, keepdims=True)
    cnt_ref[...] = cnt_s[...]
    rec_ref[...] = jnp.concatenate([gate1, gate2, idx1, idx2, pos1, pos2, jnp.zeros((2, tm), F32)], axis=0)


def _conv_mixer_kernel(x_ref, wi_ref, cw_ref, wo_ref, ln_ref, wr_ref, tri_ref,
                       o_ref, rec_ref, cnt_ref, y_s, cnt_s):
    tm, dm = x_ref.shape
    first = (pl.program_id(0) == 0) & (pl.program_id(1) == 0)

    @pl.when(first)
    def _():
        cnt_s[...] = jnp.zeros_like(cnt_s)

    @pl.when(pl.program_id(1) == 0)
    def _():
        y_s[0:8, :] = jnp.zeros((8, dm), F32)

    x = x_ref[...]
    xb = x.astype(BF16)
    gate_c = _dot(xb, wi_ref[:, dm:2 * dm])
    x_in = _dot(xb, wi_ref[:, 2 * dm:])
    y_s[8:, :] = gate_c * x_in
    z = cw_ref[0:1, :] * y_s[6:6 + tm, :] + cw_ref[1:2, :] * y_s[7:7 + tm, :] + cw_ref[2:3, :] * y_s[8:, :]
    y_s[0:8, :] = y_s[tm:, :]
    gate_b = _dot(xb, wi_ref[:, :dm])
    mixed = _dot((gate_b * z).astype(BF16), wo_ref[...])
    h = _layernorm(ALPHA * x + mixed, ln_ref[0:1, :], ln_ref[1:2, :])
    o_ref[...] = h
    _route(h, wr_ref, tri_ref, cnt_s, rec_ref, cnt_ref)


def _conv_mixer(x, w_in, conv_w, w_out, ln, wr, tri):
    bsz, seq, dm = x.shape
    tm = TOKEN_TILE
    nj = seq // tm
    return pl.pallas_call(
        _conv_mixer_kernel,
        grid=(bsz, nj),
        in_specs=[pl.BlockSpec((None, tm, dm), lambda b, j: (b, j, 0)),
                  _resident(w_in.shape), _resident(conv_w.shape), _resident(w_out.shape),
                  _resident(ln.shape), _resident(wr.shape), _resident(tri.shape)],
        out_specs=[pl.BlockSpec((None, tm, dm), lambda b, j: (b, j, 0)),
                   pl.BlockSpec((8, tm), lambda b, j: (0, b * nj + j)),
                   pl.BlockSpec((N_EXPERTS, LANES), lambda b, j: (0, 0))],
        out_shape=[jax.ShapeDtypeStruct((bsz, seq, dm), F32),
                   jax.ShapeDtypeStruct((8, bsz * seq), F32),
                   jax.ShapeDtypeStruct((N_EXPERTS, LANES), F32)],
        scratch_shapes=[pltpu.VMEM((tm + 8, dm), F32), pltpu.VMEM((N_EXPERTS, LANES), F32)],
        compiler_params=_params(("arbitrary", "arbitrary")),
        name="conv_mixer",
    )(x, w_in, conv_w, w_out, ln, wr, tri)


def _gmlp_mixer_kernel(x_ref, wi_ref, dln_ref, ws_ref, bs_ref, wo_ref, ln_ref, wr_ref, tri_ref,
                       o_ref, rec_ref, cnt_ref, mix_s, cnt_s):
    tm, dm = x_ref.shape
    ch = GMLP_CHUNK
    gw = dm // GMLP_GROUPS

    @pl.when(pl.program_id(0) == 0)
    def _():
        cnt_s[...] = jnp.zeros_like(cnt_s)

    x = x_ref[...]
    xb = x.astype(BF16)
    u = _gelu(_dot(xb, wi_ref[:, :dm]))
    v = _gelu(_dot(xb, wi_ref[:, dm:]))
    vb = _layernorm(v, dln_ref[0:1, :], dln_ref[1:2, :]).astype(BF16)
    lower = lax.broadcasted_iota(I32, (ch, ch), 0) >= lax.broadcasted_iota(I32, (ch, ch), 1)
    for g in range(GMLP_GROUPS):
        ws = jnp.where(lower, ws_ref[g], 0.0).astype(BF16)
        for c in range(tm // ch):
            blk = _dot(ws, vb[c * ch:(c + 1) * ch, g * gw:(g + 1) * gw]) + bs_ref[:, g:g + 1]
            mix_s[c * ch:(c + 1) * ch, g * gw:(g + 1) * gw] = blk
    mixed = _dot((u * mix_s[...]).astype(BF16), wo_ref[...])
    h = _layernorm(ALPHA * x + mixed, ln_ref[0:1, :], ln_ref[1:2, :])
    o_ref[...] = h
    _route(h, wr_ref, tri_ref, cnt_s, rec_ref, cnt_ref)


def _gmlp_mixer(x, w_in, dln, w_s, b_s_t, w_out, ln, wr, tri):
    t, dm = x.shape
    tm = TOKEN_TILE
    return pl.pallas_call(
        _gmlp_mixer_kernel,
        grid=(t // tm,),
        in_specs=[pl.BlockSpec((tm, dm), lambda i: (i, 0)),
                  _resident(w_in.shape), _resident(dln.shape), _resident(w_s.shape),
                  _resident(b_s_t.shape), _resident(w_out.shape), _resident(ln.shape),
                  _resident(wr.shape), _resident(tri.shape)],
        out_specs=[pl.BlockSpec((tm, dm), lambda i: (i, 0)),
                   pl.BlockSpec((8, tm), lambda i: (0, i)),
                   pl.BlockSpec((N_EXPERTS, LANES), lambda i: (0, 0))],
        out_shape=[jax.ShapeDtypeStruct((t, dm), F32),
                   jax.ShapeDtypeStruct((8, t), F32),
                   jax.ShapeDtypeStruct((N_EXPERTS, LANES), F32)],
        scratch_shapes=[pltpu.VMEM((tm, dm), F32), pltpu.VMEM((N_EXPERTS, LANES), F32)],
        compiler_params=_params(("arbitrary",)),
        name="gmlp_mixer",
    )(x, w_in, dln, w_s, b_s_t, w_out, ln, wr, tri)


def _moe_expert_kernel(te_ref, nu_ref, src_hbm, dst_hbm, h_hbm, wg_ref, wu_ref, wd_ref, y_hbm,
                       src_s, dst_s, x_s, xb_s, acc_s, idx_sem, row_sem):
    i, f = pl.program_id(0), pl.program_id(1)
    n_f = pl.num_programs(1)
    tm = x_s.shape[0]
    used = i < nu_ref[0]

    def row_in(r, tok):
        return pltpu.make_async_copy(h_hbm.at[pl.ds(tok, 1), :], x_s.at[pl.ds(r, 1), :], row_sem)

    def row_out(r, row):
        return pltpu.make_async_copy(acc_s.at[pl.ds(r, 1), :], y_hbm.at[pl.ds(row, 1), :], row_sem)

    @pl.when(used & (f == 0))
    def _():
        c_src = pltpu.make_async_copy(src_hbm.at[i], src_s, idx_sem.at[0])
        c_dst = pltpu.make_async_copy(dst_hbm.at[i], dst_s, idx_sem.at[1])
        c_src.start()
        c_dst.start()
        c_src.wait()
        c_dst.wait()

        def start(r, carry):
            row_in(r, src_s[r]).start()
            return carry

        lax.fori_loop(0, tm, start, 0)

        def wait(r, carry):
            row_in(r, 0).wait()
            return carry

        lax.fori_loop(0, tm, wait, 0)
        xb_s[...] = x_s[...].astype(BF16)

    @pl.when(used)
    def _():
        xb = xb_s[...]
        g = _dot(xb, wg_ref[...])
        u = _dot(xb, wu_ref[...])
        y = _dot((g * jax.nn.sigmoid(g) * u).astype(BF16), wd_ref[...])

        @pl.when(f == 0)
        def _():
            acc_s[...] = y

        @pl.when(f > 0)
        def _():
            acc_s[...] = acc_s[...] + y

    @pl.when(used & (f == n_f - 1))
    def _():
        def start(r, carry):
            row_out(r, dst_s[r]).start()
            return carry

        lax.fori_loop(0, tm, start, 0)

        def wait(r, carry):
            row_out(r, 0).wait()
            return carry

        lax.fori_loop(0, tm, wait, 0)


def _moe_experts(h, src, dst, tile_expert, n_used, w_gate, w_up, w_down, n_rows_out):
    t, dm = h.shape
    n_tiles, tm = src.shape
    ff = w_gate.shape[2]
    tf = MOE_FF_TILE

    def col_block(i, f, te, nu):
        return te[i], 0, jnp.where(i < nu[0], f, 0)

    def row_block(i, f, te, nu):
        return te[i], jnp.where(i < nu[0], f, 0), 0

    grid_spec = pltpu.PrefetchScalarGridSpec(
        num_scalar_prefetch=2,
        grid=(n_tiles, ff // tf),
        in_specs=[pl.BlockSpec(memory_space=pl.ANY),
                  pl.BlockSpec(memory_space=pl.ANY),
                  pl.BlockSpec(memory_space=pl.ANY),
                  pl.BlockSpec((None, dm, tf), col_block),
                  pl.BlockSpec((None, dm, tf), col_block),
                  pl.BlockSpec((None, tf, dm), row_block)],
        out_specs=pl.BlockSpec(memory_space=pl.ANY),
        scratch_shapes=[pltpu.SMEM((tm,), I32), pltpu.SMEM((tm,), I32),
                        pltpu.VMEM((tm, dm), F32), pltpu.VMEM((tm, dm), BF16), pltpu.VMEM((tm, dm), F32),
                        pltpu.SemaphoreType.DMA((2,)), pltpu.SemaphoreType.DMA(())],
    )
    return pl.pallas_call(
        _moe_expert_kernel,
        grid_spec=grid_spec,
        out_shape=jax.ShapeDtypeStruct((n_rows_out, dm), F32),
        compiler_params=_params(("arbitrary", "arbitrary")),
        name="moe_experts",
    )(tile_expert, n_used, src, dst, h, w_gate, w_up, w_down)


def _moe_combine_kernel(h_ref, y1_ref, y2_ref, rec_ref, ln_ref, o_ref):
    tm = h_ref.shape[0]
    pad_rows = jnp.zeros((LANES - 8, LANES), F32)
    for c in range(tm // LANES):
        rows = slice(c * LANES, (c + 1) * LANES)
        gates = jnp.concatenate([rec_ref[:, rows], pad_rows], axis=0).T
        y = gates[:, 0:1] * y1_ref[rows, :] + gates[:, 1:2] * y2_ref[rows, :]
        o_ref[rows, :] = _layernorm(ALPHA * h_ref[rows, :] + y, ln_ref[0:1, :], ln_ref[1:2, :])


def _moe_combine(h, y, rec, ln, half_rows):
    t, dm = h.shape
    tm = TOKEN_TILE
    off = half_rows // tm
    return pl.pallas_call(
        _moe_combine_kernel,
        grid=(t // tm,),
        in_specs=[pl.BlockSpec((tm, dm), lambda i: (i, 0)),
                  pl.BlockSpec((tm, dm), lambda i: (i, 0)),
                  pl.BlockSpec((tm, dm), lambda i: (off + i, 0)),
                  pl.BlockSpec((8, tm), lambda i: (0, i)),
                  _resident(ln.shape)],
        out_specs=pl.BlockSpec((tm, dm), lambda i: (i, 0)),
        out_shape=jax.ShapeDtypeStruct((t, dm), F32),
        compiler_params=_params(("parallel",)),
        name="moe_combine",
    )(h, y, y, rec, ln)


def _moe(h, rec, counts, w_gate, w_up, w_down, ln):
    t, dm = h.shape
    tm = MOE_TILE
    n_tiles = (2 * t) // tm + N_EXPERTS
    half_rows = t + tm
    cnt = counts[:, 0].astype(I32)
    padded = (cnt + tm - 1) // tm * tm
    pad_end = jnp.cumsum(padded)
    pad_start = pad_end - padded
    tok = jnp.arange(t, dtype=I32)
    experts = jnp.arange(N_EXPERTS, dtype=I32)

    def slot_of(e_row, pos_row):
        e = e_row.astype(I32)
        start = jnp.sum(jnp.where(e[None, :] == experts[:, None], pad_start[:, None], 0), axis=0)
        return start + pos_row.astype(I32)

    slot1, slot2 = slot_of(rec[2], rec[4]), slot_of(rec[3], rec[5])
    n_slots = n_tiles * tm
    src = jnp.zeros((n_slots,), I32).at[slot1].set(tok).at[slot2].set(tok)
    spare = t + jnp.arange(n_slots, dtype=I32) % tm
    dst = spare.at[slot1].set(tok).at[slot2].set(half_rows + tok)
    tile_start = jnp.arange(n_tiles, dtype=I32) * tm
    tile_expert = jnp.minimum(jnp.sum(tile_start[:, None] >= pad_end[None, :], axis=1), N_EXPERTS - 1).astype(I32)
    n_used = (pad_end[-1] // tm).astype(I32).reshape(1)
    y = _moe_experts(h, src.reshape(n_tiles, tm), dst.reshape(n_tiles, tm), tile_expert, n_used,
                     w_gate, w_up, w_down, 2 * half_rows)
    return _moe_combine(h, y, rec, ln, half_rows)


def kernel(x, rel_bias, ln_gain, ln_bias, a_w_in, a_w_out, b_w_in, b_conv, b_w_out, c_w_in, c_w_out,
           d_w_in, d_ln_gain, d_ln_bias, d_w_s, d_b_s, d_w_out, f_w_gate, f_w_up, f_w_down,
           e_w_router, e_w_gate, e_w_up, e_w_down):
    bsz, seq, dm = x.shape
    t = bsz * seq
    assert seq % (8 * MOBA_BLOCK) == 0 and seq == DIL_GROUPS[-1][0] and ln_gain.shape[0] == DEPTH
    assert seq % TOKEN_TILE == 0 and t % MOE_TILE == 0
    q_scale = HEAD_DIM ** -0.5

    def ln_rows(i, *pairs):
        return jnp.concatenate([jnp.stack([ln_gain[i, p], ln_bias[i, p]]) for p in pairs], axis=0)

    def router_weights(w):
        w1, w2, _ = _split3(w.T.astype(F32))
        return jnp.stack([w1, w2])

    tri = jnp.triu(jnp.ones((TOKEN_TILE, TOKEN_TILE), BF16), k=1)

    def scaled_qkv(w):
        scale = jnp.concatenate([jnp.full((ATTN_WIDTH,), q_scale, F32), jnp.ones((2 * ATTN_WIDTH,), F32)])
        return (w * scale).astype(BF16)

    h = x
    group_cols = 3 * ATTN_WIDTH
    qkv = [_project_dilated(h, scaled_qkv(a_w_in[0][:, g * group_cols:(g + 1) * group_cols]), dil)
           for g, (_, dil) in enumerate(DIL_GROUPS)]
    tables = [_dilated_bias_table(rel_bias, dil) for _, dil in DIL_GROUPS]
    tables[2] = tables[2][:, :, DIL_BACK:]
    attn = _dilated_attention(qkv, tables)
    h = _attn_out_ffn(h.reshape(t, dm), attn.reshape(t, ATTN_WIDTH), a_w_out[0].astype(BF16),
                      f_w_gate[0].astype(BF16), f_w_up[0].astype(BF16), f_w_down[0].astype(BF16),
                      ln_rows(0, 0, 1))
    h, rec, counts = _conv_mixer(h.reshape(bsz, seq, dm), b_w_in[0].astype(BF16), b_conv[0],
                                 b_w_out[0].astype(BF16), ln_rows(1, 0), router_weights(e_w_router[0]), tri)
    h = _moe(h.reshape(t, dm), rec, counts, e_w_gate[0].astype(BF16), e_w_up[0].astype(BF16),
             e_w_down[0].astype(BF16), ln_rows(1, 1))
    qkv_c, kmean = _project_moba(h.reshape(bsz, seq, dm), scaled_qkv(c_w_in[0]))
    attn = _moba_attention(qkv_c, kmean, _moba_bias_table(rel_bias, seq // MOBA_BLOCK))
    h = _attn_out_ffn(h, attn.reshape(t, ATTN_WIDTH), c_w_out[0].astype(BF16),
                      f_w_gate[1].astype(BF16), f_w_up[1].astype(BF16), f_w_down[1].astype(BF16),
                      ln_rows(2, 0, 1))
    h, rec, counts = _gmlp_mixer(h, d_w_in[0].astype(BF16), jnp.stack([d_ln_gain[0], d_ln_bias[0]]),
                                 d_w_s[0], d_b_s[0].T, d_w_out[0].astype(BF16), ln_rows(3, 0),
                                 router_weights(e_w_router[1]), tri)
    h = _moe(h, rec, counts, e_w_gate[1].astype(BF16), e_w_up[1].astype(BF16),
             e_w_down[1].astype(BF16), ln_rows(3, 1))
    return h.reshape(bsz, seq, dm)
```

```python
import functools
import math

import jax
import jax.numpy as jnp
from jax import lax
from jax.experimental import pallas as pl
from jax.experimental.pallas import tpu as pltpu

F32 = jnp.float32
BF16 = jnp.bfloat16
I32 = jnp.int32

N_HEADS = 16
HEAD_DIM = 64
LANES = 128
HEADS_PER_TILE = LANES // HEAD_DIM
N_HEAD_TILES = N_HEADS // HEADS_PER_TILE
ATTN_WIDTH = N_HEADS * HEAD_DIM
NUM_BUCKETS = 32
MAX_DISTANCE = 2048
DIL_GROUPS = ((128, 1), (512, 4), (2048, 16))
DIL_BACK = 128
DIL_UNROLL = 4
MOBA_BLOCK = 256
MOBA_TOPK = 3
GMLP_CHUNK = 128
GMLP_GROUPS = 8
N_EXPERTS = 8
DEPTH = 4
ALPHA = (2 * DEPTH) ** 0.25
LN_EPS = 1e-5
NEG_INF = -1e30

TOKEN_TILE = 512
MOE_TILE = 1024
MOE_FF_TILE = 512
VMEM_LIMIT = 56 * 1024 * 1024


def _params(semantics, **kw):
    return pltpu.CompilerParams(dimension_semantics=semantics, vmem_limit_bytes=VMEM_LIMIT, **kw)


def _resident(shape):
    return pl.BlockSpec(shape, lambda *_: (0,) * len(shape), pipeline_mode=pl.Buffered(1))


def _layernorm(z, gain, bias):
    mu = jnp.mean(z, axis=-1, keepdims=True)
    zc = z - mu
    var = jnp.mean(zc * zc, axis=-1, keepdims=True)
    return zc * lax.rsqrt(var + LN_EPS) * gain + bias


def _gelu(x):
    return 0.5 * x * (1.0 + lax.erf(x * math.sqrt(0.5)))


def _dot(a, b):
    return jnp.dot(a, b, preferred_element_type=F32)


def _dot_nt(a, b):
    return lax.dot_general(a, b, (((1,), (1,)), ((), ())), preferred_element_type=F32)


def _split3(x):
    x1 = x.astype(BF16)
    r1 = x - x1.astype(F32)
    x2 = r1.astype(BF16)
    x3 = (r1 - x2.astype(F32)).astype(BF16)
    return x1, x2, x3


def _t5_bucket(dist):
    max_exact = NUM_BUCKETS // 2
    d = jnp.maximum(dist, 0)
    log_ratio = jnp.log(jnp.maximum(d, 1).astype(F32) / max_exact) / math.log(MAX_DISTANCE / max_exact)
    large = max_exact + (log_ratio * (NUM_BUCKETS - max_exact)).astype(I32)
    large = jnp.clip(large, max_exact, NUM_BUCKETS - 1)
    return jnp.where(d < max_exact, d, large)


def _bias_lookup(rel_bias, dist):
    onehot = (_t5_bucket(dist)[..., None] == jnp.arange(NUM_BUCKETS, dtype=I32)).astype(F32)
    out = jnp.einsum('...k,kh->...h', onehot, rel_bias.astype(F32), precision=lax.Precision.HIGHEST)
    return jnp.moveaxis(out, -1, 0)


def _dilated_bias_table(rel_bias, dil):
    qi = jnp.arange(DIL_BACK, dtype=I32)[:, None]
    kj = jnp.arange(2 * DIL_BACK, dtype=I32)[None, :]
    delta = qi + DIL_BACK - kj
    band = (delta >= 0) & (delta <= DIL_BACK)
    bias = _bias_lookup(rel_bias, delta * dil)
    return jnp.where(band[None], bias, NEG_INF)


def _moba_bias_table(rel_bias, n_blk):
    pos = jnp.arange(MOBA_BLOCK, dtype=I32)
    diff = pos[:, None] - pos[None, :]
    blocks_back = jnp.arange(n_blk - 1, -1, -1, dtype=I32)
    dist = blocks_back[None, :, None] * MOBA_BLOCK + diff[:, None, :]
    bias = jnp.where((dist >= 0)[None], _bias_lookup(rel_bias, dist), NEG_INF)
    return bias.reshape(N_HEADS, MOBA_BLOCK, n_blk * MOBA_BLOCK)


def _proj_kernel(*refs, dil, n_split):
    x_refs, w_ref, o_ref = refs[:-2], refs[-2], refs[-1]
    tm = x_refs[0].shape[0]
    rows = tm // dil
    if dil == 1:
        xb = jnp.concatenate([x_ref[...] for x_ref in x_refs], axis=1).astype(BF16)
    else:
        xb = jnp.concatenate(
            [jnp.concatenate([x_ref[pl.ds(r, rows, stride=dil), :] for x_ref in x_refs], axis=1)
             for r in range(dil)], axis=0).astype(BF16)
    width = w_ref.shape[1] // n_split
    for c in range(n_split):
        y = _dot(xb, w_ref[:, c * width:(c + 1) * width]).astype(BF16)
        for r in range(dil):
            o_ref[r, :, c * width:(c + 1) * width] = y[r * rows:(r + 1) * rows]


def _project_dilated(x, w, dil):
    bsz, seq, dm = x.shape
    n = w.shape[1]
    tm = TOKEN_TILE
    return pl.pallas_call(
        functools.partial(_proj_kernel, dil=dil, n_split=3),
        grid=(bsz, seq // tm),
        in_specs=[pl.BlockSpec((None, tm, LANES), lambda b, j, c=c: (b, j, c)) for c in range(dm // LANES)]
        + [_resident((dm, n))],
        out_specs=pl.BlockSpec((None, dil, tm // dil, n), lambda b, j: (b, 0, j, 0)),
        out_shape=jax.ShapeDtypeStruct((bsz, dil, seq // dil, n), BF16),
        compiler_params=_params(("parallel", "parallel")),
        name=f"proj_dil{dil}",
    )(*([x] * (dm // LANES)), w)


def _proj_kmean_kernel(x_ref, w_ref, o_ref, km_ref):
    tm = x_ref.shape[0]
    per_step = tm // MOBA_BLOCK
    j = pl.program_id(1)
    xb = x_ref[...].astype(BF16)
    width = ATTN_WIDTH
    for c in range(3):
        y = _dot(xb, w_ref[:, c * width:(c + 1) * width])
        o_ref[:, c * width:(c + 1) * width] = y.astype(BF16)
        if c == 1:
            for jj in range(per_step):
                km_ref[pl.ds(j * per_step + jj, 1), :] = jnp.mean(
                    y[jj * MOBA_BLOCK:(jj + 1) * MOBA_BLOCK], axis=0, keepdims=True)


def _project_moba(x, w):
    bsz, seq, dm = x.shape
    n = w.shape[1]
    tm = TOKEN_TILE
    n_blk = seq // MOBA_BLOCK
    return pl.pallas_call(
        _proj_kmean_kernel,
        grid=(bsz, seq // tm),
        in_specs=[pl.BlockSpec((None, tm, dm), lambda b, j: (b, j, 0)),
                  _resident((dm, n))],
        out_specs=[pl.BlockSpec((None, tm, n), lambda b, j: (b, j, 0)),
                   pl.BlockSpec((None, n_blk, ATTN_WIDTH), lambda b, j: (b, 0, 0))],
        out_shape=[jax.ShapeDtypeStruct((bsz, seq, n), BF16),
                   jax.ShapeDtypeStruct((bsz, n_blk, ATTN_WIDTH), F32)],
        compiler_params=_params(("parallel", "arbitrary")),
        name="proj_moba",
    )(x, w)


def _two_head_partials(units, is_a):
    scores = []
    for q, kw, _, bias_a, bias_b in units:
        zero = jnp.zeros_like(q)
        scores.append((_dot_nt(jnp.where(is_a, q, zero), kw) + bias_a,
                       _dot_nt(jnp.where(is_a, zero, q), kw) + bias_b))
    stats = []
    for pair in scores:
        row = []
        for s in pair:
            m = jnp.max(s, axis=-1, keepdims=True)
            p = jnp.exp(s - m)
            row.append((m, jnp.sum(p, axis=-1, keepdims=True), p.astype(BF16)))
        stats.append(row)
    out = []
    for (_, _, vw, _, _), ((m_a, l_a, p_a), (m_b, l_b, p_b)) in zip(units, stats):
        out.append((jnp.where(is_a, m_a, m_b), jnp.where(is_a, l_a, l_b),
                    jnp.where(is_a, _dot(p_a, vw), _dot(p_b, vw))))
    return out


def _dilated_attn_kernel(q0, k0, v0, q1, k1, v1, q2, k2, v2, b0, b1, b2, o_ref, m_s, l_s, a_s, tmp_s):
    nb = DIL_BACK
    is_a = lax.broadcasted_iota(I32, (nb, LANES), 1) < HEAD_DIM
    n_res2, n_res1 = q2.shape[0], q1.shape[0]
    n_blk1 = q1.shape[1] // nb
    n_blk0 = q0.shape[0] // nb

    def merge(rows, m, l, o):
        m_old, l_old, a_old = m_s[rows, :], l_s[rows, :], a_s[rows, :]
        m_new = jnp.maximum(m_old, m)
        w_old, w_new = jnp.exp(m_old - m_new), jnp.exp(m - m_new)
        return m_new, w_old * l_old + w_new * l, w_old * a_old + w_new * o

    unroll = DIL_UNROLL

    pitch = tmp_s.shape[1] // n_res2

    def group2(i, carry):
        res = [i * unroll + u for u in range(unroll)]
        parts = _two_head_partials([(q2[r], k2[r], v2[r], b2[0], b2[1]) for r in res], is_a)
        for r, part in zip(res, parts):
            rows = pl.ds(pl.multiple_of(r * pitch, 8), nb)
            for c in range(3):
                tmp_s[c, rows, :] = part[c]
        return carry

    lax.fori_loop(0, n_res2 // unroll, group2, 0)

    def to_token_major(s, carry):
        dst = pl.ds(pl.multiple_of(s * n_res2, n_res2), n_res2)
        for c, ref in enumerate((m_s, l_s, a_s)):
            ref[dst, :] = tmp_s[c, pl.ds(s, n_res2, stride=pitch), :]
        return carry

    lax.fori_loop(0, nb, to_token_major, 0, unroll=8)

    def group1(r, carry):
        units = [(q1[r, 0:nb, :], k1[r, 0:nb, :], v1[r, 0:nb, :], b1[0, :, nb:], b1[1, :, nb:])]
        for n in range(1, n_blk1):
            keys = slice((n - 1) * nb, (n + 1) * nb)
            units.append((q1[r, n * nb:(n + 1) * nb, :], k1[r, keys, :], v1[r, keys, :], b1[0], b1[1]))
        for n, (m, l, o) in enumerate(_two_head_partials(units, is_a)):
            rows = pl.ds(n * nb * n_res1 + r, nb, stride=n_res1)
            m, l, o = merge(rows, m, l, o)
            m_s[rows, :] = m
            l_s[rows, :] = l
            a_s[rows, :] = o
        return carry

    lax.fori_loop(0, n_res1, group1, 0)

    def finish(rows, m, l, o):
        _, l, o = merge(rows, m, l, o)
        o_ref[rows, :] = (o / l).astype(o_ref.dtype)

    def unit0(n):
        aligned = (lambda v: v) if isinstance(n, int) else (lambda v: pl.multiple_of(v, nb))
        q_rows = pl.ds(aligned(n * nb), nb)
        k_rows = pl.ds(aligned((n - 1) * nb), 2 * nb)
        return q_rows, (q0[q_rows, :], k0[k_rows, :], v0[k_rows, :], b0[0], b0[1])

    def blocks0(units):
        for (q_rows, _), (m, l, o) in zip(units, _two_head_partials([u for _, u in units], is_a)):
            finish(q_rows, m, l, o)

    first = (pl.ds(0, nb), (q0[0:nb, :], k0[0:nb, :], v0[0:nb, :], b0[0, :, nb:], b0[1, :, nb:]))
    blocks0([first] + [unit0(n) for n in range(1, unroll)])

    def group0(i, carry):
        blocks0([unit0(i * unroll + u) for u in range(unroll)])
        return carry

    lax.fori_loop(1, n_blk0 // unroll, group0, 0)


def _dilated_attention(qkv, tables):
    bsz = qkv[0].shape[0]
    seq = qkv[0].shape[1] * qkv[0].shape[2]
    nt = N_HEAD_TILES
    in_specs, args = [], []
    for g, (_, dil) in enumerate(DIL_GROUPS):
        sub = seq // dil
        for part in range(3):
            if dil == 1:
                spec = pl.BlockSpec((None, None, sub, LANES), lambda t, b, part=part: (b, 0, 0, part * nt + t))
            else:
                spec = pl.BlockSpec((None, dil, sub, LANES), lambda t, b, part=part: (b, 0, 0, part * nt + t))
            in_specs.append(spec)
            args.append(qkv[g])
    for g in range(3):
        width = tables[g].shape[2]
        in_specs.append(pl.BlockSpec((HEADS_PER_TILE, DIL_BACK, width), lambda t, b: (t, 0, 0)))
        args.append(tables[g])
    return pl.pallas_call(
        _dilated_attn_kernel,
        grid=(nt, bsz),
        in_specs=in_specs,
        out_specs=pl.BlockSpec((None, seq, LANES), lambda t, b: (b, 0, t)),
        out_shape=jax.ShapeDtypeStruct((bsz, seq, ATTN_WIDTH), BF16),
        scratch_shapes=[pltpu.VMEM((seq, LANES), F32)] * 3
        + [pltpu.VMEM((3, DIL_GROUPS[2][1] * (DIL_BACK + 8), LANES), F32)],
        compiler_params=_params(("parallel", "parallel")),
        name="dilated_attn",
    )(*args)


def _moba_kernel(q_ref, k_ref, v_ref, km_ref, bias_ref, o_ref, kaug_s, nsel_s):
    seq = q_ref.shape[0]
    blk = MOBA_BLOCK
    n_blk = seq // blk
    is_a_row = lax.broadcasted_iota(I32, (n_blk, LANES), 1) < HEAD_DIM
    is_a = lax.broadcasted_iota(I32, (blk, LANES), 1) < HEAD_DIM

    kaug_s[:, :LANES] = k_ref[...]
    key_blk = lax.broadcasted_iota(I32, (seq, LANES), 0) // blk
    col = lax.broadcasted_iota(I32, (seq, LANES), 1)
    kaug_s[:, LANES:] = jnp.where(col == key_blk, NEG_INF, 0.0).astype(BF16)

    q_all = q_ref[...]
    km = km_ref[...]
    q_blk = lax.broadcasted_iota(I32, (n_blk, seq), 1) // blk
    m_idx = lax.broadcasted_iota(I32, (n_blk, seq), 0)
    past = m_idx < q_blk
    n_sel = min(MOBA_TOPK, n_blk - 1)
    pad_rows = jnp.zeros((LANES - n_blk, LANES), F32)
    for hd in range(HEADS_PER_TILE):
        kmh = jnp.where(is_a_row if hd == 0 else jnp.logical_not(is_a_row), km, 0.0)
        gate = sum(_dot_nt(piece, q_all) for piece in _split3(kmh))
        gate = jnp.where(past, gate, NEG_INF)
        rank = jnp.zeros((n_blk, seq), F32)
        for mp in range(n_blk - 1):
            row = gate[mp:mp + 1, :]
            beats = jnp.where(row > gate, 1.0, jnp.where(row == gate, jnp.where(mp < m_idx, 1.0, 0.0), 0.0))
            rank = rank + jnp.where(mp < q_blk, beats, 0.0)
        unselected = jnp.where(past, jnp.where(rank < n_sel, 0.0, 1.0), 1.0)
        unselected = jnp.where(m_idx == q_blk, 0.0, unselected)
        for c in range(seq // LANES):
            tile = jnp.concatenate([unselected[:, c * LANES:(c + 1) * LANES], pad_rows], axis=0)
            nsel_s[hd, c * LANES:(c + 1) * LANES, :] = tile.T.astype(BF16)

    for n in range(n_blk):
        rows = slice(n * blk, (n + 1) * blk)
        keys = slice(0, (n + 1) * blk)
        qn = q_ref[rows, :]
        zero = jnp.zeros_like(qn)
        scores = []
        for hd in range(HEADS_PER_TILE):
            qh = jnp.where(is_a, qn, zero) if hd == 0 else jnp.where(is_a, zero, qn)
            qa = jnp.concatenate([qh, nsel_s[hd, rows, :]], axis=1)
            scores.append(_dot_nt(qa, kaug_s[keys, :]) + bias_ref[hd, :, (n_blk - 1 - n) * blk:])
        probs = []
        for s in scores:
            p = jnp.exp(s - jnp.max(s, axis=-1, keepdims=True))
            probs.append((p.astype(BF16), jnp.sum(p, axis=-1, keepdims=True)))
        outs = [_dot(p, v_ref[keys, :]) / l for p, l in probs]
        o_ref[rows, :] = jnp.where(is_a, outs[0], outs[1]).astype(o_ref.dtype)


def _moba_attention(qkv, kmean, table):
    bsz, seq, _ = qkv.shape
    nt = N_HEAD_TILES
    n_blk = seq // MOBA_BLOCK
    return pl.pallas_call(
        _moba_kernel,
        grid=(nt, bsz),
        in_specs=[pl.BlockSpec((None, seq, LANES), lambda t, b: (b, 0, t)),
                  pl.BlockSpec((None, seq, LANES), lambda t, b: (b, 0, nt + t)),
                  pl.BlockSpec((None, seq, LANES), lambda t, b: (b, 0, 2 * nt + t)),
                  pl.BlockSpec((None, n_blk, LANES), lambda t, b: (b, 0, t)),
                  pl.BlockSpec((HEADS_PER_TILE, MOBA_BLOCK, seq), lambda t, b: (t, 0, 0))],
        out_specs=pl.BlockSpec((None, seq, LANES), lambda t, b: (b, 0, t)),
        out_shape=jax.ShapeDtypeStruct((bsz, seq, ATTN_WIDTH), BF16),
        scratch_shapes=[pltpu.VMEM((seq, 2 * LANES), BF16),
                        pltpu.VMEM((HEADS_PER_TILE, seq, LANES), BF16)],
        compiler_params=_params(("parallel", "parallel")),
        name="moba_attn",
    )(qkv, qkv, qkv, kmean, table)


def _attn_out_ffn_kernel(x_ref, a_ref, wo_ref, wg_ref, wu_ref, wd_ref, ln_ref, o_ref, *, n_split):
    x = x_ref[...]
    h = _layernorm(ALPHA * x + _dot(a_ref[...], wo_ref[...]), ln_ref[0:1, :], ln_ref[1:2, :])
    hb = h.astype(BF16)
    width = wg_ref.shape[1] // n_split
    y = None
    for c in range(n_split):
        cols = slice(c * width, (c + 1) * width)
        g = _dot(hb, wg_ref[:, cols])
        u = _dot(hb, wu_ref[:, cols])
        part = _dot((g * jax.nn.sigmoid(g) * u).astype(BF16), wd_ref[cols, :])
        y = part if y is None else y + part
    o_ref[...] = _layernorm(ALPHA * h + y, ln_ref[2:3, :], ln_ref[3:4, :])


def _attn_out_ffn(x, attn, w_out, w_gate, w_up, w_down, ln):
    t, dm = x.shape
    ff = w_gate.shape[1]
    tm = TOKEN_TILE
    n_split = 2 if (ff // 2) % LANES == 0 else 1
    return pl.pallas_call(
        functools.partial(_attn_out_ffn_kernel, n_split=n_split),
        grid=(t // tm,),
        in_specs=[pl.BlockSpec((tm, dm), lambda i: (i, 0)),
                  pl.BlockSpec((tm, attn.shape[1]), lambda i: (i, 0)),
                  _resident(w_out.shape), _resident(w_gate.shape), _resident(w_up.shape),
                  _resident(w_down.shape), _resident(ln.shape)],
        out_specs=pl.BlockSpec((tm, dm), lambda i: (i, 0)),
        out_shape=jax.ShapeDtypeStruct((t, dm), F32),
        compiler_params=_params(("parallel",)),
        name="attn_out_ffn",
    )(x, attn, w_out, w_gate, w_up, w_down, ln)


def _route(h, wr_ref, tri_ref, cnt_s, rec_ref, cnt_ref):
    tm = h.shape[0]
    h1, h2, _ = _split3(h)
    logits = _dot_nt(wr_ref[0], h1) + _dot_nt(wr_ref[1], h1) + _dot_nt(wr_ref[0], h2)
    e_idx = lax.broadcasted_iota(I32, (N_EXPERTS, tm), 0).astype(F32)
    top1 = jnp.max(logits, axis=0, keepdims=True)
    idx1 = jnp.min(jnp.where(logits == top1, e_idx, float(N_EXPERTS)), axis=0, keepdims=True)
    rest = jnp.where(e_idx == idx1, -jnp.inf, logits)
    top2 = jnp.max(rest, axis=0, keepdims=True)
    idx2 = jnp.min(jnp.where(rest == top2, e_idx, float(N_EXPERTS)), axis=0, keepdims=True)
    z = jnp.exp(top2 - top1)
    gate1 = 1.0 / (1.0 + z)
    gate2 = z / (1.0 + z)
    hot1 = jnp.where(e_idx == idx1, 1.0, 0.0)
    hot2 = jnp.where(e_idx == idx2, 1.0, 0.0)
    hot = hot1 + hot2
    before = _dot(hot.astype(BF16), tri_ref[...]) + cnt_s[:, 0:1]
    pos1 = jnp.sum(hot1 * before, axis=0, keepdims=True)
    pos2 = jnp.sum(hot2 * before, axis=0, keepdims=True)
    cnt_s[...] = cnt_s[...] + jnp.sum(hot, axis=1, keepdims=True)
    cnt_ref[...] = cnt_s[...]
    rec_ref[...] = jnp.concatenate([gate1, gate2, idx1, idx2, pos1, pos2, jnp.zeros((2, tm), F32)], axis=0)


def _conv_mixer_kernel(x_ref, wi_ref, cw_ref, wo_ref, ln_ref, wr_ref, tri_ref,
                       o_ref, rec_ref, cnt_ref, y_s, cnt_s):
    tm, dm = x_ref.shape
    first = (pl.program_id(0) == 0) & (pl.program_id(1) == 0)

    @pl.when(first)
    def _():
        cnt_s[...] = jnp.zeros_like(cnt_s)

    @pl.when(pl.program_id(1) == 0)
    def _():
        y_s[0:8, :] = jnp.zeros((8, dm), F32)

    x = x_ref[...]
    xb = x.astype(BF16)
    gate_c = _dot(xb, wi_ref[:, dm:2 * dm])
    x_in = _dot(xb, wi_ref[:, 2 * dm:])
    y_s[8:, :] = gate_c * x_in
    z = cw_ref[0:1, :] * y_s[6:6 + tm, :] + cw_ref[1:2, :] * y_s[7:7 + tm, :] + cw_ref[2:3, :] * y_s[8:, :]
    y_s[0:8, :] = y_s[tm:, :]
    gate_b = _dot(xb, wi_ref[:, :dm])
    mixed = _dot((gate_b * z).astype(BF16), wo_ref[...])
    h = _layernorm(ALPHA * x + mixed, ln_ref[0:1, :], ln_ref[1:2, :])
    o_ref[...] = h
    _route(h, wr_ref, tri_ref, cnt_s, rec_ref, cnt_ref)


def _conv_mixer(x, w_in, conv_w, w_out, ln, wr, tri):
    bsz, seq, dm = x.shape
    tm = TOKEN_TILE
    nj = seq // tm
    return pl.pallas_call(
        _conv_mixer_kernel,
        grid=(bsz, nj),
        in_specs=[pl.BlockSpec((None, tm, dm), lambda b, j: (b, j, 0)),
                  _resident(w_in.shape), _resident(conv_w.shape), _resident(w_out.shape),
                  _resident(ln.shape), _resident(wr.shape), _resident(tri.shape)],
        out_specs=[pl.BlockSpec((None, tm, dm), lambda b, j: (b, j, 0)),
                   pl.BlockSpec((8, tm), lambda b, j: (0, b * nj + j)),
                   pl.BlockSpec((N_EXPERTS, LANES), lambda b, j: (0, 0))],
        out_shape=[jax.ShapeDtypeStruct((bsz, seq, dm), F32),
                   jax.ShapeDtypeStruct((8, bsz * seq), F32),
                   jax.ShapeDtypeStruct((N_EXPERTS, LANES), F32)],
        scratch_shapes=[pltpu.VMEM((tm + 8, dm), F32), pltpu.VMEM((N_EXPERTS, LANES), F32)],
        compiler_params=_params(("arbitrary", "arbitrary")),
        name="conv_mixer",
    )(x, w_in, conv_w, w_out, ln, wr, tri)


def _gmlp_mixer_kernel(x_ref, wi_ref, dln_ref, ws_ref, bs_ref, wo_ref, ln_ref, wr_ref, tri_ref,
                       o_ref, rec_ref, cnt_ref, mix_s, cnt_s):
    tm, dm = x_ref.shape
    ch = GMLP_CHUNK
    gw = dm // GMLP_GROUPS

    @pl.when(pl.program_id(0) == 0)
    def _():
        cnt_s[...] = jnp.zeros_like(cnt_s)

    x = x_ref[...]
    xb = x.astype(BF16)
    u = _gelu(_dot(xb, wi_ref[:, :dm]))
    v = _gelu(_dot(xb, wi_ref[:, dm:]))
    vb = _layernorm(v, dln_ref[0:1, :], dln_ref[1:2, :]).astype(BF16)
    lower = lax.broadcasted_iota(I32, (ch, ch), 0) >= lax.broadcasted_iota(I32, (ch, ch), 1)
    for g in range(GMLP_GROUPS):
        ws = jnp.where(lower, ws_ref[g], 0.0).astype(BF16)
        for c in range(tm // ch):
            blk = _dot(ws, vb[c * ch:(c + 1) * ch, g * gw:(g + 1) * gw]) + bs_ref[:, g:g + 1]
            mix_s[c * ch:(c + 1) * ch, g * gw:(g + 1) * gw] = blk
    mixed = _dot((u * mix_s[...]).astype(BF16), wo_ref[...])
    h = _layernorm(ALPHA * x + mixed, ln_ref[0:1, :], ln_ref[1:2, :])
    o_ref[...] = h
    _route(h, wr_ref, tri_ref, cnt_s, rec_ref, cnt_ref)


def _gmlp_mixer(x, w_in, dln, w_s, b_s_t, w_out, ln, wr, tri):
    t, dm = x.shape
    tm = TOKEN_TILE
    return pl.pallas_call(
        _gmlp_mixer_kernel,
        grid=(t // tm,),
        in_specs=[pl.BlockSpec((tm, dm), lambda i: (i, 0)),
                  _resident(w_in.shape), _resident(dln.shape), _resident(w_s.shape),
                  _resident(b_s_t.shape), _resident(w_out.shape), _resident(ln.shape),
                  _resident(wr.shape), _resident(tri.shape)],
        out_specs=[pl.BlockSpec((tm, dm), lambda i: (i, 0)),
                   pl.BlockSpec((8, tm), lambda i: (0, i)),
                   pl.BlockSpec((N_EXPERTS, LANES), lambda i: (0, 0))],
        out_shape=[jax.ShapeDtypeStruct((t, dm), F32),
                   jax.ShapeDtypeStruct((8, t), F32),
                   jax.ShapeDtypeStruct((N_EXPERTS, LANES), F32)],
        scratch_shapes=[pltpu.VMEM((tm, dm), F32), pltpu.VMEM((N_EXPERTS, LANES), F32)],
        compiler_params=_params(("arbitrary",)),
        name="gmlp_mixer",
    )(x, w_in, dln, w_s, b_s_t, w_out, ln, wr, tri)


def _load_slots(slot_hbm, slot_s, sem, i):
    copies = [pltpu.make_async_copy(slot_hbm.at[c, i], slot_s.at[c], sem.at[c]) for c in range(2)]
    for cp in copies:
        cp.start()
    for cp in copies:
        cp.wait()


def _moe_dispatch_kernel(fs_ref, fc_ref, slot_hbm, h_ref, x_hbm, slot_s, zero_s, idx_sem, row_sem):
    i = pl.program_id(0)
    tm = h_ref.shape[0]
    _load_slots(slot_hbm, slot_s, idx_sem, i)

    def row_copy(r, c, slot):
        return pltpu.make_async_copy(h_ref.at[pl.ds(r, 1), :], x_hbm.at[pl.ds(slot, 1), :], row_sem.at[c])

    def start(r, carry):
        row_copy(r, 0, slot_s[0, r]).start()
        row_copy(r, 1, slot_s[1, r]).start()
        return carry

    lax.fori_loop(0, tm, start, 0, unroll=8)

    @pl.when(i == 0)
    def _():
        zero_s[...] = jnp.zeros_like(zero_s)

        def fill_copy(row):
            return pltpu.make_async_copy(zero_s.at[pl.ds(0, 1), :], x_hbm.at[pl.ds(row, 1), :], row_sem.at[2])

        for e in range(N_EXPERTS):
            def fill(j, carry, e=e):
                fill_copy(fs_ref[e] + j).start()
                return carry

            lax.fori_loop(0, fc_ref[e], fill, 0)

            def drain(j, carry):
                fill_copy(0).wait()
                return carry

            lax.fori_loop(0, fc_ref[e], drain, 0)

    def wait(r, carry):
        row_copy(r, 0, 0).wait()
        row_copy(r, 1, 0).wait()
        return carry

    lax.fori_loop(0, tm, wait, 0, unroll=8)


def _moe_dispatch(h, slots, fill_start, fill_count, n_slots):
    t, dm = h.shape
    tm = slots.shape[2]
    grid_spec = pltpu.PrefetchScalarGridSpec(
        num_scalar_prefetch=2,
        grid=(t // tm,),
        in_specs=[pl.BlockSpec(memory_space=pl.ANY),
                  pl.BlockSpec((tm, dm), lambda i, fs, fc: (i, 0))],
        out_specs=pl.BlockSpec(memory_space=pl.ANY),
        scratch_shapes=[pltpu.SMEM((2, tm), I32), pltpu.VMEM((8, dm), F32),
                        pltpu.SemaphoreType.DMA((2,)), pltpu.SemaphoreType.DMA((3,))],
    )
    return pl.pallas_call(
        _moe_dispatch_kernel,
        grid_spec=grid_spec,
        out_shape=jax.ShapeDtypeStruct((n_slots, dm), F32),
        compiler_params=_params(("arbitrary",)),
        name="moe_dispatch",
    )(fill_start, fill_count, slots, h)


def _moe_expert_kernel(te_ref, nu_ref, x_ref, wg_ref, wu_ref, wd_ref, y_ref, xb_s):
    i, f = pl.program_id(0), pl.program_id(1)

    @pl.when(i < nu_ref[0])
    def _():
        @pl.when(f == 0)
        def _():
            xb_s[...] = x_ref[...].astype(BF16)

        xb = xb_s[...]
        g = _dot(xb, wg_ref[...])
        u = _dot(xb, wu_ref[...])
        y = _dot((g * jax.nn.sigmoid(g) * u).astype(BF16), wd_ref[...])

        @pl.when(f == 0)
        def _():
            y_ref[...] = y

        @pl.when(f > 0)
        def _():
            y_ref[...] = y_ref[...] + y


def _moe_experts(x_sorted, tile_expert, n_used, w_gate, w_up, w_down):
    n_slots, dm = x_sorted.shape
    tm = MOE_TILE
    ff = w_gate.shape[2]
    tf = MOE_FF_TILE

    def tile_block(i, f, te, nu):
        return jnp.minimum(i, nu[0] - 1), 0

    def col_block(i, f, te, nu):
        return te[i], 0, jnp.where(i < nu[0], f, ff // tf - 1)

    def row_block(i, f, te, nu):
        return te[i], jnp.where(i < nu[0], f, ff // tf - 1), 0

    grid_spec = pltpu.PrefetchScalarGridSpec(
        num_scalar_prefetch=2,
        grid=(n_slots // tm, ff // tf),
        in_specs=[pl.BlockSpec((tm, dm), tile_block),
                  pl.BlockSpec((None, dm, tf), col_block),
                  pl.BlockSpec((None, dm, tf), col_block),
                  pl.BlockSpec((None, tf, dm), row_block)],
        out_specs=pl.BlockSpec((tm, dm), tile_block),
        scratch_shapes=[pltpu.VMEM((tm, dm), BF16)],
    )
    return pl.pallas_call(
        _moe_expert_kernel,
        grid_spec=grid_spec,
        out_shape=jax.ShapeDtypeStruct((n_slots, dm), F32),
        compiler_params=_params(("arbitrary", "arbitrary")),
        name="moe_experts",
    )(tile_expert, n_used, x_sorted, w_gate, w_up, w_down)


def _moe_combine_kernel(slot_hbm, y_hbm, h_ref, rec_ref, ln_ref, o_ref, slot_s, y_s, idx_sem, row_sem):
    i = pl.program_id(0)
    tm = h_ref.shape[0]
    _load_slots(slot_hbm, slot_s, idx_sem, i)

    def row_copy(r, c, slot):
        return pltpu.make_async_copy(y_hbm.at[pl.ds(slot, 1), :], y_s.at[c, pl.ds(r, 1), :], row_sem.at[c])

    def start(r, carry):
        row_copy(r, 0, slot_s[0, r]).start()
        row_copy(r, 1, slot_s[1, r]).start()
        return carry

    lax.fori_loop(0, tm, start, 0, unroll=8)

    def wait(r, carry):
        row_copy(r, 0, 0).wait()
        row_copy(r, 1, 0).wait()
        return carry

    lax.fori_loop(0, tm, wait, 0, unroll=8)

    pad_rows = jnp.zeros((LANES - 8, LANES), F32)
    for c in range(tm // LANES):
        rows = slice(c * LANES, (c + 1) * LANES)
        gates = jnp.concatenate([rec_ref[:, rows], pad_rows], axis=0).T
        y = gates[:, 0:1] * y_s[0, rows, :] + gates[:, 1:2] * y_s[1, rows, :]
        o_ref[rows, :] = _layernorm(ALPHA * h_ref[rows, :] + y, ln_ref[0:1, :], ln_ref[1:2, :])


def _moe_combine(h, y_sorted, slots, rec, ln):
    t, dm = h.shape
    tm = slots.shape[2]
    return pl.pallas_call(
        _moe_combine_kernel,
        grid=(t // tm,),
        in_specs=[pl.BlockSpec(memory_space=pl.ANY),
                  pl.BlockSpec(memory_space=pl.ANY),
                  pl.BlockSpec((tm, dm), lambda i: (i, 0)),
                  pl.BlockSpec((8, tm), lambda i: (0, i)),
                  _resident(ln.shape)],
        out_specs=pl.BlockSpec((tm, dm), lambda i: (i, 0)),
        out_shape=jax.ShapeDtypeStruct((t, dm), F32),
        scratch_shapes=[pltpu.SMEM((2, tm), I32), pltpu.VMEM((2, tm, dm), F32),
                        pltpu.SemaphoreType.DMA((2,)), pltpu.SemaphoreType.DMA((2,))],
        compiler_params=_params(("arbitrary",)),
        name="moe_combine",
    )(slots, y_sorted, h, rec, ln)


def _moe(h, rec, counts, w_gate, w_up, w_down, ln):
    t, dm = h.shape
    tm = MOE_TILE
    n_tiles = (2 * t) // tm + N_EXPERTS
    cnt = counts[:, 0].astype(I32)
    padded = (cnt + tm - 1) // tm * tm
    pad_end = jnp.cumsum(padded)
    pad_start = pad_end - padded
    experts = jnp.arange(N_EXPERTS, dtype=I32)

    def slot_of(e_row, pos_row):
        e = e_row.astype(I32)
        start = jnp.sum(jnp.where(e[None, :] == experts[:, None], pad_start[:, None], 0), axis=0)
        return start + pos_row.astype(I32)

    slots = jnp.stack([slot_of(rec[2], rec[4]), slot_of(rec[3], rec[5])]).reshape(2, t // TOKEN_TILE, TOKEN_TILE)
    n_used = pad_end[-1] // tm
    tile_start = jnp.minimum(jnp.arange(n_tiles, dtype=I32), n_used - 1) * tm
    tile_expert = jnp.sum(tile_start[:, None] >= pad_end[None, :], axis=1).astype(I32)
    x_sorted = _moe_dispatch(h, slots, pad_start + cnt, padded - cnt, n_tiles * tm)
    y_sorted = _moe_experts(x_sorted, tile_expert, n_used.astype(I32).reshape(1), w_gate, w_up, w_down)
    return _moe_combine(h, y_sorted, slots, rec, ln)


def kernel(x, rel_bias, ln_gain, ln_bias, a_w_in, a_w_out, b_w_in, b_conv, b_w_out, c_w_in, c_w_out,
           d_w_in, d_ln_gain, d_ln_bias, d_w_s, d_b_s, d_w_out, f_w_gate, f_w_up, f_w_down,
           e_w_router, e_w_gate, e_w_up, e_w_down):
    bsz, seq, dm = x.shape
    t = bsz * seq
    assert seq % (8 * MOBA_BLOCK) == 0 and seq == DIL_GROUPS[-1][0] and ln_gain.shape[0] == DEPTH
    assert seq % TOKEN_TILE == 0 and t % MOE_TILE == 0
    q_scale = HEAD_DIM ** -0.5

    def ln_rows(i, *pairs):
        return jnp.concatenate([jnp.stack([ln_gain[i, p], ln_bias[i, p]]) for p in pairs], axis=0)

    def router_weights(w):
        w1, w2, _ = _split3(w.T.astype(F32))
        return jnp.stack([w1, w2])

    tri = jnp.triu(jnp.ones((TOKEN_TILE, TOKEN_TILE), BF16), k=1)

    def scaled_qkv(w):
        scale = jnp.concatenate([jnp.full((ATTN_WIDTH,), q_scale, F32), jnp.ones((2 * ATTN_WIDTH,), F32)])
        return (w * scale).astype(BF16)

    h = x
    group_cols = 3 * ATTN_WIDTH
    qkv = [_project_dilated(h, scaled_qkv(a_w_in[0][:, g * group_cols:(g + 1) * group_cols]), dil)
           for g, (_, dil) in enumerate(DIL_GROUPS)]
    tables = [_dilated_bias_table(rel_bias, dil) for _, dil in DIL_GROUPS]
    tables[2] = tables[2][:, :, DIL_BACK:]
    attn = _dilated_attention(qkv, tables)
    h = _attn_out_ffn(h.reshape(t, dm), attn.reshape(t, ATTN_WIDTH), a_w_out[0].astype(BF16),
                      f_w_gate[0].astype(BF16), f_w_up[0].astype(BF16), f_w_down[0].astype(BF16),
                      ln_rows(0, 0, 1))
    h, rec, counts = _conv_mixer(h.reshape(bsz, seq, dm), b_w_in[0].astype(BF16), b_conv[0],
                                 b_w_out[0].astype(BF16), ln_rows(1, 0), router_weights(e_w_router[0]), tri)
    h = _moe(h.reshape(t, dm), rec, counts, e_w_gate[0].astype(BF16), e_w_up[0].astype(BF16),
             e_w_down[0].astype(BF16), ln_rows(1, 1))
    qkv_c, kmean = _project_moba(h.reshape(bsz, seq, dm), scaled_qkv(c_w_in[0]))
    attn = _moba_attention(qkv_c, kmean, _moba_bias_table(rel_bias, seq // MOBA_BLOCK))
    h = _attn_out_ffn(h, attn.reshape(t, ATTN_WIDTH), c_w_out[0].astype(BF16),
                      f_w_gate[1].astype(BF16), f_w_up[1].astype(BF16), f_w_down[1].astype(BF16),
                      ln_rows(2, 0, 1))
    h, rec, counts = _gmlp_mixer(h, d_w_in[0].astype(BF16), jnp.stack([d_ln_gain[0], d_ln_bias[0]]),
                                 d_w_s[0], d_b_s[0].T, d_w_out[0].astype(BF16), ln_rows(3, 0),
                                 router_weights(e_w_router[1]), tri)
    h = _moe(h, rec, counts, e_w_gate[1].astype(BF16), e_w_up[1].astype(BF16),
             e_w_down[1].astype(BF16), ln_rows(3, 1))
    return h.reshape(bsz, seq, dm)
```

```python
import functools
import math

import jax
import jax.numpy as jnp
from jax import lax
from jax.experimental import pallas as pl
from jax.experimental.pallas import tpu as pltpu

F32 = jnp.float32
BF16 = jnp.bfloat16
I32 = jnp.int32

N_HEADS = 16
HEAD_DIM = 64
LANES = 128
HEADS_PER_TILE = LANES // HEAD_DIM
N_HEAD_TILES = N_HEADS // HEADS_PER_TILE
ATTN_WIDTH = N_HEADS * HEAD_DIM
NUM_BUCKETS = 32
MAX_DISTANCE = 2048
DIL_GROUPS = ((128, 1), (512, 4), (2048, 16))
DIL_BACK = 128
DIL_UNROLL = 4
MOBA_BLOCK = 256
MOBA_TOPK = 3
GMLP_CHUNK = 128
GMLP_GROUPS = 8
N_EXPERTS = 8
DEPTH = 4
ALPHA = (2 * DEPTH) ** 0.25
LN_EPS = 1e-5
NEG_INF = -1e30

TOKEN_TILE = 512
MOE_TILE = 1024
MOE_FF_TILE = 512
MOE_ROW_TILE = 1024
VMEM_LIMIT = 56 * 1024 * 1024


def _params(semantics, **kw):
    return pltpu.CompilerParams(dimension_semantics=semantics, vmem_limit_bytes=VMEM_LIMIT, **kw)


def _resident(shape):
    return pl.BlockSpec(shape, lambda *_: (0,) * len(shape), pipeline_mode=pl.Buffered(1))


def _layernorm(z, gain, bias):
    mu = jnp.mean(z, axis=-1, keepdims=True)
    zc = z - mu
    var = jnp.mean(zc * zc, axis=-1, keepdims=True)
    return zc * lax.rsqrt(var + LN_EPS) * gain + bias


def _gelu(x):
    return 0.5 * x * (1.0 + lax.erf(x * math.sqrt(0.5)))


def _dot(a, b):
    return jnp.dot(a, b, preferred_element_type=F32)


def _dot_nt(a, b):
    return lax.dot_general(a, b, (((1,), (1,)), ((), ())), preferred_element_type=F32)


def _split3(x):
    x1 = x.astype(BF16)
    r1 = x - x1.astype(F32)
    x2 = r1.astype(BF16)
    x3 = (r1 - x2.astype(F32)).astype(BF16)
    return x1, x2, x3


def _t5_bucket(dist):
    max_exact = NUM_BUCKETS // 2
    d = jnp.maximum(dist, 0)
    log_ratio = jnp.log(jnp.maximum(d, 1).astype(F32) / max_exact) / math.log(MAX_DISTANCE / max_exact)
    large = max_exact + (log_ratio * (NUM_BUCKETS - max_exact)).astype(I32)
    large = jnp.clip(large, max_exact, NUM_BUCKETS - 1)
    return jnp.where(d < max_exact, d, large)


def _bias_lookup(rel_bias, dist):
    onehot = (_t5_bucket(dist)[..., None] == jnp.arange(NUM_BUCKETS, dtype=I32)).astype(F32)
    out = jnp.einsum('...k,kh->...h', onehot, rel_bias.astype(F32), precision=lax.Precision.HIGHEST)
    return jnp.moveaxis(out, -1, 0)


def _dilated_bias_table(rel_bias, dil):
    qi = jnp.arange(DIL_BACK, dtype=I32)[:, None]
    kj = jnp.arange(2 * DIL_BACK, dtype=I32)[None, :]
    delta = qi + DIL_BACK - kj
    band = (delta >= 0) & (delta <= DIL_BACK)
    bias = _bias_lookup(rel_bias, delta * dil)
    return jnp.where(band[None], bias, NEG_INF)


def _moba_bias_table(rel_bias, n_blk):
    pos = jnp.arange(MOBA_BLOCK, dtype=I32)
    diff = pos[:, None] - pos[None, :]
    blocks_back = jnp.arange(n_blk - 1, -1, -1, dtype=I32)
    dist = blocks_back[None, :, None] * MOBA_BLOCK + diff[:, None, :]
    bias = jnp.where((dist >= 0)[None], _bias_lookup(rel_bias, dist), NEG_INF)
    return bias.reshape(N_HEADS, MOBA_BLOCK, n_blk * MOBA_BLOCK)


def _proj_kernel(*refs, dil, n_split):
    x_refs, w_ref, o_ref = refs[:-2], refs[-2], refs[-1]
    tm = x_refs[0].shape[0]
    rows = tm // dil
    if dil == 1:
        xb = jnp.concatenate([x_ref[...] for x_ref in x_refs], axis=1).astype(BF16)
    else:
        xb = jnp.concatenate(
            [jnp.concatenate([x_ref[pl.ds(r, rows, stride=dil), :] for x_ref in x_refs], axis=1)
             for r in range(dil)], axis=0).astype(BF16)
    width = w_ref.shape[1] // n_split
    for c in range(n_split):
        y = _dot(xb, w_ref[:, c * width:(c + 1) * width]).astype(BF16)
        for r in range(dil):
            o_ref[r, :, c * width:(c + 1) * width] = y[r * rows:(r + 1) * rows]


def _project_dilated(x, w, dil):
    bsz, seq, dm = x.shape
    n = w.shape[1]
    tm = TOKEN_TILE
    return pl.pallas_call(
        functools.partial(_proj_kernel, dil=dil, n_split=3),
        grid=(bsz, seq // tm),
        in_specs=[pl.BlockSpec((None, tm, LANES), lambda b, j, c=c: (b, j, c)) for c in range(dm // LANES)]
        + [_resident((dm, n))],
        out_specs=pl.BlockSpec((None, dil, tm // dil, n), lambda b, j: (b, 0, j, 0)),
        out_shape=jax.ShapeDtypeStruct((bsz, dil, seq // dil, n), BF16),
        compiler_params=_params(("parallel", "parallel")),
        name=f"proj_dil{dil}",
    )(*([x] * (dm // LANES)), w)


def _proj_kmean_kernel(x_ref, w_ref, o_ref, km_ref):
    tm = x_ref.shape[0]
    per_step = tm // MOBA_BLOCK
    j = pl.program_id(1)
    xb = x_ref[...].astype(BF16)
    width = ATTN_WIDTH
    for c in range(3):
        y = _dot(xb, w_ref[:, c * width:(c + 1) * width])
        o_ref[:, c * width:(c + 1) * width] = y.astype(BF16)
        if c == 1:
            for jj in range(per_step):
                km_ref[pl.ds(j * per_step + jj, 1), :] = jnp.mean(
                    y[jj * MOBA_BLOCK:(jj + 1) * MOBA_BLOCK], axis=0, keepdims=True)


def _project_moba(x, w):
    bsz, seq, dm = x.shape
    n = w.shape[1]
    tm = TOKEN_TILE
    n_blk = seq // MOBA_BLOCK
    return pl.pallas_call(
        _proj_kmean_kernel,
        grid=(bsz, seq // tm),
        in_specs=[pl.BlockSpec((None, tm, dm), lambda b, j: (b, j, 0)),
                  _resident((dm, n))],
        out_specs=[pl.BlockSpec((None, tm, n), lambda b, j: (b, j, 0)),
                   pl.BlockSpec((None, n_blk, ATTN_WIDTH), lambda b, j: (b, 0, 0))],
        out_shape=[jax.ShapeDtypeStruct((bsz, seq, n), BF16),
                   jax.ShapeDtypeStruct((bsz, n_blk, ATTN_WIDTH), F32)],
        compiler_params=_params(("parallel", "arbitrary")),
        name="proj_moba",
    )(x, w)


def _two_head_partials(units, is_a):
    scores = []
    for q, kw, _, bias_a, bias_b in units:
        zero = jnp.zeros_like(q)
        scores.append((_dot_nt(jnp.where(is_a, q, zero), kw) + bias_a,
                       _dot_nt(jnp.where(is_a, zero, q), kw) + bias_b))
    stats = []
    for pair in scores:
        row = []
        for s in pair:
            m = jnp.max(s, axis=-1, keepdims=True)
            p = jnp.exp(s - m)
            row.append((m, jnp.sum(p, axis=-1, keepdims=True), p.astype(BF16)))
        stats.append(row)
    out = []
    for (_, _, vw, _, _), ((m_a, l_a, p_a), (m_b, l_b, p_b)) in zip(units, stats):
        out.append((jnp.where(is_a, m_a, m_b), jnp.where(is_a, l_a, l_b),
                    jnp.where(is_a, _dot(p_a, vw), _dot(p_b, vw))))
    return out


def _dilated_attn_kernel(q0, k0, v0, q1, k1, v1, q2, k2, v2, b0, b1, b2, o_ref, m_s, l_s, a_s, tmp_s):
    nb = DIL_BACK
    is_a = lax.broadcasted_iota(I32, (nb, LANES), 1) < HEAD_DIM
    n_res2, n_res1 = q2.shape[0], q1.shape[0]
    n_blk1 = q1.shape[1] // nb
    n_blk0 = q0.shape[0] // nb

    def merge(rows, m, l, o):
        m_old, l_old, a_old = m_s[rows, :], l_s[rows, :], a_s[rows, :]
        m_new = jnp.maximum(m_old, m)
        w_old, w_new = jnp.exp(m_old - m_new), jnp.exp(m - m_new)
        return m_new, w_old * l_old + w_new * l, w_old * a_old + w_new * o

    unroll = DIL_UNROLL

    pitch = tmp_s.shape[1] // n_res2

    def group2(i, carry):
        res = [i * unroll + u for u in range(unroll)]
        parts = _two_head_partials([(q2[r], k2[r], v2[r], b2[0], b2[1]) for r in res], is_a)
        for r, part in zip(res, parts):
            rows = pl.ds(pl.multiple_of(r * pitch, 8), nb)
            for c in range(3):
                tmp_s[c, rows, :] = part[c]
        return carry

    lax.fori_loop(0, n_res2 // unroll, group2, 0)

    def to_token_major(s, carry):
        dst = pl.ds(pl.multiple_of(s * n_res2, n_res2), n_res2)
        for c, ref in enumerate((m_s, l_s, a_s)):
            ref[dst, :] = tmp_s[c, pl.ds(s, n_res2, stride=pitch), :]
        return carry

    lax.fori_loop(0, nb, to_token_major, 0, unroll=8)

    def group1(r, carry):
        units = [(q1[r, 0:nb, :], k1[r, 0:nb, :], v1[r, 0:nb, :], b1[0, :, nb:], b1[1, :, nb:])]
        for n in range(1, n_blk1):
            keys = slice((n - 1) * nb, (n + 1) * nb)
            units.append((q1[r, n * nb:(n + 1) * nb, :], k1[r, keys, :], v1[r, keys, :], b1[0], b1[1]))
        for n, (m, l, o) in enumerate(_two_head_partials(units, is_a)):
            rows = pl.ds(n * nb * n_res1 + r, nb, stride=n_res1)
            m, l, o = merge(rows, m, l, o)
            m_s[rows, :] = m
            l_s[rows, :] = l
            a_s[rows, :] = o
        return carry

    lax.fori_loop(0, n_res1, group1, 0)

    def finish(rows, m, l, o):
        _, l, o = merge(rows, m, l, o)
        o_ref[rows, :] = (o / l).astype(o_ref.dtype)

    def unit0(n):
        aligned = (lambda v: v) if isinstance(n, int) else (lambda v: pl.multiple_of(v, nb))
        q_rows = pl.ds(aligned(n * nb), nb)
        k_rows = pl.ds(aligned((n - 1) * nb), 2 * nb)
        return q_rows, (q0[q_rows, :], k0[k_rows, :], v0[k_rows, :], b0[0], b0[1])

    def blocks0(units):
        for (q_rows, _), (m, l, o) in zip(units, _two_head_partials([u for _, u in units], is_a)):
            finish(q_rows, m, l, o)

    first = (pl.ds(0, nb), (q0[0:nb, :], k0[0:nb, :], v0[0:nb, :], b0[0, :, nb:], b0[1, :, nb:]))
    blocks0([first] + [unit0(n) for n in range(1, unroll)])

    def group0(i, carry):
        blocks0([unit0(i * unroll + u) for u in range(unroll)])
        return carry

    lax.fori_loop(1, n_blk0 // unroll, group0, 0)


def _dilated_attention(qkv, tables):
    bsz = qkv[0].shape[0]
    seq = qkv[0].shape[1] * qkv[0].shape[2]
    nt = N_HEAD_TILES
    in_specs, args = [], []
    for g, (_, dil) in enumerate(DIL_GROUPS):
        sub = seq // dil
        for part in range(3):
            if dil == 1:
                spec = pl.BlockSpec((None, None, sub, LANES), lambda t, b, part=part: (b, 0, 0, part * nt + t))
            else:
                spec = pl.BlockSpec((None, dil, sub, LANES), lambda t, b, part=part: (b, 0, 0, part * nt + t))
            in_specs.append(spec)
            args.append(qkv[g])
    for g in range(3):
        width = tables[g].shape[2]
        in_specs.append(pl.BlockSpec((HEADS_PER_TILE, DIL_BACK, width), lambda t, b: (t, 0, 0)))
        args.append(tables[g])
    return pl.pallas_call(
        _dilated_attn_kernel,
        grid=(nt, bsz),
        in_specs=in_specs,
        out_specs=pl.BlockSpec((None, seq, LANES), lambda t, b: (b, 0, t)),
        out_shape=jax.ShapeDtypeStruct((bsz, seq, ATTN_WIDTH), BF16),
        scratch_shapes=[pltpu.VMEM((seq, LANES), F32)] * 3
        + [pltpu.VMEM((3, DIL_GROUPS[2][1] * (DIL_BACK + 8), LANES), F32)],
        compiler_params=_params(("parallel", "parallel")),
        name="dilated_attn",
    )(*args)


def _moba_kernel(q_ref, k_ref, v_ref, km_ref, bias_ref, o_ref, kaug_s, nsel_s):
    seq = q_ref.shape[0]
    blk = MOBA_BLOCK
    n_blk = seq // blk
    is_a_row = lax.broadcasted_iota(I32, (n_blk, LANES), 1) < HEAD_DIM
    is_a = lax.broadcasted_iota(I32, (blk, LANES), 1) < HEAD_DIM

    kaug_s[:, :LANES] = k_ref[...]
    key_blk = lax.broadcasted_iota(I32, (seq, LANES), 0) // blk
    col = lax.broadcasted_iota(I32, (seq, LANES), 1)
    kaug_s[:, LANES:] = jnp.where(col == key_blk, NEG_INF, 0.0).astype(BF16)

    q_all = q_ref[...]
    km = km_ref[...]
    q_blk = lax.broadcasted_iota(I32, (n_blk, seq), 1) // blk
    m_idx = lax.broadcasted_iota(I32, (n_blk, seq), 0)
    past = m_idx < q_blk
    n_sel = min(MOBA_TOPK, n_blk - 1)
    pad_rows = jnp.zeros((LANES - n_blk, LANES), F32)
    for hd in range(HEADS_PER_TILE):
        kmh = jnp.where(is_a_row if hd == 0 else jnp.logical_not(is_a_row), km, 0.0)
        gate = sum(_dot_nt(piece, q_all) for piece in _split3(kmh))
        gate = jnp.where(past, gate, NEG_INF)
        rank = jnp.zeros((n_blk, seq), F32)
        for mp in range(n_blk - 1):
            row = gate[mp:mp + 1, :]
            beats = jnp.where(row > gate, 1.0, jnp.where(row == gate, jnp.where(mp < m_idx, 1.0, 0.0), 0.0))
            rank = rank + jnp.where(mp < q_blk, beats, 0.0)
        unselected = jnp.where(past, jnp.where(rank < n_sel, 0.0, 1.0), 1.0)
        unselected = jnp.where(m_idx == q_blk, 0.0, unselected)
        for c in range(seq // LANES):
            tile = jnp.concatenate([unselected[:, c * LANES:(c + 1) * LANES], pad_rows], axis=0)
            nsel_s[hd, c * LANES:(c + 1) * LANES, :] = tile.T.astype(BF16)

    for n in range(n_blk):
        rows = slice(n * blk, (n + 1) * blk)
        keys = slice(0, (n + 1) * blk)
        qn = q_ref[rows, :]
        zero = jnp.zeros_like(qn)
        scores = []
        for hd in range(HEADS_PER_TILE):
            qh = jnp.where(is_a, qn, zero) if hd == 0 else jnp.where(is_a, zero, qn)
            qa = jnp.concatenate([qh, nsel_s[hd, rows, :]], axis=1)
            scores.append(_dot_nt(qa, kaug_s[keys, :]) + bias_ref[hd, :, (n_blk - 1 - n) * blk:])
        probs = []
        for s in scores:
            p = jnp.exp(s - jnp.max(s, axis=-1, keepdims=True))
            probs.append((p.astype(BF16), jnp.sum(p, axis=-1, keepdims=True)))
        outs = [_dot(p, v_ref[keys, :]) / l for p, l in probs]
        o_ref[rows, :] = jnp.where(is_a, outs[0], outs[1]).astype(o_ref.dtype)


def _moba_attention(qkv, kmean, table):
    bsz, seq, _ = qkv.shape
    nt = N_HEAD_TILES
    n_blk = seq // MOBA_BLOCK
    return pl.pallas_call(
        _moba_kernel,
        grid=(nt, bsz),
        in_specs=[pl.BlockSpec((None, seq, LANES), lambda t, b: (b, 0, t)),
                  pl.BlockSpec((None, seq, LANES), lambda t, b: (b, 0, nt + t)),
                  pl.BlockSpec((None, seq, LANES), lambda t, b: (b, 0, 2 * nt + t)),
                  pl.BlockSpec((None, n_blk, LANES), lambda t, b: (b, 0, t)),
                  pl.BlockSpec((HEADS_PER_TILE, MOBA_BLOCK, seq), lambda t, b: (t, 0, 0))],
        out_specs=pl.BlockSpec((None, seq, LANES), lambda t, b: (b, 0, t)),
        out_shape=jax.ShapeDtypeStruct((bsz, seq, ATTN_WIDTH), BF16),
        scratch_shapes=[pltpu.VMEM((seq, 2 * LANES), BF16),
                        pltpu.VMEM((HEADS_PER_TILE, seq, LANES), BF16)],
        compiler_params=_params(("parallel", "parallel")),
        name="moba_attn",
    )(qkv, qkv, qkv, kmean, table)


def _attn_out_ffn_kernel(x_ref, a_ref, wo_ref, wg_ref, wu_ref, wd_ref, ln_ref, o_ref, *, n_split):
    x = x_ref[...]
    h = _layernorm(ALPHA * x + _dot(a_ref[...], wo_ref[...]), ln_ref[0:1, :], ln_ref[1:2, :])
    hb = h.astype(BF16)
    width = wg_ref.shape[1] // n_split
    y = None
    for c in range(n_split):
        cols = slice(c * width, (c + 1) * width)
        g = _dot(hb, wg_ref[:, cols])
        u = _dot(hb, wu_ref[:, cols])
        part = _dot((g * jax.nn.sigmoid(g) * u).astype(BF16), wd_ref[cols, :])
        y = part if y is None else y + part
    o_ref[...] = _layernorm(ALPHA * h + y, ln_ref[2:3, :], ln_ref[3:4, :])


def _attn_out_ffn(x, attn, w_out, w_gate, w_up, w_down, ln):
    t, dm = x.shape
    ff = w_gate.shape[1]
    tm = TOKEN_TILE
    n_split = 2 if (ff // 2) % LANES == 0 else 1
    return pl.pallas_call(
        functools.partial(_attn_out_ffn_kernel, n_split=n_split),
        grid=(t // tm,),
        in_specs=[pl.BlockSpec((tm, dm), lambda i: (i, 0)),
                  pl.BlockSpec((tm, attn.shape[1]), lambda i: (i, 0)),
                  _resident(w_out.shape), _resident(w_gate.shape), _resident(w_up.shape),
                  _resident(w_down.shape), _resident(ln.shape)],
        out_specs=pl.BlockSpec((tm, dm), lambda i: (i, 0)),
        out_shape=jax.ShapeDtypeStruct((t, dm), F32),
        compiler_params=_params(("parallel",)),
        name="attn_out_ffn",
    )(x, attn, w_out, w_gate, w_up, w_down, ln)


def _route(h, wr_ref, tri_ref, cnt_s, rec_ref, cnt_ref):
    tm = h.shape[0]
    h1, h2, _ = _split3(h)
    logits = _dot_nt(wr_ref[0], h1) + _dot_nt(wr_ref[1], h1) + _dot_nt(wr_ref[0], h2)
    e_idx = lax.broadcasted_iota(I32, (N_EXPERTS, tm), 0).astype(F32)
    top1 = jnp.max(logits, axis=0, keepdims=True)
    idx1 = jnp.min(jnp.where(logits == top1, e_idx, float(N_EXPERTS)), axis=0, keepdims=True)
    rest = jnp.where(e_idx == idx1, -jnp.inf, logits)
    top2 = jnp.max(rest, axis=0, keepdims=True)
    idx2 = jnp.min(jnp.where(rest == top2, e_idx, float(N_EXPERTS)), axis=0, keepdims=True)
    z = jnp.exp(top2 - top1)
    gate1 = 1.0 / (1.0 + z)
    gate2 = z / (1.0 + z)
    hot1 = jnp.where(e_idx == idx1, 1.0, 0.0)
    hot2 = jnp.where(e_idx == idx2, 1.0, 0.0)
    hot = hot1 + hot2
    before = _dot(hot.astype(BF16), tri_ref[...]) + cnt_s[:, 0:1]
    pos1 = jnp.sum(hot1 * before, axis=0, keepdims=True)
    pos2 = jnp.sum(hot2 * before, axis=0, keepdims=True)
    cnt_s[...] = cnt_s[...] + jnp.sum(hot, axis=1, keepdims=True)
    cnt_ref[...] = cnt_s[...]
    rec_ref[...] = jnp.concatenate([gate1, gate2, idx1, idx2, pos1, pos2, jnp.zeros((2, tm), F32)], axis=0)


def _conv_mixer_kernel(x_ref, wi_ref, cw_ref, wo_ref, ln_ref, wr_ref, tri_ref,
                       o_ref, rec_ref, cnt_ref, y_s, cnt_s):
    tm, dm = x_ref.shape
    first = (pl.program_id(0) == 0) & (pl.program_id(1) == 0)

    @pl.when(first)
    def _():
        cnt_s[...] = jnp.zeros_like(cnt_s)

    @pl.when(pl.program_id(1) == 0)
    def _():
        y_s[0:8, :] = jnp.zeros((8, dm), F32)

    x = x_ref[...]
    xb = x.astype(BF16)
    gate_c = _dot(xb, wi_ref[:, dm:2 * dm])
    x_in = _dot(xb, wi_ref[:, 2 * dm:])
    y_s[8:, :] = gate_c * x_in
    z = cw_ref[0:1, :] * y_s[6:6 + tm, :] + cw_ref[1:2, :] * y_s[7:7 + tm, :] + cw_ref[2:3, :] * y_s[8:, :]
    y_s[0:8, :] = y_s[tm:, :]
    gate_b = _dot(xb, wi_ref[:, :dm])
    mixed = _dot((gate_b * z).astype(BF16), wo_ref[...])
    h = _layernorm(ALPHA * x + mixed, ln_ref[0:1, :], ln_ref[1:2, :])
    o_ref[...] = h
    _route(h, wr_ref, tri_ref, cnt_s, rec_ref, cnt_ref)


def _conv_mixer(x, w_in, conv_w, w_out, ln, wr, tri):
    bsz, seq, dm = x.shape
    tm = TOKEN_TILE
    nj = seq // tm
    return pl.pallas_call(
        _conv_mixer_kernel,
        grid=(bsz, nj),
        in_specs=[pl.BlockSpec((None, tm, dm), lambda b, j: (b, j, 0)),
                  _resident(w_in.shape), _resident(conv_w.shape), _resident(w_out.shape),
                  _resident(ln.shape), _resident(wr.shape), _resident(tri.shape)],
        out_specs=[pl.BlockSpec((None, tm, dm), lambda b, j: (b, j, 0)),
                   pl.BlockSpec((8, tm), lambda b, j: (0, b * nj + j)),
                   pl.BlockSpec((N_EXPERTS, LANES), lambda b, j: (0, 0))],
        out_shape=[jax.ShapeDtypeStruct((bsz, seq, dm), F32),
                   jax.ShapeDtypeStruct((8, bsz * seq), F32),
                   jax.ShapeDtypeStruct((N_EXPERTS, LANES), F32)],
        scratch_shapes=[pltpu.VMEM((tm + 8, dm), F32), pltpu.VMEM((N_EXPERTS, LANES), F32)],
        compiler_params=_params(("arbitrary", "arbitrary")),
        name="conv_mixer",
    )(x, w_in, conv_w, w_out, ln, wr, tri)


def _gmlp_mixer_kernel(x_ref, wi_ref, dln_ref, ws_ref, bs_ref, wo_ref, ln_ref, wr_ref, tri_ref,
                       o_ref, rec_ref, cnt_ref, mix_s, cnt_s):
    tm, dm = x_ref.shape
    ch = GMLP_CHUNK
    gw = dm // GMLP_GROUPS

    @pl.when(pl.program_id(0) == 0)
    def _():
        cnt_s[...] = jnp.zeros_like(cnt_s)

    x = x_ref[...]
    xb = x.astype(BF16)
    u = _gelu(_dot(xb, wi_ref[:, :dm]))
    v = _gelu(_dot(xb, wi_ref[:, dm:]))
    vb = _layernorm(v, dln_ref[0:1, :], dln_ref[1:2, :]).astype(BF16)
    lower = lax.broadcasted_iota(I32, (ch, ch), 0) >= lax.broadcasted_iota(I32, (ch, ch), 1)
    for g in range(GMLP_GROUPS):
        ws = jnp.where(lower, ws_ref[g], 0.0).astype(BF16)
        for c in range(tm // ch):
            blk = _dot(ws, vb[c * ch:(c + 1) * ch, g * gw:(g + 1) * gw]) + bs_ref[:, g:g + 1]
            mix_s[c * ch:(c + 1) * ch, g * gw:(g + 1) * gw] = blk
    mixed = _dot((u * mix_s[...]).astype(BF16), wo_ref[...])
    h = _layernorm(ALPHA * x + mixed, ln_ref[0:1, :], ln_ref[1:2, :])
    o_ref[...] = h
    _route(h, wr_ref, tri_ref, cnt_s, rec_ref, cnt_ref)


def _gmlp_mixer(x, w_in, dln, w_s, b_s_t, w_out, ln, wr, tri):
    t, dm = x.shape
    tm = TOKEN_TILE
    return pl.pallas_call(
        _gmlp_mixer_kernel,
        grid=(t // tm,),
        in_specs=[pl.BlockSpec((tm, dm), lambda i: (i, 0)),
                  _resident(w_in.shape), _resident(dln.shape), _resident(w_s.shape),
                  _resident(b_s_t.shape), _resident(w_out.shape), _resident(ln.shape),
                  _resident(wr.shape), _resident(tri.shape)],
        out_specs=[pl.BlockSpec((tm, dm), lambda i: (i, 0)),
                   pl.BlockSpec((8, tm), lambda i: (0, i)),
                   pl.BlockSpec((N_EXPERTS, LANES), lambda i: (0, 0))],
        out_shape=[jax.ShapeDtypeStruct((t, dm), F32),
                   jax.ShapeDtypeStruct((8, t), F32),
                   jax.ShapeDtypeStruct((N_EXPERTS, LANES), F32)],
        scratch_shapes=[pltpu.VMEM((tm, dm), F32), pltpu.VMEM((N_EXPERTS, LANES), F32)],
        compiler_params=_params(("arbitrary",)),
        name="gmlp_mixer",
    )(x, w_in, dln, w_s, b_s_t, w_out, ln, wr, tri)


def _load_slots(slot_hbm, slot_s, sem, i):
    copies = [pltpu.make_async_copy(slot_hbm.at[c, i], slot_s[c], sem.at[c]) for c in range(2)]
    for cp in copies:
        cp.start()
    for cp in copies:
        cp.wait()


def _start_row_copies(n_rows, row_copy, slot_s):
    def body(j, carry):
        base = pl.multiple_of(j * 8, 8)
        for u in range(8):
            for c in range(2):
                row_copy(base + u, c, slot_s[c][base + u]).start(priority=(u + c) % 2)
        return carry

    lax.fori_loop(0, n_rows // 8, body, 0)


def _moe_dispatch_kernel(fs_ref, fc_ref, slot_hbm, h_ref, x_hbm, slot0_s, slot1_s, zero_s, idx_sem, row_sem):
    i = pl.program_id(0)
    tm = h_ref.shape[0]
    slot_s = (slot0_s, slot1_s)
    _load_slots(slot_hbm, slot_s, idx_sem, i)

    def row_copy(r, c, slot):
        return pltpu.make_async_copy(h_ref.at[pl.ds(r, 1), :], x_hbm.at[pl.ds(slot, 1), :], row_sem.at[c])

    _start_row_copies(tm, row_copy, slot_s)

    @pl.when(i == 0)
    def _():
        zero_s[...] = jnp.zeros_like(zero_s)

        def fill_copy(row):
            return pltpu.make_async_copy(zero_s.at[pl.ds(0, 1), :], x_hbm.at[pl.ds(row, 1), :], row_sem.at[2])

        for e in range(N_EXPERTS):
            def fill(j, carry, e=e):
                fill_copy(fs_ref[e] + j).start()
                return carry

            lax.fori_loop(0, fc_ref[e], fill, 0)

            def drain(j, carry):
                fill_copy(0).wait()
                return carry

            lax.fori_loop(0, fc_ref[e], drain, 0)

    for c in range(2):
        pltpu.make_async_copy(h_ref, x_hbm.at[pl.ds(0, tm), :], row_sem.at[c]).wait()


def _moe_dispatch(h, slots, fill_start, fill_count, n_slots):
    t, dm = h.shape
    tm = slots.shape[2]
    grid_spec = pltpu.PrefetchScalarGridSpec(
        num_scalar_prefetch=2,
        grid=(t // tm,),
        in_specs=[pl.BlockSpec(memory_space=pl.ANY),
                  pl.BlockSpec((tm, dm), lambda i, fs, fc: (i, 0))],
        out_specs=pl.BlockSpec(memory_space=pl.ANY),
        scratch_shapes=[pltpu.SMEM((tm,), I32), pltpu.SMEM((tm,), I32), pltpu.VMEM((8, dm), F32),
                        pltpu.SemaphoreType.DMA((2,)), pltpu.SemaphoreType.DMA((3,))],
    )
    return pl.pallas_call(
        _moe_dispatch_kernel,
        grid_spec=grid_spec,
        out_shape=jax.ShapeDtypeStruct((n_slots, dm), F32),
        compiler_params=_params(("arbitrary",)),
        name="moe_dispatch",
    )(fill_start, fill_count, slots, h)


def _moe_expert_kernel(te_ref, nu_ref, x_ref, wg_ref, wu_ref, wd_ref, y_ref, xb_s):
    i, f = pl.program_id(0), pl.program_id(1)

    @pl.when(i < nu_ref[0])
    def _():
        @pl.when(f == 0)
        def _():
            xb_s[...] = x_ref[...].astype(BF16)
            y_ref[...] = jnp.zeros_like(y_ref)

        xb = xb_s[...]
        g = _dot(xb, wg_ref[...])
        u = _dot(xb, wu_ref[...])
        y_ref[...] += _dot((g * jax.nn.sigmoid(g) * u).astype(BF16), wd_ref[...])


def _moe_experts(x_sorted, tile_expert, n_used, w_gate, w_up, w_down):
    n_slots, dm = x_sorted.shape
    tm = MOE_TILE
    ff = w_gate.shape[2]
    tf = MOE_FF_TILE

    def tile_block(i, f, te, nu):
        return jnp.minimum(i, nu[0] - 1), 0

    def col_block(i, f, te, nu):
        return te[i], 0, jnp.where(i < nu[0], f, ff // tf - 1)

    def row_block(i, f, te, nu):
        return te[i], jnp.where(i < nu[0], f, ff // tf - 1), 0

    grid_spec = pltpu.PrefetchScalarGridSpec(
        num_scalar_prefetch=2,
        grid=(n_slots // tm, ff // tf),
        in_specs=[pl.BlockSpec((tm, dm), tile_block),
                  pl.BlockSpec((None, dm, tf), col_block),
                  pl.BlockSpec((None, dm, tf), col_block),
                  pl.BlockSpec((None, tf, dm), row_block)],
        out_specs=pl.BlockSpec((tm, dm), tile_block),
        scratch_shapes=[pltpu.VMEM((tm, dm), BF16)],
    )
    return pl.pallas_call(
        _moe_expert_kernel,
        grid_spec=grid_spec,
        out_shape=jax.ShapeDtypeStruct((n_slots, dm), F32),
        compiler_params=_params(("arbitrary", "arbitrary")),
        name="moe_experts",
    )(tile_expert, n_used, x_sorted, w_gate, w_up, w_down)


def _moe_combine_kernel(slot_hbm, y_hbm, h_ref, rec_ref, ln_ref, o_ref, slot0_s, slot1_s, y_s, idx_sem, row_sem):
    i = pl.program_id(0)
    tm = h_ref.shape[0]
    slot_s = (slot0_s, slot1_s)
    _load_slots(slot_hbm, slot_s, idx_sem, i)

    def row_copy(r, c, slot):
        return pltpu.make_async_copy(y_hbm.at[pl.ds(slot, 1), :], y_s.at[c, pl.ds(r, 1), :], row_sem.at[c])

    _start_row_copies(tm, row_copy, slot_s)
    for c in range(2):
        pltpu.make_async_copy(y_hbm.at[pl.ds(0, tm), :], y_s.at[c], row_sem.at[c]).wait()

    pad_rows = jnp.zeros((LANES - 8, LANES), F32)
    for c in range(tm // LANES):
        rows = slice(c * LANES, (c + 1) * LANES)
        gates = jnp.concatenate([rec_ref[:, rows], pad_rows], axis=0).T
        y = gates[:, 0:1] * y_s[0, rows, :] + gates[:, 1:2] * y_s[1, rows, :]
        o_ref[rows, :] = _layernorm(ALPHA * h_ref[rows, :] + y, ln_ref[0:1, :], ln_ref[1:2, :])


def _moe_combine(h, y_sorted, slots, rec, ln):
    t, dm = h.shape
    tm = slots.shape[2]
    return pl.pallas_call(
        _moe_combine_kernel,
        grid=(t // tm,),
        in_specs=[pl.BlockSpec(memory_space=pl.ANY),
                  pl.BlockSpec(memory_space=pl.ANY),
                  pl.BlockSpec((tm, dm), lambda i: (i, 0)),
                  pl.BlockSpec((8, tm), lambda i: (0, i)),
                  _resident(ln.shape)],
        out_specs=pl.BlockSpec((tm, dm), lambda i: (i, 0)),
        out_shape=jax.ShapeDtypeStruct((t, dm), F32),
        scratch_shapes=[pltpu.SMEM((tm,), I32), pltpu.SMEM((tm,), I32), pltpu.VMEM((2, tm, dm), F32),
                        pltpu.SemaphoreType.DMA((2,)), pltpu.SemaphoreType.DMA((2,))],
        compiler_params=_params(("arbitrary",)),
        name="moe_combine",
    )(slots, y_sorted, h, rec, ln)


def _moe(h, rec, counts, w_gate, w_up, w_down, ln):
    t, dm = h.shape
    tm = MOE_TILE
    n_tiles = (2 * t) // tm + N_EXPERTS
    cnt = counts[:, 0].astype(I32)
    padded = (cnt + tm - 1) // tm * tm
    pad_end = jnp.cumsum(padded)
    pad_start = pad_end - padded
    experts = jnp.arange(N_EXPERTS, dtype=I32)

    def slot_of(e_row, pos_row):
        e = e_row.astype(I32)
        start = jnp.sum(jnp.where(e[None, :] == experts[:, None], pad_start[:, None], 0), axis=0)
        return start + pos_row.astype(I32)

    slots = jnp.stack([slot_of(rec[2], rec[4]), slot_of(rec[3], rec[5])]).reshape(2, t // MOE_ROW_TILE, MOE_ROW_TILE)
    n_used = pad_end[-1] // tm
    tile_start = jnp.minimum(jnp.arange(n_tiles, dtype=I32), n_used - 1) * tm
    tile_expert = jnp.sum(tile_start[:, None] >= pad_end[None, :], axis=1).astype(I32)
    x_sorted = _moe_dispatch(h, slots, pad_start + cnt, padded - cnt, n_tiles * tm)
    y_sorted = _moe_experts(x_sorted, tile_expert, n_used.astype(I32).reshape(1), w_gate, w_up, w_down)
    return _moe_combine(h, y_sorted, slots, rec, ln)


def kernel(x, rel_bias, ln_gain, ln_bias, a_w_in, a_w_out, b_w_in, b_conv, b_w_out, c_w_in, c_w_out,
           d_w_in, d_ln_gain, d_ln_bias, d_w_s, d_b_s, d_w_out, f_w_gate, f_w_up, f_w_down,
           e_w_router, e_w_gate, e_w_up, e_w_down):
    bsz, seq, dm = x.shape
    t = bsz * seq
    assert seq % (8 * MOBA_BLOCK) == 0 and seq == DIL_GROUPS[-1][0] and ln_gain.shape[0] == DEPTH
    assert seq % TOKEN_TILE == 0 and t % MOE_TILE == 0
    q_scale = HEAD_DIM ** -0.5

    def ln_rows(i, *pairs):
        return jnp.concatenate([jnp.stack([ln_gain[i, p], ln_bias[i, p]]) for p in pairs], axis=0)

    def router_weights(w):
        w1, w2, _ = _split3(w.T.astype(F32))
        return jnp.stack([w1, w2])

    tri = jnp.triu(jnp.ones((TOKEN_TILE, TOKEN_TILE), BF16), k=1)

    def scaled_qkv(w):
        scale = jnp.concatenate([jnp.full((ATTN_WIDTH,), q_scale, F32), jnp.ones((2 * ATTN_WIDTH,), F32)])
        return (w * scale).astype(BF16)

    h = x
    group_cols = 3 * ATTN_WIDTH
    qkv = [_project_dilated(h, scaled_qkv(a_w_in[0][:, g * group_cols:(g + 1) * group_cols]), dil)
           for g, (_, dil) in enumerate(DIL_GROUPS)]
    tables = [_dilated_bias_table(rel_bias, dil) for _, dil in DIL_GROUPS]
    tables[2] = tables[2][:, :, DIL_BACK:]
    attn = _dilated_attention(qkv, tables)
    h = _attn_out_ffn(h.reshape(t, dm), attn.reshape(t, ATTN_WIDTH), a_w_out[0].astype(BF16),
                      f_w_gate[0].astype(BF16), f_w_up[0].astype(BF16), f_w_down[0].astype(BF16),
                      ln_rows(0, 0, 1))
    h, rec, counts = _conv_mixer(h.reshape(bsz, seq, dm), b_w_in[0].astype(BF16), b_conv[0],
                                 b_w_out[0].astype(BF16), ln_rows(1, 0), router_weights(e_w_router[0]), tri)
    h = _moe(h.reshape(t, dm), rec, counts, e_w_gate[0].astype(BF16), e_w_up[0].astype(BF16),
             e_w_down[0].astype(BF16), ln_rows(1, 1))
    qkv_c, kmean = _project_moba(h.reshape(bsz, seq, dm), scaled_qkv(c_w_in[0]))
    attn = _moba_attention(qkv_c, kmean, _moba_bias_table(rel_bias, seq // MOBA_BLOCK))
    h = _attn_out_ffn(h, attn.reshape(t, ATTN_WIDTH), c_w_out[0].astype(BF16),
                      f_w_gate[1].astype(BF16), f_w_up[1].astype(BF16), f_w_down[1].astype(BF16),
                      ln_rows(2, 0, 1))
    h, rec, counts = _gmlp_mixer(h, d_w_in[0].astype(BF16), jnp.stack([d_ln_gain[0], d_ln_bias[0]]),
                                 d_w_s[0], d_b_s[0].T, d_w_out[0].astype(BF16), ln_rows(3, 0),
                                 router_weights(e_w_router[1]), tri)
    h = _moe(h, rec, counts, e_w_gate[1].astype(BF16), e_w_up[1].astype(BF16),
             e_w_down[1].astype(BF16), ln_rows(3, 1))
    return h.reshape(bsz, seq, dm)
```

```python
import functools
import math

import jax
import jax.numpy as jnp
from jax import lax
from jax.experimental import pallas as pl
from jax.experimental.pallas import tpu as pltpu

F32 = jnp.float32
BF16 = jnp.bfloat16
I32 = jnp.int32

N_HEADS = 16
HEAD_DIM = 64
LANES = 128
HEADS_PER_TILE = LANES // HEAD_DIM
N_HEAD_TILES = N_HEADS // HEADS_PER_TILE
ATTN_WIDTH = N_HEADS * HEAD_DIM
NUM_BUCKETS = 32
MAX_DISTANCE = 2048
DIL_GROUPS = ((128, 1), (512, 4), (2048, 16))
DIL_BACK = 128
DIL_UNROLL = 8
MOBA_BLOCK = 256
MOBA_TOPK = 3
GMLP_CHUNK = 128
GMLP_GROUPS = 8
N_EXPERTS = 8
DEPTH = 4
ALPHA = (2 * DEPTH) ** 0.25
LN_EPS = 1e-5
NEG_INF = -1e30

TOKEN_TILE = 512
MOE_TILE = 512
MOE_FF_TILE = 1792
MOE_ROW_TILE = 1024
VMEM_LIMIT = 56 * 1024 * 1024


def _params(semantics, **kw):
    return pltpu.CompilerParams(dimension_semantics=semantics, vmem_limit_bytes=VMEM_LIMIT, **kw)


def _resident(shape):
    return pl.BlockSpec(shape, lambda *_: (0,) * len(shape), pipeline_mode=pl.Buffered(1))


def _layernorm(z, gain, bias):
    mu = jnp.mean(z, axis=-1, keepdims=True)
    zc = z - mu
    var = jnp.mean(zc * zc, axis=-1, keepdims=True)
    return zc * lax.rsqrt(var + LN_EPS) * gain + bias


def _gelu(x):
    return 0.5 * x * (1.0 + lax.erf(x * math.sqrt(0.5)))


def _dot(a, b):
    return jnp.dot(a, b, preferred_element_type=F32)


def _dot_nt(a, b):
    return lax.dot_general(a, b, (((1,), (1,)), ((), ())), preferred_element_type=F32)


def _split3(x):
    x1 = x.astype(BF16)
    r1 = x - x1.astype(F32)
    x2 = r1.astype(BF16)
    x3 = (r1 - x2.astype(F32)).astype(BF16)
    return x1, x2, x3


def _t5_bucket(dist):
    max_exact = NUM_BUCKETS // 2
    d = jnp.maximum(dist, 0)
    log_ratio = jnp.log(jnp.maximum(d, 1).astype(F32) / max_exact) / math.log(MAX_DISTANCE / max_exact)
    large = max_exact + (log_ratio * (NUM_BUCKETS - max_exact)).astype(I32)
    large = jnp.clip(large, max_exact, NUM_BUCKETS - 1)
    return jnp.where(d < max_exact, d, large)


def _bias_lookup(rel_bias, dist):
    onehot = (_t5_bucket(dist)[..., None] == jnp.arange(NUM_BUCKETS, dtype=I32)).astype(F32)
    out = jnp.einsum('...k,kh->...h', onehot, rel_bias.astype(F32), precision=lax.Precision.HIGHEST)
    return jnp.moveaxis(out, -1, 0)


def _dilated_bias_table(rel_bias, dil):
    qi = jnp.arange(DIL_BACK, dtype=I32)[:, None]
    kj = jnp.arange(2 * DIL_BACK, dtype=I32)[None, :]
    delta = qi + DIL_BACK - kj
    band = (delta >= 0) & (delta <= DIL_BACK)
    bias = _bias_lookup(rel_bias, delta * dil)
    return jnp.where(band[None], bias, NEG_INF)


def _moba_bias_table(rel_bias, n_blk):
    pos = jnp.arange(MOBA_BLOCK, dtype=I32)
    diff = pos[:, None] - pos[None, :]
    blocks_back = jnp.arange(n_blk - 1, -1, -1, dtype=I32)
    dist = blocks_back[None, :, None] * MOBA_BLOCK + diff[:, None, :]
    bias = jnp.where((dist >= 0)[None], _bias_lookup(rel_bias, dist), NEG_INF)
    return bias.reshape(N_HEADS, MOBA_BLOCK, n_blk * MOBA_BLOCK)


def _proj_kernel(*refs, dil, n_split):
    x_refs, w_ref, o_ref = refs[:-2], refs[-2], refs[-1]
    tm = x_refs[0].shape[0]
    rows = tm // dil
    if dil == 1:
        xb = jnp.concatenate([x_ref[...] for x_ref in x_refs], axis=1).astype(BF16)
    else:
        xb = jnp.concatenate(
            [jnp.concatenate([x_ref[pl.ds(r, rows, stride=dil), :] for x_ref in x_refs], axis=1)
             for r in range(dil)], axis=0).astype(BF16)
    width = w_ref.shape[1] // n_split
    for c in range(n_split):
        y = _dot(xb, w_ref[:, c * width:(c + 1) * width]).astype(BF16)
        for r in range(dil):
            o_ref[r, :, c * width:(c + 1) * width] = y[r * rows:(r + 1) * rows]


def _project_dilated(x, w, dil):
    bsz, seq, dm = x.shape
    n = w.shape[1]
    tm = TOKEN_TILE
    return pl.pallas_call(
        functools.partial(_proj_kernel, dil=dil, n_split=3),
        grid=(bsz, seq // tm),
        in_specs=[pl.BlockSpec((None, tm, LANES), lambda b, j, c=c: (b, j, c)) for c in range(dm // LANES)]
        + [_resident((dm, n))],
        out_specs=pl.BlockSpec((None, dil, tm // dil, n), lambda b, j: (b, 0, j, 0)),
        out_shape=jax.ShapeDtypeStruct((bsz, dil, seq // dil, n), BF16),
        compiler_params=_params(("parallel", "parallel")),
        name=f"proj_dil{dil}",
    )(*([x] * (dm // LANES)), w)


def _proj_kmean_kernel(x_ref, w_ref, o_ref, km_ref):
    tm = x_ref.shape[0]
    per_step = tm // MOBA_BLOCK
    j = pl.program_id(1)
    xb = x_ref[...].astype(BF16)
    width = ATTN_WIDTH
    for c in range(3):
        y = _dot(xb, w_ref[:, c * width:(c + 1) * width])
        o_ref[:, c * width:(c + 1) * width] = y.astype(BF16)
        if c == 1:
            for jj in range(per_step):
                km_ref[pl.ds(j * per_step + jj, 1), :] = jnp.mean(
                    y[jj * MOBA_BLOCK:(jj + 1) * MOBA_BLOCK], axis=0, keepdims=True)


def _project_moba(x, w):
    bsz, seq, dm = x.shape
    n = w.shape[1]
    tm = TOKEN_TILE
    n_blk = seq // MOBA_BLOCK
    return pl.pallas_call(
        _proj_kmean_kernel,
        grid=(bsz, seq // tm),
        in_specs=[pl.BlockSpec((None, tm, dm), lambda b, j: (b, j, 0)),
                  _resident((dm, n))],
        out_specs=[pl.BlockSpec((None, tm, n), lambda b, j: (b, j, 0)),
                   pl.BlockSpec((None, n_blk, ATTN_WIDTH), lambda b, j: (b, 0, 0))],
        out_shape=[jax.ShapeDtypeStruct((bsz, seq, n), BF16),
                   jax.ShapeDtypeStruct((bsz, n_blk, ATTN_WIDTH), F32)],
        compiler_params=_params(("parallel", "arbitrary")),
        name="proj_moba",
    )(x, w)


def _two_head_partials(units, is_a):
    scores = []
    for q, kw, _, bias_a, bias_b in units:
        zero = jnp.zeros_like(q)
        scores.append((_dot_nt(jnp.where(is_a, q, zero), kw) + bias_a,
                       _dot_nt(jnp.where(is_a, zero, q), kw) + bias_b))
    stats = []
    for pair in scores:
        row = []
        for s in pair:
            m = jnp.max(s, axis=-1, keepdims=True)
            row.append((m, jnp.exp(s - m).astype(BF16)))
        stats.append(row)
    out = []
    for (_, _, vw, _, _), ((m_a, p_a), (m_b, p_b)) in zip(units, stats):
        pv_a, pv_b = _two_head_pv(p_a, p_b, vw)
        out.append((jnp.where(is_a, m_a, m_b), pltpu.roll(jnp.where(is_a, pv_b, pv_a), HEAD_DIM, axis=1),
                    jnp.where(is_a, pv_a, pv_b)))
    return out


def _two_head_pv(p_a, p_b, vw):
    own_a = lax.broadcasted_iota(I32, vw.shape, 1) < HEAD_DIM
    one = jnp.ones_like(vw)
    return _dot(p_a, jnp.where(own_a, vw, one)), _dot(p_b, jnp.where(own_a, one, vw))


def _dilated_attn_kernel(q0, k0, v0, q1, k1, v1, q2, k2, v2, b0, b1, b2, o_ref, m_s, l_s, a_s, tmp_s):
    nb = DIL_BACK
    is_a = lax.broadcasted_iota(I32, (nb, LANES), 1) < HEAD_DIM
    n_res2, n_res1 = q2.shape[0], q1.shape[0]
    n_blk1 = q1.shape[1] // nb
    n_blk0 = q0.shape[0] // nb

    def merge(rows, m, l, o):
        m_old, l_old, a_old = m_s[rows, :], l_s[rows, :], a_s[rows, :]
        m_new = jnp.maximum(m_old, m)
        w_old, w_new = jnp.exp(m_old - m_new), jnp.exp(m - m_new)
        return m_new, w_old * l_old + w_new * l, w_old * a_old + w_new * o

    unroll = DIL_UNROLL

    pitch = tmp_s.shape[1] // n_res2

    def group2(i, carry):
        res = [i * unroll + u for u in range(unroll)]
        parts = _two_head_partials([(q2[r], k2[r], v2[r], b2[0], b2[1]) for r in res], is_a)
        for r, part in zip(res, parts):
            rows = pl.ds(pl.multiple_of(r * pitch, 8), nb)
            for c in range(3):
                tmp_s[c, rows, :] = part[c]
        return carry

    lax.fori_loop(0, n_res2 // unroll, group2, 0)

    def to_token_major(s, carry):
        dst = pl.ds(pl.multiple_of(s * n_res2, n_res2), n_res2)
        for c, ref in enumerate((m_s, l_s, a_s)):
            ref[dst, :] = tmp_s[c, pl.ds(s, n_res2, stride=pitch), :]
        return carry

    lax.fori_loop(0, nb, to_token_major, 0, unroll=8)

    res_per_body = max(1, unroll // n_blk1)

    def group1(i, carry):
        units, where = [], []
        for rr in range(res_per_body):
            r = i * res_per_body + rr
            units.append((q1[r, 0:nb, :], k1[r, 0:nb, :], v1[r, 0:nb, :], b1[0, :, nb:], b1[1, :, nb:]))
            where.append((r, 0))
            for n in range(1, n_blk1):
                keys = slice((n - 1) * nb, (n + 1) * nb)
                units.append((q1[r, n * nb:(n + 1) * nb, :], k1[r, keys, :], v1[r, keys, :], b1[0], b1[1]))
                where.append((r, n))
        for (r, n), (m, l, o) in zip(where, _two_head_partials(units, is_a)):
            rows = pl.ds(n * nb * n_res1 + r, nb, stride=n_res1)
            m, l, o = merge(rows, m, l, o)
            m_s[rows, :] = m
            l_s[rows, :] = l
            a_s[rows, :] = o
        return carry

    lax.fori_loop(0, n_res1 // res_per_body, group1, 0)

    def finish(rows, m, l, o):
        _, l, o = merge(rows, m, l, o)
        o_ref[rows, :] = (o / l).astype(o_ref.dtype)

    def unit0(n):
        aligned = (lambda v: v) if isinstance(n, int) else (lambda v: pl.multiple_of(v, nb))
        q_rows = pl.ds(aligned(n * nb), nb)
        k_rows = pl.ds(aligned((n - 1) * nb), 2 * nb)
        return q_rows, (q0[q_rows, :], k0[k_rows, :], v0[k_rows, :], b0[0], b0[1])

    def blocks0(units):
        for (q_rows, _), (m, l, o) in zip(units, _two_head_partials([u for _, u in units], is_a)):
            finish(q_rows, m, l, o)

    first = (pl.ds(0, nb), (q0[0:nb, :], k0[0:nb, :], v0[0:nb, :], b0[0, :, nb:], b0[1, :, nb:]))
    blocks0([first] + [unit0(n) for n in range(1, unroll)])

    def group0(i, carry):
        blocks0([unit0(i * unroll + u) for u in range(unroll)])
        return carry

    lax.fori_loop(1, n_blk0 // unroll, group0, 0)


def _dilated_attention(qkv, tables):
    bsz = qkv[0].shape[0]
    seq = qkv[0].shape[1] * qkv[0].shape[2]
    nt = N_HEAD_TILES
    in_specs, args = [], []
    for g, (_, dil) in enumerate(DIL_GROUPS):
        sub = seq // dil
        for part in range(3):
            if dil == 1:
                spec = pl.BlockSpec((None, None, sub, LANES), lambda t, b, part=part: (b, 0, 0, part * nt + t))
            else:
                spec = pl.BlockSpec((None, dil, sub, LANES), lambda t, b, part=part: (b, 0, 0, part * nt + t))
            in_specs.append(spec)
            args.append(qkv[g])
    for g in range(3):
        width = tables[g].shape[2]
        in_specs.append(pl.BlockSpec((HEADS_PER_TILE, DIL_BACK, width), lambda t, b: (t, 0, 0)))
        args.append(tables[g])
    return pl.pallas_call(
        _dilated_attn_kernel,
        grid=(nt, bsz),
        in_specs=in_specs,
        out_specs=pl.BlockSpec((None, seq, LANES), lambda t, b: (b, 0, t)),
        out_shape=jax.ShapeDtypeStruct((bsz, seq, ATTN_WIDTH), BF16),
        scratch_shapes=[pltpu.VMEM((seq, LANES), F32)] * 3
        + [pltpu.VMEM((3, DIL_GROUPS[2][1] * (DIL_BACK + 8), LANES), F32)],
        compiler_params=_params(("parallel", "parallel")),
        name="dilated_attn",
    )(*args)


def _moba_kernel(q_ref, k_ref, v_ref, km_ref, bias_ref, o_ref, kaug_s, nsel_s):
    seq = q_ref.shape[0]
    blk = MOBA_BLOCK
    n_blk = seq // blk
    is_a_row = lax.broadcasted_iota(I32, (n_blk, LANES), 1) < HEAD_DIM
    is_a = lax.broadcasted_iota(I32, (blk, LANES), 1) < HEAD_DIM

    kaug_s[:, :LANES] = k_ref[...]
    key_blk = lax.broadcasted_iota(I32, (seq, LANES), 0) // blk
    col = lax.broadcasted_iota(I32, (seq, LANES), 1)
    kaug_s[:, LANES:] = jnp.where(col == key_blk, NEG_INF, 0.0).astype(BF16)

    q_all = q_ref[...]
    km = km_ref[...]
    q_blk = lax.broadcasted_iota(I32, (n_blk, seq), 1) // blk
    m_idx = lax.broadcasted_iota(I32, (n_blk, seq), 0)
    past = m_idx < q_blk
    n_sel = min(MOBA_TOPK, n_blk - 1)
    pad_rows = jnp.zeros((LANES - n_blk, LANES), F32)
    for hd in range(HEADS_PER_TILE):
        kmh = jnp.where(is_a_row if hd == 0 else jnp.logical_not(is_a_row), km, 0.0)
        gate = sum(_dot_nt(piece, q_all) for piece in _split3(kmh))
        gate = jnp.where(past, gate, NEG_INF)
        rank = jnp.zeros((n_blk, seq), F32)
        for mp in range(n_blk - 1):
            row = gate[mp:mp + 1, :]
            beats = jnp.where(row > gate, 1.0, jnp.where(row == gate, jnp.where(mp < m_idx, 1.0, 0.0), 0.0))
            rank = rank + jnp.where(mp < q_blk, beats, 0.0)
        unselected = jnp.where(past, jnp.where(rank < n_sel, 0.0, 1.0), 1.0)
        unselected = jnp.where(m_idx == q_blk, 0.0, unselected)
        for c in range(seq // LANES):
            tile = jnp.concatenate([unselected[:, c * LANES:(c + 1) * LANES], pad_rows], axis=0)
            nsel_s[hd, c * LANES:(c + 1) * LANES, :] = tile.T.astype(BF16)

    def block_scores(n):
        rows = slice(n * blk, (n + 1) * blk)
        qn = q_ref[rows, :]
        zero = jnp.zeros_like(qn)
        scores = []
        for hd in range(HEADS_PER_TILE):
            qh = jnp.where(is_a, qn, zero) if hd == 0 else jnp.where(is_a, zero, qn)
            qa = jnp.concatenate([qh, nsel_s[hd, rows, :]], axis=1)
            scores.append(_dot_nt(qa, kaug_s[0:(n + 1) * blk, :]) + bias_ref[hd, :, (n_blk - 1 - n) * blk:])
        return scores

    def block_finish(n, scores):
        probs = [jnp.exp(s - jnp.max(s, axis=-1, keepdims=True)).astype(BF16) for s in scores]
        pv_a, pv_b = _two_head_pv(probs[0], probs[1], v_ref[0:(n + 1) * blk, :])
        denom = pltpu.roll(jnp.where(is_a, pv_b, pv_a), HEAD_DIM, axis=1)
        o_ref[n * blk:(n + 1) * blk, :] = (jnp.where(is_a, pv_a, pv_b) / denom).astype(o_ref.dtype)

    for n in range(n_blk):
        block_finish(n, block_scores(n))


def _moba_attention(qkv, kmean, table):
    bsz, seq, _ = qkv.shape
    nt = N_HEAD_TILES
    n_blk = seq // MOBA_BLOCK
    return pl.pallas_call(
        _moba_kernel,
        grid=(nt, bsz),
        in_specs=[pl.BlockSpec((None, seq, LANES), lambda t, b: (b, 0, t)),
                  pl.BlockSpec((None, seq, LANES), lambda t, b: (b, 0, nt + t)),
                  pl.BlockSpec((None, seq, LANES), lambda t, b: (b, 0, 2 * nt + t)),
                  pl.BlockSpec((None, n_blk, LANES), lambda t, b: (b, 0, t)),
                  pl.BlockSpec((HEADS_PER_TILE, MOBA_BLOCK, seq), lambda t, b: (t, 0, 0))],
        out_specs=pl.BlockSpec((None, seq, LANES), lambda t, b: (b, 0, t)),
        out_shape=jax.ShapeDtypeStruct((bsz, seq, ATTN_WIDTH), BF16),
        scratch_shapes=[pltpu.VMEM((seq, 2 * LANES), BF16),
                        pltpu.VMEM((HEADS_PER_TILE, seq, LANES), BF16)],
        compiler_params=_params(("parallel", "parallel")),
        name="moba_attn",
    )(qkv, qkv, qkv, kmean, table)


def _attn_out_ffn_kernel(x_ref, a_ref, wo_ref, wg_ref, wu_ref, wd_ref, ln_ref, o_ref, *, n_split):
    x = x_ref[...]
    h = _layernorm(ALPHA * x + _dot(a_ref[...], wo_ref[...]), ln_ref[0:1, :], ln_ref[1:2, :])
    hb = h.astype(BF16)
    width = wg_ref.shape[1] // n_split
    y = None
    for c in range(n_split):
        cols = slice(c * width, (c + 1) * width)
        g = _dot(hb, wg_ref[:, cols])
        u = _dot(hb, wu_ref[:, cols])
        part = _dot((g * jax.nn.sigmoid(g) * u).astype(BF16), wd_ref[cols, :])
        y = part if y is None else y + part
    o_ref[...] = _layernorm(ALPHA * h + y, ln_ref[2:3, :], ln_ref[3:4, :])


def _attn_out_ffn(x, attn, w_out, w_gate, w_up, w_down, ln):
    t, dm = x.shape
    ff = w_gate.shape[1]
    tm = TOKEN_TILE
    n_split = 2 if (ff // 2) % LANES == 0 else 1
    return pl.pallas_call(
        functools.partial(_attn_out_ffn_kernel, n_split=n_split),
        grid=(t // tm,),
        in_specs=[pl.BlockSpec((tm, dm), lambda i: (i, 0)),
                  pl.BlockSpec((tm, attn.shape[1]), lambda i: (i, 0)),
                  _resident(w_out.shape), _resident(w_gate.shape), _resident(w_up.shape),
                  _resident(w_down.shape), _resident(ln.shape)],
        out_specs=pl.BlockSpec((tm, dm), lambda i: (i, 0)),
        out_shape=jax.ShapeDtypeStruct((t, dm), F32),
        compiler_params=_params(("parallel",)),
        name="attn_out_ffn",
    )(x, attn, w_out, w_gate, w_up, w_down, ln)


def _route(h, wr_ref, tri_ref, cnt_s, rec_ref, cnt_ref):
    tm = h.shape[0]
    h1, h2, _ = _split3(h)
    logits = _dot_nt(wr_ref[0], h1) + _dot_nt(wr_ref[1], h1) + _dot_nt(wr_ref[0], h2)
    e_idx = lax.broadcasted_iota(I32, (N_EXPERTS, tm), 0).astype(F32)
    top1 = jnp.max(logits, axis=0, keepdims=True)
    idx1 = jnp.min(jnp.where(logits == top1, e_idx, float(N_EXPERTS)), axis=0, keepdims=True)
    rest = jnp.where(e_idx == idx1, -jnp.inf, logits)
    top2 = jnp.max(rest, axis=0, keepdims=True)
    idx2 = jnp.min(jnp.where(rest == top2, e_idx, float(N_EXPERTS)), axis=0, keepdims=True)
    z = jnp.exp(top2 - top1)
    gate1 = 1.0 / (1.0 + z)
    gate2 = z / (1.0 + z)
    hot1 = jnp.where(e_idx == idx1, 1.0, 0.0)
    hot2 = jnp.where(e_idx == idx2, 1.0, 0.0)
    hot = hot1 + hot2
    before = _dot(hot.astype(BF16), tri_ref[...]) + cnt_s[:, 0:1]
    pos1 = jnp.sum(hot1 * before, axis=0, keepdims=True)
    pos2 = jnp.sum(hot2 * before, axis=0, keepdims=True)
    cnt_s[...] = cnt_s[...] + jnp.sum(hot, axis=1, keepdims=True)
    cnt_ref[...] = cnt_s[...]
    rec_ref[...] = jnp.concatenate([gate1, gate2, idx1, idx2, pos1, pos2, jnp.zeros((2, tm), F32)], axis=0)


def _conv_mixer_kernel(x_ref, wi_ref, cw_ref, wo_ref, ln_ref, wr_ref, tri_ref,
                       o_ref, rec_ref, cnt_ref, y_s, cnt_s):
    tm, dm = x_ref.shape
    first = (pl.program_id(0) == 0) & (pl.program_id(1) == 0)

    @pl.when(first)
    def _():
        cnt_s[...] = jnp.zeros_like(cnt_s)

    @pl.when(pl.program_id(1) == 0)
    def _():
        y_s[0:8, :] = jnp.zeros((8, dm), F32)

    x = x_ref[...]
    xb = x.astype(BF16)
    gate_c = _dot(xb, wi_ref[:, dm:2 * dm])
    x_in = _dot(xb, wi_ref[:, 2 * dm:])
    y_s[8:, :] = gate_c * x_in
    z = cw_ref[0:1, :] * y_s[6:6 + tm, :] + cw_ref[1:2, :] * y_s[7:7 + tm, :] + cw_ref[2:3, :] * y_s[8:, :]
    y_s[0:8, :] = y_s[tm:, :]
    gate_b = _dot(xb, wi_ref[:, :dm])
    mixed = _dot((gate_b * z).astype(BF16), wo_ref[...])
    h = _layernorm(ALPHA * x + mixed, ln_ref[0:1, :], ln_ref[1:2, :])
    o_ref[...] = h
    _route(h, wr_ref, tri_ref, cnt_s, rec_ref, cnt_ref)


def _conv_mixer(x, w_in, conv_w, w_out, ln, wr, tri):
    bsz, seq, dm = x.shape
    tm = TOKEN_TILE
    nj = seq // tm
    return pl.pallas_call(
        _conv_mixer_kernel,
        grid=(bsz, nj),
        in_specs=[pl.BlockSpec((None, tm, dm), lambda b, j: (b, j, 0)),
                  _resident(w_in.shape), _resident(conv_w.shape), _resident(w_out.shape),
                  _resident(ln.shape), _resident(wr.shape), _resident(tri.shape)],
        out_specs=[pl.BlockSpec((None, tm, dm), lambda b, j: (b, j, 0)),
                   pl.BlockSpec((8, tm), lambda b, j: (0, b * nj + j)),
                   pl.BlockSpec((N_EXPERTS, LANES), lambda b, j: (0, 0))],
        out_shape=[jax.ShapeDtypeStruct((bsz, seq, dm), F32),
                   jax.ShapeDtypeStruct((8, bsz * seq), F32),
                   jax.ShapeDtypeStruct((N_EXPERTS, LANES), F32)],
        scratch_shapes=[pltpu.VMEM((tm + 8, dm), F32), pltpu.VMEM((N_EXPERTS, LANES), F32)],
        compiler_params=_params(("arbitrary", "arbitrary")),
        name="conv_mixer",
    )(x, w_in, conv_w, w_out, ln, wr, tri)


def _gmlp_mixer_kernel(x_ref, wi_ref, dln_ref, ws_ref, bs_ref, wo_ref, ln_ref, wr_ref, tri_ref,
                       o_ref, rec_ref, cnt_ref, mix_s, cnt_s):
    tm, dm = x_ref.shape
    ch = GMLP_CHUNK
    gw = dm // GMLP_GROUPS

    @pl.when(pl.program_id(0) == 0)
    def _():
        cnt_s[...] = jnp.zeros_like(cnt_s)

    x = x_ref[...]
    xb = x.astype(BF16)
    u = _gelu(_dot(xb, wi_ref[:, :dm]))
    v = _gelu(_dot(xb, wi_ref[:, dm:]))
    vb = _layernorm(v, dln_ref[0:1, :], dln_ref[1:2, :]).astype(BF16)
    lower = lax.broadcasted_iota(I32, (ch, ch), 0) >= lax.broadcasted_iota(I32, (ch, ch), 1)
    for g in range(GMLP_GROUPS):
        ws = jnp.where(lower, ws_ref[g], 0.0).astype(BF16)
        for c in range(tm // ch):
            blk = _dot(ws, vb[c * ch:(c + 1) * ch, g * gw:(g + 1) * gw]) + bs_ref[:, g:g + 1]
            mix_s[c * ch:(c + 1) * ch, g * gw:(g + 1) * gw] = blk
    mixed = _dot((u * mix_s[...]).astype(BF16), wo_ref[...])
    h = _layernorm(ALPHA * x + mixed, ln_ref[0:1, :], ln_ref[1:2, :])
    o_ref[...] = h
    _route(h, wr_ref, tri_ref, cnt_s, rec_ref, cnt_ref)


def _gmlp_mixer(x, w_in, dln, w_s, b_s_t, w_out, ln, wr, tri):
    t, dm = x.shape
    tm = TOKEN_TILE
    return pl.pallas_call(
        _gmlp_mixer_kernel,
        grid=(t // tm,),
        in_specs=[pl.BlockSpec((tm, dm), lambda i: (i, 0)),
                  _resident(w_in.shape), _resident(dln.shape), _resident(w_s.shape),
                  _resident(b_s_t.shape), _resident(w_out.shape), _resident(ln.shape),
                  _resident(wr.shape), _resident(tri.shape)],
        out_specs=[pl.BlockSpec((tm, dm), lambda i: (i, 0)),
                   pl.BlockSpec((8, tm), lambda i: (0, i)),
                   pl.BlockSpec((N_EXPERTS, LANES), lambda i: (0, 0))],
        out_shape=[jax.ShapeDtypeStruct((t, dm), F32),
                   jax.ShapeDtypeStruct((8, t), F32),
                   jax.ShapeDtypeStruct((N_EXPERTS, LANES), F32)],
        scratch_shapes=[pltpu.VMEM((tm, dm), F32), pltpu.VMEM((N_EXPERTS, LANES), F32)],
        compiler_params=_params(("arbitrary",)),
        name="gmlp_mixer",
    )(x, w_in, dln, w_s, b_s_t, w_out, ln, wr, tri)


def _load_slots(slot_hbm, slot_s, sem, i):
    copies = [pltpu.make_async_copy(slot_hbm.at[c, i], slot_s[c], sem.at[c]) for c in range(2)]
    for cp in copies:
        cp.start()
    for cp in copies:
        cp.wait()


def _start_row_copies(n_rows, row_copy, slot_s):
    def body(j, carry):
        base = pl.multiple_of(j * 8, 8)
        for u in range(8):
            for c in range(2):
                row_copy(base + u, c, slot_s[c][base + u]).start(priority=(u + c) % 2)
        return carry

    lax.fori_loop(0, n_rows // 8, body, 0)


def _moe_dispatch_kernel(fs_ref, fc_ref, slot_hbm, h_ref, x_hbm, slot0_s, slot1_s, zero_s, idx_sem, row_sem):
    i = pl.program_id(0)
    tm = h_ref.shape[0]
    slot_s = (slot0_s, slot1_s)
    _load_slots(slot_hbm, slot_s, idx_sem, i)

    def row_copy(r, c, slot):
        return pltpu.make_async_copy(h_ref.at[pl.ds(r, 1), :], x_hbm.at[pl.ds(slot, 1), :], row_sem.at[c])

    _start_row_copies(tm, row_copy, slot_s)

    @pl.when(i == 0)
    def _():
        zero_s[...] = jnp.zeros_like(zero_s)

        def fill_copy(row):
            return pltpu.make_async_copy(zero_s.at[pl.ds(0, 1), :], x_hbm.at[pl.ds(row, 1), :], row_sem.at[2])

        for e in range(N_EXPERTS):
            def fill(j, carry, e=e):
                fill_copy(fs_ref[e] + j).start()
                return carry

            lax.fori_loop(0, fc_ref[e], fill, 0)

            def drain(j, carry):
                fill_copy(0).wait()
                return carry

            lax.fori_loop(0, fc_ref[e], drain, 0)

    for c in range(2):
        pltpu.make_async_copy(h_ref, x_hbm.at[pl.ds(0, tm), :], row_sem.at[c]).wait()


def _moe_dispatch(h, slots, fill_start, fill_count, n_slots):
    t, dm = h.shape
    tm = slots.shape[2]
    grid_spec = pltpu.PrefetchScalarGridSpec(
        num_scalar_prefetch=2,
        grid=(t // tm,),
        in_specs=[pl.BlockSpec(memory_space=pl.ANY),
                  pl.BlockSpec((tm, dm), lambda i, fs, fc: (i, 0))],
        out_specs=pl.BlockSpec(memory_space=pl.ANY),
        scratch_shapes=[pltpu.SMEM((tm,), I32), pltpu.SMEM((tm,), I32), pltpu.VMEM((8, dm), F32),
                        pltpu.SemaphoreType.DMA((2,)), pltpu.SemaphoreType.DMA((3,))],
    )
    return pl.pallas_call(
        _moe_dispatch_kernel,
        grid_spec=grid_spec,
        out_shape=jax.ShapeDtypeStruct((n_slots, dm), F32),
        compiler_params=_params(("arbitrary",)),
        name="moe_dispatch",
    )(fill_start, fill_count, slots, h)


def _moe_expert_kernel(te_ref, nu_ref, x_ref, wg_ref, wu_ref, wd_ref, y_ref, xb_s):
    i, f = pl.program_id(0), pl.program_id(1)

    @pl.when(i < nu_ref[0])
    def _():
        @pl.when(f == 0)
        def _():
            xb_s[...] = x_ref[...].astype(BF16)
            y_ref[...] = jnp.zeros_like(y_ref)

        xb = xb_s[...]
        g = _dot(xb, wg_ref[...])
        u = _dot(xb, wu_ref[...])
        y_ref[...] += _dot((g * jax.nn.sigmoid(g) * u).astype(BF16), wd_ref[...])


def _moe_experts(x_sorted, tile_expert, n_used, w_gate, w_up, w_down):
    n_slots, dm = x_sorted.shape
    tm = MOE_TILE
    ff = w_gate.shape[2]
    tf = MOE_FF_TILE

    def tile_block(i, f, te, nu):
        return jnp.minimum(i, nu[0] - 1), 0

    def col_block(i, f, te, nu):
        return te[i], 0, jnp.where(i < nu[0], f, ff // tf - 1)

    def row_block(i, f, te, nu):
        return te[i], jnp.where(i < nu[0], f, ff // tf - 1), 0

    grid_spec = pltpu.PrefetchScalarGridSpec(
        num_scalar_prefetch=2,
        grid=(n_slots // tm, ff // tf),
        in_specs=[pl.BlockSpec((tm, dm), tile_block),
                  pl.BlockSpec((None, dm, tf), col_block),
                  pl.BlockSpec((None, dm, tf), col_block),
                  pl.BlockSpec((None, tf, dm), row_block)],
        out_specs=pl.BlockSpec((tm, dm), tile_block),
        scratch_shapes=[pltpu.VMEM((tm, dm), BF16)],
    )
    return pl.pallas_call(
        _moe_expert_kernel,
        grid_spec=grid_spec,
        out_shape=jax.ShapeDtypeStruct((n_slots, dm), F32),
        compiler_params=_params(("arbitrary", "arbitrary")),
        name="moe_experts",
    )(tile_expert, n_used, x_sorted, w_gate, w_up, w_down)


def _moe_combine_kernel(slot_hbm, y_hbm, h_ref, rec_ref, ln_ref, o_ref, slot0_s, slot1_s, y_s, idx_sem, row_sem):
    i = pl.program_id(0)
    tm = h_ref.shape[0]
    slot_s = (slot0_s, slot1_s)
    _load_slots(slot_hbm, slot_s, idx_sem, i)

    def row_copy(r, c, slot):
        return pltpu.make_async_copy(y_hbm.at[pl.ds(slot, 1), :], y_s.at[c, pl.ds(r, 1), :], row_sem.at[c])

    _start_row_copies(tm, row_copy, slot_s)
    for c in range(2):
        pltpu.make_async_copy(y_hbm.at[pl.ds(0, tm), :], y_s.at[c], row_sem.at[c]).wait()

    pad_rows = jnp.zeros((LANES - 8, LANES), F32)
    for c in range(tm // LANES):
        rows = slice(c * LANES, (c + 1) * LANES)
        gates = jnp.concatenate([rec_ref[:, rows], pad_rows], axis=0).T
        y = gates[:, 0:1] * y_s[0, rows, :] + gates[:, 1:2] * y_s[1, rows, :]
        o_ref[rows, :] = _layernorm(ALPHA * h_ref[rows, :] + y, ln_ref[0:1, :], ln_ref[1:2, :])


def _moe_combine(h, y_sorted, slots, rec, ln):
    t, dm = h.shape
    tm = slots.shape[2]
    return pl.pallas_call(
        _moe_combine_kernel,
        grid=(t // tm,),
        in_specs=[pl.BlockSpec(memory_space=pl.ANY),
                  pl.BlockSpec(memory_space=pl.ANY),
                  pl.BlockSpec((tm, dm), lambda i: (i, 0)),
                  pl.BlockSpec((8, tm), lambda i: (0, i)),
                  _resident(ln.shape)],
        out_specs=pl.BlockSpec((tm, dm), lambda i: (i, 0)),
        out_shape=jax.ShapeDtypeStruct((t, dm), F32),
        scratch_shapes=[pltpu.SMEM((tm,), I32), pltpu.SMEM((tm,), I32), pltpu.VMEM((2, tm, dm), F32),
                        pltpu.SemaphoreType.DMA((2,)), pltpu.SemaphoreType.DMA((2,))],
        compiler_params=_params(("arbitrary",)),
        name="moe_combine",
    )(slots, y_sorted, h, rec, ln)


def _moe(h, rec, counts, w_gate, w_up, w_down, ln):
    t, dm = h.shape
    tm = MOE_TILE
    n_tiles = (2 * t) // tm + N_EXPERTS
    cnt = counts[:, 0].astype(I32)
    padded = (cnt + tm - 1) // tm * tm
    pad_end = jnp.cumsum(padded)
    pad_start = pad_end - padded
    experts = jnp.arange(N_EXPERTS, dtype=I32)

    def slot_of(e_row, pos_row):
        e = e_row.astype(I32)
        start = jnp.sum(jnp.where(e[None, :] == experts[:, None], pad_start[:, None], 0), axis=0)
        return start + pos_row.astype(I32)

    slots = jnp.stack([slot_of(rec[2], rec[4]), slot_of(rec[3], rec[5])]).reshape(2, t // MOE_ROW_TILE, MOE_ROW_TILE)
    n_used = pad_end[-1] // tm
    tile_start = jnp.minimum(jnp.arange(n_tiles, dtype=I32), n_used - 1) * tm
    tile_expert = jnp.sum(tile_start[:, None] >= pad_end[None, :], axis=1).astype(I32)
    x_sorted = _moe_dispatch(h, slots, pad_start + cnt, padded - cnt, n_tiles * tm)
    y_sorted = _moe_experts(x_sorted, tile_expert, n_used.astype(I32).reshape(1), w_gate, w_up, w_down)
    return _moe_combine(h, y_sorted, slots, rec, ln)


def kernel(x, rel_bias, ln_gain, ln_bias, a_w_in, a_w_out, b_w_in, b_conv, b_w_out, c_w_in, c_w_out,
           d_w_in, d_ln_gain, d_ln_bias, d_w_s, d_b_s, d_w_out, f_w_gate, f_w_up, f_w_down,
           e_w_router, e_w_gate, e_w_up, e_w_down):
    bsz, seq, dm = x.shape
    t = bsz * seq
    assert seq % (8 * MOBA_BLOCK) == 0 and seq == DIL_GROUPS[-1][0] and ln_gain.shape[0] == DEPTH
    assert seq % TOKEN_TILE == 0 and t % MOE_TILE == 0
    q_scale = HEAD_DIM ** -0.5

    def ln_rows(i, *pairs):
        return jnp.concatenate([jnp.stack([ln_gain[i, p], ln_bias[i, p]]) for p in pairs], axis=0)

    def router_weights(w):
        w1, w2, _ = _split3(w.T.astype(F32))
        return jnp.stack([w1, w2])

    tri = jnp.triu(jnp.ones((TOKEN_TILE, TOKEN_TILE), BF16), k=1)

    def scaled_qkv(w):
        scale = jnp.concatenate([jnp.full((ATTN_WIDTH,), q_scale, F32), jnp.ones((2 * ATTN_WIDTH,), F32)])
        return (w * scale).astype(BF16)

    h = x
    group_cols = 3 * ATTN_WIDTH
    qkv = [_project_dilated(h, scaled_qkv(a_w_in[0][:, g * group_cols:(g + 1) * group_cols]), dil)
           for g, (_, dil) in enumerate(DIL_GROUPS)]
    tables = [_dilated_bias_table(rel_bias, dil) for _, dil in DIL_GROUPS]
    tables[2] = tables[2][:, :, DIL_BACK:]
    attn = _dilated_attention(qkv, tables)
    h = _attn_out_ffn(h.reshape(t, dm), attn.reshape(t, ATTN_WIDTH), a_w_out[0].astype(BF16),
                      f_w_gate[0].astype(BF16), f_w_up[0].astype(BF16), f_w_down[0].astype(BF16),
                      ln_rows(0, 0, 1))
    h, rec, counts = _conv_mixer(h.reshape(bsz, seq, dm), b_w_in[0].astype(BF16), b_conv[0],
                                 b_w_out[0].astype(BF16), ln_rows(1, 0), router_weights(e_w_router[0]), tri)
    h = _moe(h.reshape(t, dm), rec, counts, e_w_gate[0].astype(BF16), e_w_up[0].astype(BF16),
             e_w_down[0].astype(BF16), ln_rows(1, 1))
    qkv_c, kmean = _project_moba(h.reshape(bsz, seq, dm), scaled_qkv(c_w_in[0]))
    attn = _moba_attention(qkv_c, kmean, _moba_bias_table(rel_bias, seq // MOBA_BLOCK))
    h = _attn_out_ffn(h, attn.reshape(t, ATTN_WIDTH), c_w_out[0].astype(BF16),
                      f_w_gate[1].astype(BF16), f_w_up[1].astype(BF16), f_w_down[1].astype(BF16),
                      ln_rows(2, 0, 1))
    h, rec, counts = _gmlp_mixer(h, d_w_in[0].astype(BF16), jnp.stack([d_ln_gain[0], d_ln_bias[0]]),
                                 d_w_s[0], d_b_s[0].T, d_w_out[0].astype(BF16), ln_rows(3, 0),
                                 router_weights(e_w_router[1]), tri)
    h = _moe(h, rec, counts, e_w_gate[1].astype(BF16), e_w_up[1].astype(BF16),
             e_w_down[1].astype(BF16), ln_rows(3, 1))
    return h.reshape(bsz, seq, dm)
```

```python
import functools
import math

import jax
import jax.numpy as jnp
from jax import lax
from jax.experimental import pallas as pl
from jax.experimental.pallas import tpu as pltpu

F32 = jnp.float32
BF16 = jnp.bfloat16
I32 = jnp.int32

N_HEADS = 16
HEAD_DIM = 64
LANES = 128
HEADS_PER_TILE = LANES // HEAD_DIM
N_HEAD_TILES = N_HEADS // HEADS_PER_TILE
ATTN_WIDTH = N_HEADS * HEAD_DIM
NUM_BUCKETS = 32
MAX_DISTANCE = 2048
DIL_GROUPS = ((128, 1), (512, 4), (2048, 16))
DIL_BACK = 128
DIL_UNROLL = 8
MOBA_BLOCK = 256
MOBA_TOPK = 3
GMLP_CHUNK = 128
GMLP_GROUPS = 8
N_EXPERTS = 8
DEPTH = 4
ALPHA = (2 * DEPTH) ** 0.25
LN_EPS = 1e-5
NEG_INF = -1e30
LOG2_E = math.log2(math.e)

TOKEN_TILE = 512
FFN_TILE = 1024
FFN_SPLIT = 256
MOE_TILE = 512
MOE_FF_TILE = 1792
MOE_ROW_TILE = 1024
VMEM_LIMIT = 56 * 1024 * 1024


def _params(semantics, **kw):
    return pltpu.CompilerParams(dimension_semantics=semantics, vmem_limit_bytes=VMEM_LIMIT, **kw)


def _resident(shape):
    return pl.BlockSpec(shape, lambda *_: (0,) * len(shape), pipeline_mode=pl.Buffered(1))


def _layernorm(z, gain, bias):
    mu = jnp.mean(z, axis=-1, keepdims=True)
    zc = z - mu
    var = jnp.mean(zc * zc, axis=-1, keepdims=True)
    return zc * lax.rsqrt(var + LN_EPS) * gain + bias


def _gelu(x):
    return 0.5 * x * (1.0 + lax.erf(x * math.sqrt(0.5)))


def _dot(a, b):
    return jnp.dot(a, b, preferred_element_type=F32)


def _dot_nt(a, b):
    return lax.dot_general(a, b, (((1,), (1,)), ((), ())), preferred_element_type=F32)


def _split3(x):
    x1 = x.astype(BF16)
    r1 = x - x1.astype(F32)
    x2 = r1.astype(BF16)
    x3 = (r1 - x2.astype(F32)).astype(BF16)
    return x1, x2, x3


def _t5_bucket(dist):
    max_exact = NUM_BUCKETS // 2
    d = jnp.maximum(dist, 0)
    log_ratio = jnp.log(jnp.maximum(d, 1).astype(F32) / max_exact) / math.log(MAX_DISTANCE / max_exact)
    large = max_exact + (log_ratio * (NUM_BUCKETS - max_exact)).astype(I32)
    large = jnp.clip(large, max_exact, NUM_BUCKETS - 1)
    return jnp.where(d < max_exact, d, large)


def _bias_lookup(rel_bias, dist):
    onehot = (_t5_bucket(dist)[..., None] == jnp.arange(NUM_BUCKETS, dtype=I32)).astype(F32)
    out = jnp.einsum('...k,kh->...h', onehot, rel_bias.astype(F32), precision=lax.Precision.HIGHEST)
    return jnp.moveaxis(out, -1, 0) * LOG2_E


def _dilated_bias_table(rel_bias, dil):
    qi = jnp.arange(DIL_BACK, dtype=I32)[:, None]
    kj = jnp.arange(2 * DIL_BACK, dtype=I32)[None, :]
    delta = qi + DIL_BACK - kj
    band = (delta >= 0) & (delta <= DIL_BACK)
    bias = _bias_lookup(rel_bias, delta * dil)
    return jnp.where(band[None], bias, NEG_INF)


def _moba_bias_table(rel_bias, n_blk):
    pos = jnp.arange(MOBA_BLOCK, dtype=I32)
    diff = pos[:, None] - pos[None, :]
    blocks_back = jnp.arange(n_blk - 1, -1, -1, dtype=I32)
    dist = blocks_back[None, :, None] * MOBA_BLOCK + diff[:, None, :]
    bias = jnp.where((dist >= 0)[None], _bias_lookup(rel_bias, dist), NEG_INF)
    return bias.reshape(N_HEADS, MOBA_BLOCK, n_blk * MOBA_BLOCK)


def _proj_kernel(*refs, dil, n_split):
    x_refs, w_ref, o_ref = refs[:-2], refs[-2], refs[-1]
    tm = x_refs[0].shape[0]
    rows = tm // dil
    if dil == 1:
        xb = jnp.concatenate([x_ref[...] for x_ref in x_refs], axis=1).astype(BF16)
    else:
        xb = jnp.concatenate(
            [jnp.concatenate([x_ref[pl.ds(r, rows, stride=dil), :] for x_ref in x_refs], axis=1)
             for r in range(dil)], axis=0).astype(BF16)
    width = w_ref.shape[1] // n_split
    for c in range(n_split):
        y = _dot(xb, w_ref[:, c * width:(c + 1) * width]).astype(BF16)
        for r in range(dil):
            o_ref[r, :, c * width:(c + 1) * width] = y[r * rows:(r + 1) * rows]


def _project_dilated(x, w, dil):
    bsz, seq, dm = x.shape
    n = w.shape[1]
    tm = TOKEN_TILE
    return pl.pallas_call(
        functools.partial(_proj_kernel, dil=dil, n_split=3),
        grid=(bsz, seq // tm),
        in_specs=[pl.BlockSpec((None, tm, LANES), lambda b, j, c=c: (b, j, c)) for c in range(dm // LANES)]
        + [_resident((dm, n))],
        out_specs=pl.BlockSpec((None, dil, tm // dil, n), lambda b, j: (b, 0, j, 0)),
        out_shape=jax.ShapeDtypeStruct((bsz, dil, seq // dil, n), BF16),
        compiler_params=_params(("parallel", "parallel")),
        name=f"proj_dil{dil}",
    )(*([x] * (dm // LANES)), w)


def _proj_kmean_kernel(x_ref, w_ref, o_ref, km_ref):
    tm = x_ref.shape[0]
    per_step = tm // MOBA_BLOCK
    j = pl.program_id(1)
    xb = x_ref[...].astype(BF16)
    width = ATTN_WIDTH
    for c in range(3):
        y = _dot(xb, w_ref[:, c * width:(c + 1) * width])
        o_ref[:, c * width:(c + 1) * width] = y.astype(BF16)
        if c == 1:
            for jj in range(per_step):
                km_ref[pl.ds(j * per_step + jj, 1), :] = jnp.mean(
                    y[jj * MOBA_BLOCK:(jj + 1) * MOBA_BLOCK], axis=0, keepdims=True)


def _project_moba(x, w):
    bsz, seq, dm = x.shape
    n = w.shape[1]
    tm = TOKEN_TILE
    n_blk = seq // MOBA_BLOCK
    return pl.pallas_call(
        _proj_kmean_kernel,
        grid=(bsz, seq // tm),
        in_specs=[pl.BlockSpec((None, tm, dm), lambda b, j: (b, j, 0)),
                  _resident((dm, n))],
        out_specs=[pl.BlockSpec((None, tm, n), lambda b, j: (b, j, 0)),
                   pl.BlockSpec((None, n_blk, ATTN_WIDTH), lambda b, j: (b, 0, 0))],
        out_shape=[jax.ShapeDtypeStruct((bsz, seq, n), BF16),
                   jax.ShapeDtypeStruct((bsz, n_blk, ATTN_WIDTH), F32)],
        compiler_params=_params(("parallel", "arbitrary")),
        name="proj_moba",
    )(x, w)


def _two_head_partials(units, is_a):
    scores = []
    for q, kw, _, bias_a, bias_b in units:
        zero = jnp.zeros_like(q)
        scores.append((_dot_nt(jnp.where(is_a, q, zero), kw) + bias_a,
                       _dot_nt(jnp.where(is_a, zero, q), kw) + bias_b))
    stats = []
    for pair in scores:
        row = []
        for s in pair:
            m = jnp.max(s, axis=-1, keepdims=True)
            row.append((m, jnp.exp2(s - m).astype(BF16)))
        stats.append(row)
    out = []
    for (_, _, vw, _, _), ((m_a, p_a), (m_b, p_b)) in zip(units, stats):
        pv_a, pv_b = _two_head_pv(p_a, p_b, vw)
        out.append((jnp.where(is_a, m_a, m_b), pltpu.roll(jnp.where(is_a, pv_b, pv_a), HEAD_DIM, axis=1),
                    jnp.where(is_a, pv_a, pv_b)))
    return out


def _two_head_pv(p_a, p_b, vw):
    own_a = lax.broadcasted_iota(I32, vw.shape, 1) < HEAD_DIM
    one = jnp.ones_like(vw)
    return _dot(p_a, jnp.where(own_a, vw, one)), _dot(p_b, jnp.where(own_a, one, vw))


def _dilated_attn_kernel(q0, k0, v0, q1, k1, v1, q2, k2, v2, b0, b1, b2, o_ref, m_s, l_s, a_s, tmp_s):
    nb = DIL_BACK
    is_a = lax.broadcasted_iota(I32, (nb, LANES), 1) < HEAD_DIM
    n_res2, n_res1 = q2.shape[0], q1.shape[0]
    n_blk1 = q1.shape[1] // nb
    n_blk0 = q0.shape[0] // nb

    def merge(rows, m, l, o):
        m_old, l_old, a_old = m_s[rows, :], l_s[rows, :], a_s[rows, :]
        m_new = jnp.maximum(m_old, m)
        w_old, w_new = jnp.exp2(m_old - m_new), jnp.exp2(m - m_new)
        return m_new, w_old * l_old + w_new * l, w_old * a_old + w_new * o

    unroll = DIL_UNROLL

    pitch = tmp_s.shape[1] // n_res2

    def group2(i, carry):
        res = [i * unroll + u for u in range(unroll)]
        parts = _two_head_partials([(q2[r], k2[r], v2[r], b2[0], b2[1]) for r in res], is_a)
        for r, part in zip(res, parts):
            rows = pl.ds(pl.multiple_of(r * pitch, 8), nb)
            for c in range(3):
                tmp_s[c, rows, :] = part[c]
        return carry

    lax.fori_loop(0, n_res2 // unroll, group2, 0)

    def to_token_major(s, carry):
        dst = pl.ds(pl.multiple_of(s * n_res2, n_res2), n_res2)
        for c, ref in enumerate((m_s, l_s, a_s)):
            ref[dst, :] = tmp_s[c, pl.ds(s, n_res2, stride=pitch), :]
        return carry

    lax.fori_loop(0, nb, to_token_major, 0, unroll=8)

    res_per_body = max(1, unroll // n_blk1)

    def group1(i, carry):
        units, where = [], []
        for rr in range(res_per_body):
            r = i * res_per_body + rr
            units.append((q1[r, 0:nb, :], k1[r, 0:nb, :], v1[r, 0:nb, :], b1[0, :, nb:], b1[1, :, nb:]))
            where.append((r, 0))
            for n in range(1, n_blk1):
                keys = slice((n - 1) * nb, (n + 1) * nb)
                units.append((q1[r, n * nb:(n + 1) * nb, :], k1[r, keys, :], v1[r, keys, :], b1[0], b1[1]))
                where.append((r, n))
        for (r, n), (m, l, o) in zip(where, _two_head_partials(units, is_a)):
            rows = pl.ds(n * nb * n_res1 + r, nb, stride=n_res1)
            m, l, o = merge(rows, m, l, o)
            m_s[rows, :] = m
            l_s[rows, :] = l
            a_s[rows, :] = o
        return carry

    lax.fori_loop(0, n_res1 // res_per_body, group1, 0)

    def finish(rows, m, l, o):
        _, l, o = merge(rows, m, l, o)
        o_ref[rows, :] = (o / l).astype(o_ref.dtype)

    def unit0(n):
        aligned = (lambda v: v) if isinstance(n, int) else (lambda v: pl.multiple_of(v, nb))
        q_rows = pl.ds(aligned(n * nb), nb)
        k_rows = pl.ds(aligned((n - 1) * nb), 2 * nb)
        return q_rows, (q0[q_rows, :], k0[k_rows, :], v0[k_rows, :], b0[0], b0[1])

    def blocks0(units):
        for (q_rows, _), (m, l, o) in zip(units, _two_head_partials([u for _, u in units], is_a)):
            finish(q_rows, m, l, o)

    first = (pl.ds(0, nb), (q0[0:nb, :], k0[0:nb, :], v0[0:nb, :], b0[0, :, nb:], b0[1, :, nb:]))
    blocks0([first] + [unit0(n) for n in range(1, unroll)])

    def group0(i, carry):
        blocks0([unit0(i * unroll + u) for u in range(unroll)])
        return carry

    lax.fori_loop(1, n_blk0 // unroll, group0, 0)


def _dilated_attention(qkv, tables):
    bsz = qkv[0].shape[0]
    seq = qkv[0].shape[1] * qkv[0].shape[2]
    nt = N_HEAD_TILES
    in_specs, args = [], []
    for g, (_, dil) in enumerate(DIL_GROUPS):
        sub = seq // dil
        for part in range(3):
            if dil == 1:
                spec = pl.BlockSpec((None, None, sub, LANES), lambda t, b, part=part: (b, 0, 0, part * nt + t))
            else:
                spec = pl.BlockSpec((None, dil, sub, LANES), lambda t, b, part=part: (b, 0, 0, part * nt + t))
            in_specs.append(spec)
            args.append(qkv[g])
    for g in range(3):
        width = tables[g].shape[2]
        in_specs.append(pl.BlockSpec((HEADS_PER_TILE, DIL_BACK, width), lambda t, b: (t, 0, 0)))
        args.append(tables[g])
    return pl.pallas_call(
        _dilated_attn_kernel,
        grid=(nt, bsz),
        in_specs=in_specs,
        out_specs=pl.BlockSpec((None, seq, LANES), lambda t, b: (b, 0, t)),
        out_shape=jax.ShapeDtypeStruct((bsz, seq, ATTN_WIDTH), BF16),
        scratch_shapes=[pltpu.VMEM((seq, LANES), F32)] * 3
        + [pltpu.VMEM((3, DIL_GROUPS[2][1] * (DIL_BACK + 8), LANES), F32)],
        compiler_params=_params(("parallel", "parallel")),
        name="dilated_attn",
    )(*args)


def _moba_kernel(q_ref, k_ref, v_ref, km_ref, bias_ref, o_ref, kaug_s, nsel_s):
    seq = q_ref.shape[0]
    blk = MOBA_BLOCK
    n_blk = seq // blk
    is_a_row = lax.broadcasted_iota(I32, (n_blk, LANES), 1) < HEAD_DIM
    is_a = lax.broadcasted_iota(I32, (blk, LANES), 1) < HEAD_DIM

    kaug_s[:, :LANES] = k_ref[...]
    key_blk = lax.broadcasted_iota(I32, (seq, LANES), 0) // blk
    col = lax.broadcasted_iota(I32, (seq, LANES), 1)
    kaug_s[:, LANES:] = jnp.where(col == key_blk, NEG_INF, 0.0).astype(BF16)

    q_all = q_ref[...]
    km = km_ref[...]
    q_blk = lax.broadcasted_iota(I32, (n_blk, seq), 1) // blk
    m_idx = lax.broadcasted_iota(I32, (n_blk, seq), 0)
    past = m_idx < q_blk
    n_sel = min(MOBA_TOPK, n_blk - 1)
    pad_rows = jnp.zeros((LANES - n_blk, LANES), F32)
    for hd in range(HEADS_PER_TILE):
        kmh = jnp.where(is_a_row if hd == 0 else jnp.logical_not(is_a_row), km, 0.0)
        gate = sum(_dot_nt(piece, q_all) for piece in _split3(kmh))
        gate = jnp.where(past, gate, NEG_INF)
        rank = jnp.zeros((n_blk, seq), F32)
        for mp in range(n_blk - 1):
            row = gate[mp:mp + 1, :]
            beats = jnp.where(row > gate, 1.0, jnp.where(row == gate, jnp.where(mp < m_idx, 1.0, 0.0), 0.0))
            rank = rank + jnp.where(mp < q_blk, beats, 0.0)
        unselected = jnp.where(past, jnp.where(rank < n_sel, 0.0, 1.0), 1.0)
        unselected = jnp.where(m_idx == q_blk, 0.0, unselected)
        for c in range(seq // LANES):
            tile = jnp.concatenate([unselected[:, c * LANES:(c + 1) * LANES], pad_rows], axis=0)
            nsel_s[hd, c * LANES:(c + 1) * LANES, :] = tile.T.astype(BF16)

    def block_scores(n):
        rows = slice(n * blk, (n + 1) * blk)
        qn = q_ref[rows, :]
        zero = jnp.zeros_like(qn)
        scores = []
        for hd in range(HEADS_PER_TILE):
            qh = jnp.where(is_a, qn, zero) if hd == 0 else jnp.where(is_a, zero, qn)
            qa = jnp.concatenate([qh, nsel_s[hd, rows, :]], axis=1)
            scores.append(_dot_nt(qa, kaug_s[0:(n + 1) * blk, :]) + bias_ref[hd, :, (n_blk - 1 - n) * blk:])
        return scores

    def block_finish(n, scores):
        probs = [jnp.exp2(s - jnp.max(s, axis=-1, keepdims=True)).astype(BF16) for s in scores]
        pv_a, pv_b = _two_head_pv(probs[0], probs[1], v_ref[0:(n + 1) * blk, :])
        denom = pltpu.roll(jnp.where(is_a, pv_b, pv_a), HEAD_DIM, axis=1)
        o_ref[n * blk:(n + 1) * blk, :] = (jnp.where(is_a, pv_a, pv_b) / denom).astype(o_ref.dtype)

    for n in range(n_blk):
        block_finish(n, block_scores(n))


def _moba_attention(qkv, kmean, table):
    bsz, seq, _ = qkv.shape
    nt = N_HEAD_TILES
    n_blk = seq // MOBA_BLOCK
    return pl.pallas_call(
        _moba_kernel,
        grid=(nt, bsz),
        in_specs=[pl.BlockSpec((None, seq, LANES), lambda t, b: (b, 0, t)),
                  pl.BlockSpec((None, seq, LANES), lambda t, b: (b, 0, nt + t)),
                  pl.BlockSpec((None, seq, LANES), lambda t, b: (b, 0, 2 * nt + t)),
                  pl.BlockSpec((None, n_blk, LANES), lambda t, b: (b, 0, t)),
                  pl.BlockSpec((HEADS_PER_TILE, MOBA_BLOCK, seq), lambda t, b: (t, 0, 0))],
        out_specs=pl.BlockSpec((None, seq, LANES), lambda t, b: (b, 0, t)),
        out_shape=jax.ShapeDtypeStruct((bsz, seq, ATTN_WIDTH), BF16),
        scratch_shapes=[pltpu.VMEM((seq, 2 * LANES), BF16),
                        pltpu.VMEM((HEADS_PER_TILE, seq, LANES), BF16)],
        compiler_params=_params(("parallel", "parallel")),
        name="moba_attn",
    )(qkv, qkv, qkv, kmean, table)


def _attn_out_ffn_kernel(x_ref, a_ref, wo_ref, wg_ref, wu_ref, wd_ref, ln_ref, o_ref, *, n_split, n_part):
    rows = x_ref.shape[0] // n_part
    parts = [slice(p * rows, (p + 1) * rows) for p in range(n_part)]
    width = wg_ref.shape[1] // n_split
    proj = [_dot(a_ref[r, :], wo_ref[...]) for r in parts]
    for r, t in zip(parts, proj):
        h = _layernorm(ALPHA * x_ref[r, :] + t, ln_ref[0:1, :], ln_ref[1:2, :])
        hb = h.astype(BF16)
        y = None
        for c in range(n_split):
            cols = slice(c * width, (c + 1) * width)
            g = _dot(hb, wg_ref[:, cols])
            u = _dot(hb, wu_ref[:, cols])
            part = _dot((g * jax.nn.sigmoid(g) * u).astype(BF16), wd_ref[cols, :])
            y = part if y is None else y + part
        o_ref[r, :] = _layernorm(ALPHA * h + y, ln_ref[2:3, :], ln_ref[3:4, :])


def _attn_out_ffn(x, attn, w_out, w_gate, w_up, w_down, ln):
    t, dm = x.shape
    ff = w_gate.shape[1]
    tm = FFN_TILE
    n_split = ff // FFN_SPLIT
    return pl.pallas_call(
        functools.partial(_attn_out_ffn_kernel, n_split=n_split, n_part=2),
        grid=(t // tm,),
        in_specs=[pl.BlockSpec((tm, dm), lambda i: (i, 0)),
                  pl.BlockSpec((tm, attn.shape[1]), lambda i: (i, 0)),
                  _resident(w_out.shape), _resident(w_gate.shape), _resident(w_up.shape),
                  _resident(w_down.shape), _resident(ln.shape)],
        out_specs=pl.BlockSpec((tm, dm), lambda i: (i, 0)),
        out_shape=jax.ShapeDtypeStruct((t, dm), F32),
        compiler_params=_params(("parallel",)),
        name="attn_out_ffn",
    )(x, attn, w_out, w_gate, w_up, w_down, ln)


def _route(h, wr_ref, tri_ref, cnt_s, rec_ref, cnt_ref):
    tm = h.shape[0]
    h1, h2, _ = _split3(h)
    logits = _dot_nt(wr_ref[0], h1) + _dot_nt(wr_ref[1], h1) + _dot_nt(wr_ref[0], h2)
    e_idx = lax.broadcasted_iota(I32, (N_EXPERTS, tm), 0).astype(F32)
    top1 = jnp.max(logits, axis=0, keepdims=True)
    idx1 = jnp.min(jnp.where(logits == top1, e_idx, float(N_EXPERTS)), axis=0, keepdims=True)
    rest = jnp.where(e_idx == idx1, -jnp.inf, logits)
    top2 = jnp.max(rest, axis=0, keepdims=True)
    idx2 = jnp.min(jnp.where(rest == top2, e_idx, float(N_EXPERTS)), axis=0, keepdims=True)
    z = jnp.exp(top2 - top1)
    gate1 = 1.0 / (1.0 + z)
    gate2 = z / (1.0 + z)
    hot1 = jnp.where(e_idx == idx1, 1.0, 0.0)
    hot2 = jnp.where(e_idx == idx2, 1.0, 0.0)
    hot = hot1 + hot2
    before = _dot(hot.astype(BF16), tri_ref[...]) + cnt_s[:, 0:1]
    pos1 = jnp.sum(hot1 * before, axis=0, keepdims=True)
    pos2 = jnp.sum(hot2 * before, axis=0, keepdims=True)
    cnt_s[...] = cnt_s[...] + jnp.sum(hot, axis=1, keepdims=True)
    cnt_ref[...] = cnt_s[...]
    rec_ref[...] = jnp.concatenate([gate1, gate2, idx1, idx2, pos1, pos2, jnp.zeros((2, tm), F32)], axis=0)


def _conv_mixer_kernel(x_ref, wi_ref, cw_ref, wo_ref, ln_ref, wr_ref, tri_ref,
                       o_ref, rec_ref, cnt_ref, y_s, cnt_s):
    tm, dm = x_ref.shape
    first = (pl.program_id(0) == 0) & (pl.program_id(1) == 0)

    @pl.when(first)
    def _():
        cnt_s[...] = jnp.zeros_like(cnt_s)

    @pl.when(pl.program_id(1) == 0)
    def _():
        y_s[0:8, :] = jnp.zeros((8, dm), F32)

    x = x_ref[...]
    xb = x.astype(BF16)
    gate_c = _dot(xb, wi_ref[:, dm:2 * dm])
    x_in = _dot(xb, wi_ref[:, 2 * dm:])
    y_s[8:, :] = gate_c * x_in
    z = cw_ref[0:1, :] * y_s[6:6 + tm, :] + cw_ref[1:2, :] * y_s[7:7 + tm, :] + cw_ref[2:3, :] * y_s[8:, :]
    y_s[0:8, :] = y_s[tm:, :]
    gate_b = _dot(xb, wi_ref[:, :dm])
    mixed = _dot((gate_b * z).astype(BF16), wo_ref[...])
    h = _layernorm(ALPHA * x + mixed, ln_ref[0:1, :], ln_ref[1:2, :])
    o_ref[...] = h
    _route(h, wr_ref, tri_ref, cnt_s, rec_ref, cnt_ref)


def _conv_mixer(x, w_in, conv_w, w_out, ln, wr, tri):
    bsz, seq, dm = x.shape
    tm = TOKEN_TILE
    nj = seq // tm
    return pl.pallas_call(
        _conv_mixer_kernel,
        grid=(bsz, nj),
        in_specs=[pl.BlockSpec((None, tm, dm), lambda b, j: (b, j, 0)),
                  _resident(w_in.shape), _resident(conv_w.shape), _resident(w_out.shape),
                  _resident(ln.shape), _resident(wr.shape), _resident(tri.shape)],
        out_specs=[pl.BlockSpec((None, tm, dm), lambda b, j: (b, j, 0)),
                   pl.BlockSpec((8, tm), lambda b, j: (0, b * nj + j)),
                   pl.BlockSpec((N_EXPERTS, LANES), lambda b, j: (0, 0))],
        out_shape=[jax.ShapeDtypeStruct((bsz, seq, dm), F32),
                   jax.ShapeDtypeStruct((8, bsz * seq), F32),
                   jax.ShapeDtypeStruct((N_EXPERTS, LANES), F32)],
        scratch_shapes=[pltpu.VMEM((tm + 8, dm), F32), pltpu.VMEM((N_EXPERTS, LANES), F32)],
        compiler_params=_params(("arbitrary", "arbitrary")),
        name="conv_mixer",
    )(x, w_in, conv_w, w_out, ln, wr, tri)


def _gmlp_mixer_kernel(x_ref, wi_ref, dln_ref, ws_ref, bs_ref, wo_ref, ln_ref, wr_ref, tri_ref,
                       o_ref, rec_ref, cnt_ref, mix_s, cnt_s):
    tm, dm = x_ref.shape
    ch = GMLP_CHUNK
    gw = dm // GMLP_GROUPS

    @pl.when(pl.program_id(0) == 0)
    def _():
        cnt_s[...] = jnp.zeros_like(cnt_s)

    x = x_ref[...]
    xb = x.astype(BF16)
    u = _gelu(_dot(xb, wi_ref[:, :dm]))
    v = _gelu(_dot(xb, wi_ref[:, dm:]))
    vb = _layernorm(v, dln_ref[0:1, :], dln_ref[1:2, :]).astype(BF16)
    lower = lax.broadcasted_iota(I32, (ch, ch), 0) >= lax.broadcasted_iota(I32, (ch, ch), 1)
    for g in range(GMLP_GROUPS):
        ws = jnp.where(lower, ws_ref[g], 0.0).astype(BF16)
        for c in range(tm // ch):
            blk = _dot(ws, vb[c * ch:(c + 1) * ch, g * gw:(g + 1) * gw]) + bs_ref[:, g:g + 1]
            mix_s[c * ch:(c + 1) * ch, g * gw:(g + 1) * gw] = blk
    mixed = _dot((u * mix_s[...]).astype(BF16), wo_ref[...])
    h = _layernorm(ALPHA * x + mixed, ln_ref[0:1, :], ln_ref[1:2, :])
    o_ref[...] = h
    _route(h, wr_ref, tri_ref, cnt_s, rec_ref, cnt_ref)


def _gmlp_mixer(x, w_in, dln, w_s, b_s_t, w_out, ln, wr, tri):
    t, dm = x.shape
    tm = TOKEN_TILE
    return pl.pallas_call(
        _gmlp_mixer_kernel,
        grid=(t // tm,),
        in_specs=[pl.BlockSpec((tm, dm), lambda i: (i, 0)),
                  _resident(w_in.shape), _resident(dln.shape), _resident(w_s.shape),
                  _resident(b_s_t.shape), _resident(w_out.shape), _resident(ln.shape),
                  _resident(wr.shape), _resident(tri.shape)],
        out_specs=[pl.BlockSpec((tm, dm), lambda i: (i, 0)),
                   pl.BlockSpec((8, tm), lambda i: (0, i)),
                   pl.BlockSpec((N_EXPERTS, LANES), lambda i: (0, 0))],
        out_shape=[jax.ShapeDtypeStruct((t, dm), F32),
                   jax.ShapeDtypeStruct((8, t), F32),
                   jax.ShapeDtypeStruct((N_EXPERTS, LANES), F32)],
        scratch_shapes=[pltpu.VMEM((tm, dm), F32), pltpu.VMEM((N_EXPERTS, LANES), F32)],
        compiler_params=_params(("arbitrary",)),
        name="gmlp_mixer",
    )(x, w_in, dln, w_s, b_s_t, w_out, ln, wr, tri)


def _load_slots(slot_hbm, slot_s, sem, i):
    copies = [pltpu.make_async_copy(slot_hbm.at[c, i], slot_s[c], sem.at[c]) for c in range(2)]
    for cp in copies:
        cp.start()
    for cp in copies:
        cp.wait()


def _start_row_copies(n_rows, row_copy, slot_s):
    def body(j, carry):
        base = pl.multiple_of(j * 8, 8)
        for u in range(8):
            for c in range(2):
                row_copy(base + u, c, slot_s[c][base + u]).start(priority=(u + c) % 2)
        return carry

    lax.fori_loop(0, n_rows // 8, body, 0)


def _moe_dispatch_kernel(fs_ref, fc_ref, slot_hbm, h_ref, x_hbm, slot0_s, slot1_s, zero_s, idx_sem, row_sem):
    i = pl.program_id(0)
    tm = h_ref.shape[0]
    slot_s = (slot0_s, slot1_s)
    _load_slots(slot_hbm, slot_s, idx_sem, i)

    def row_copy(r, c, slot):
        return pltpu.make_async_copy(h_ref.at[pl.ds(r, 1), :], x_hbm.at[pl.ds(slot, 1), :], row_sem.at[c])

    _start_row_copies(tm, row_copy, slot_s)

    @pl.when(i == 0)
    def _():
        zero_s[...] = jnp.zeros_like(zero_s)

        def fill_copy(row):
            return pltpu.make_async_copy(zero_s.at[pl.ds(0, 1), :], x_hbm.at[pl.ds(row, 1), :], row_sem.at[2])

        for e in range(N_EXPERTS):
            def fill(j, carry, e=e):
                fill_copy(fs_ref[e] + j).start()
                return carry

            lax.fori_loop(0, fc_ref[e], fill, 0)

            def drain(j, carry):
                fill_copy(0).wait()
                return carry

            lax.fori_loop(0, fc_ref[e], drain, 0)

    for c in range(2):
        pltpu.make_async_copy(h_ref, x_hbm.at[pl.ds(0, tm), :], row_sem.at[c]).wait()


def _moe_dispatch(h, slots, fill_start, fill_count, n_slots):
    t, dm = h.shape
    tm = slots.shape[2]
    grid_spec = pltpu.PrefetchScalarGridSpec(
        num_scalar_prefetch=2,
        grid=(t // tm,),
        in_specs=[pl.BlockSpec(memory_space=pl.ANY),
                  pl.BlockSpec((tm, dm), lambda i, fs, fc: (i, 0))],
        out_specs=pl.BlockSpec(memory_space=pl.ANY),
        scratch_shapes=[pltpu.SMEM((tm,), I32), pltpu.SMEM((tm,), I32), pltpu.VMEM((8, dm), F32),
                        pltpu.SemaphoreType.DMA((2,)), pltpu.SemaphoreType.DMA((3,))],
    )
    return pl.pallas_call(
        _moe_dispatch_kernel,
        grid_spec=grid_spec,
        out_shape=jax.ShapeDtypeStruct((n_slots, dm), F32),
        compiler_params=_params(("arbitrary",)),
        name="moe_dispatch",
    )(fill_start, fill_count, slots, h)


def _moe_expert_kernel(te_ref, nu_ref, x_ref, wg_ref, wu_ref, wd_ref, y_ref, xb_s):
    i, f = pl.program_id(0), pl.program_id(1)

    @pl.when(i < nu_ref[0])
    def _():
        @pl.when(f == 0)
        def _():
            xb_s[...] = x_ref[...].astype(BF16)
            y_ref[...] = jnp.zeros_like(y_ref)

        xb = xb_s[...]
        g = _dot(xb, wg_ref[...])
        u = _dot(xb, wu_ref[...])
        y_ref[...] += _dot((g * jax.nn.sigmoid(g) * u).astype(BF16), wd_ref[...])


def _moe_experts(x_sorted, tile_expert, n_used, w_gate, w_up, w_down):
    n_slots, dm = x_sorted.shape
    tm = MOE_TILE
    ff = w_gate.shape[2]
    tf = MOE_FF_TILE

    def tile_block(i, f, te, nu):
        return jnp.minimum(i, nu[0] - 1), 0

    def col_block(i, f, te, nu):
        return te[i], 0, jnp.where(i < nu[0], f, ff // tf - 1)

    def row_block(i, f, te, nu):
        return te[i], jnp.where(i < nu[0], f, ff // tf - 1), 0

    grid_spec = pltpu.PrefetchScalarGridSpec(
        num_scalar_prefetch=2,
        grid=(n_slots // tm, ff // tf),
        in_specs=[pl.BlockSpec((tm, dm), tile_block),
                  pl.BlockSpec((None, dm, tf), col_block),
                  pl.BlockSpec((None, dm, tf), col_block),
                  pl.BlockSpec((None, tf, dm), row_block)],
        out_specs=pl.BlockSpec((tm, dm), tile_block),
        scratch_shapes=[pltpu.VMEM((tm, dm), BF16)],
    )
    return pl.pallas_call(
        _moe_expert_kernel,
        grid_spec=grid_spec,
        out_shape=jax.ShapeDtypeStruct((n_slots, dm), F32),
        compiler_params=_params(("arbitrary", "arbitrary")),
        name="moe_experts",
    )(tile_expert, n_used, x_sorted, w_gate, w_up, w_down)


def _moe_combine_kernel(slot_hbm, y_hbm, h_ref, rec_ref, ln_ref, o_ref, slot0_s, slot1_s, y_s, idx_sem, row_sem):
    i = pl.program_id(0)
    tm = h_ref.shape[0]
    slot_s = (slot0_s, slot1_s)
    _load_slots(slot_hbm, slot_s, idx_sem, i)

    def row_copy(r, c, slot):
        return pltpu.make_async_copy(y_hbm.at[pl.ds(slot, 1), :], y_s.at[c, pl.ds(r, 1), :], row_sem.at[c])

    _start_row_copies(tm, row_copy, slot_s)
    for c in range(2):
        pltpu.make_async_copy(y_hbm.at[pl.ds(0, tm), :], y_s.at[c], row_sem.at[c]).wait()

    pad_rows = jnp.zeros((LANES - 8, LANES), F32)
    for c in range(tm // LANES):
        rows = slice(c * LANES, (c + 1) * LANES)
        gates = jnp.concatenate([rec_ref[:, rows], pad_rows], axis=0).T
        y = gates[:, 0:1] * y_s[0, rows, :] + gates[:, 1:2] * y_s[1, rows, :]
        o_ref[rows, :] = _layernorm(ALPHA * h_ref[rows, :] + y, ln_ref[0:1, :], ln_ref[1:2, :])


def _moe_combine(h, y_sorted, slots, rec, ln):
    t, dm = h.shape
    tm = slots.shape[2]
    return pl.pallas_call(
        _moe_combine_kernel,
        grid=(t // tm,),
        in_specs=[pl.BlockSpec(memory_space=pl.ANY),
                  pl.BlockSpec(memory_space=pl.ANY),
                  pl.BlockSpec((tm, dm), lambda i: (i, 0)),
                  pl.BlockSpec((8, tm), lambda i: (0, i)),
                  _resident(ln.shape)],
        out_specs=pl.BlockSpec((tm, dm), lambda i: (i, 0)),
        out_shape=jax.ShapeDtypeStruct((t, dm), F32),
        scratch_shapes=[pltpu.SMEM((tm,), I32), pltpu.SMEM((tm,), I32), pltpu.VMEM((2, tm, dm), F32),
                        pltpu.SemaphoreType.DMA((2,)), pltpu.SemaphoreType.DMA((2,))],
        compiler_params=_params(("arbitrary",)),
        name="moe_combine",
    )(slots, y_sorted, h, rec, ln)


def _moe(h, rec, counts, w_gate, w_up, w_down, ln):
    t, dm = h.shape
    tm = MOE_TILE
    n_tiles = (2 * t) // tm + N_EXPERTS
    cnt = counts[:, 0].astype(I32)
    padded = (cnt + tm - 1) // tm * tm
    pad_end = jnp.cumsum(padded)
    pad_start = pad_end - padded
    experts = jnp.arange(N_EXPERTS, dtype=I32)

    def slot_of(e_row, pos_row):
        e = e_row.astype(I32)
        start = jnp.sum(jnp.where(e[None, :] == experts[:, None], pad_start[:, None], 0), axis=0)
        return start + pos_row.astype(I32)

    slots = jnp.stack([slot_of(rec[2], rec[4]), slot_of(rec[3], rec[5])]).reshape(2, t // MOE_ROW_TILE, MOE_ROW_TILE)
    n_used = pad_end[-1] // tm
    tile_start = jnp.minimum(jnp.arange(n_tiles, dtype=I32), n_used - 1) * tm
    tile_expert = jnp.sum(tile_start[:, None] >= pad_end[None, :], axis=1).astype(I32)
    x_sorted = _moe_dispatch(h, slots, pad_start + cnt, padded - cnt, n_tiles * tm)
    y_sorted = _moe_experts(x_sorted, tile_expert, n_used.astype(I32).reshape(1), w_gate, w_up, w_down)
    return _moe_combine(h, y_sorted, slots, rec, ln)


def kernel(x, rel_bias, ln_gain, ln_bias, a_w_in, a_w_out, b_w_in, b_conv, b_w_out, c_w_in, c_w_out,
           d_w_in, d_ln_gain, d_ln_bias, d_w_s, d_b_s, d_w_out, f_w_gate, f_w_up, f_w_down,
           e_w_router, e_w_gate, e_w_up, e_w_down):
    bsz, seq, dm = x.shape
    t = bsz * seq
    assert seq % (8 * MOBA_BLOCK) == 0 and seq == DIL_GROUPS[-1][0] and ln_gain.shape[0] == DEPTH
    assert seq % TOKEN_TILE == 0 and t % MOE_TILE == 0
    q_scale = HEAD_DIM ** -0.5 * LOG2_E

    def ln_rows(i, *pairs):
        return jnp.concatenate([jnp.stack([ln_gain[i, p], ln_bias[i, p]]) for p in pairs], axis=0)

    def router_weights(w):
        w1, w2, _ = _split3(w.T.astype(F32))
        return jnp.stack([w1, w2])

    tri = jnp.triu(jnp.ones((TOKEN_TILE, TOKEN_TILE), BF16), k=1)

    def scaled_qkv(w):
        scale = jnp.concatenate([jnp.full((ATTN_WIDTH,), q_scale, F32), jnp.ones((2 * ATTN_WIDTH,), F32)])
        return (w * scale).astype(BF16)

    h = x
    group_cols = 3 * ATTN_WIDTH
    qkv = [_project_dilated(h, scaled_qkv(a_w_in[0][:, g * group_cols:(g + 1) * group_cols]), dil)
           for g, (_, dil) in enumerate(DIL_GROUPS)]
    tables = [_dilated_bias_table(rel_bias, dil) for _, dil in DIL_GROUPS]
    tables[2] = tables[2][:, :, DIL_BACK:]
    attn = _dilated_attention(qkv, tables)
    h = _attn_out_ffn(h.reshape(t, dm), attn.reshape(t, ATTN_WIDTH), a_w_out[0].astype(BF16),
                      f_w_gate[0].astype(BF16), f_w_up[0].astype(BF16), f_w_down[0].astype(BF16),
                      ln_rows(0, 0, 1))
    h, rec, counts = _conv_mixer(h.reshape(bsz, seq, dm), b_w_in[0].astype(BF16), b_conv[0],
                                 b_w_out[0].astype(BF16), ln_rows(1, 0), router_weights(e_w_router[0]), tri)
    h = _moe(h.reshape(t, dm), rec, counts, e_w_gate[0].astype(BF16), e_w_up[0].astype(BF16),
             e_w_down[0].astype(BF16), ln_rows(1, 1))
    qkv_c, kmean = _project_moba(h.reshape(bsz, seq, dm), scaled_qkv(c_w_in[0]))
    attn = _moba_attention(qkv_c, kmean, _moba_bias_table(rel_bias, seq // MOBA_BLOCK))
    h = _attn_out_ffn(h, attn.reshape(t, ATTN_WIDTH), c_w_out[0].astype(BF16),
                      f_w_gate[1].astype(BF16), f_w_up[1].astype(BF16), f_w_down[1].astype(BF16),
                      ln_rows(2, 0, 1))
    h, rec, counts = _gmlp_mixer(h, d_w_in[0].astype(BF16), jnp.stack([d_ln_gain[0], d_ln_bias[0]]),
                                 d_w_s[0], d_b_s[0].T, d_w_out[0].astype(BF16), ln_rows(3, 0),
                                 router_weights(e_w_router[1]), tri)
    h = _moe(h, rec, counts, e_w_gate[1].astype(BF16), e_w_up[1].astype(BF16),
             e_w_down[1].astype(BF16), ln_rows(3, 1))
    return h.reshape(bsz, seq, dm)
```

```python
import functools
import math

import jax
import jax.numpy as jnp
from jax import lax
from jax.experimental import pallas as pl
from jax.experimental.pallas import tpu as pltpu

F32 = jnp.float32
BF16 = jnp.bfloat16
I32 = jnp.int32

N_HEADS = 16
HEAD_DIM = 64
LANES = 128
HEADS_PER_TILE = LANES // HEAD_DIM
N_HEAD_TILES = N_HEADS // HEADS_PER_TILE
ATTN_WIDTH = N_HEADS * HEAD_DIM
NUM_BUCKETS = 32
MAX_DISTANCE = 2048
DIL_GROUPS = ((128, 1), (512, 4), (2048, 16))
DIL_BACK = 128
DIL_UNROLL = 8
MOBA_BLOCK = 256
MOBA_TOPK = 3
GMLP_CHUNK = 128
GMLP_GROUPS = 8
N_EXPERTS = 8
DEPTH = 4
ALPHA = (2 * DEPTH) ** 0.25
LN_EPS = 1e-5
NEG_INF = -1e30
LOG2_E = math.log2(math.e)

TOKEN_TILE = 512
MIXER_TILE = 1024
FFN_TILE = 1024
FFN_SPLIT = 256
MOE_TILE = 512
MOE_FF_TILE = 1792
MOE_ROW_TILE = 1024
VMEM_LIMIT = 56 * 1024 * 1024


def _params(semantics, **kw):
    return pltpu.CompilerParams(dimension_semantics=semantics, vmem_limit_bytes=VMEM_LIMIT, **kw)


def _resident(shape):
    return pl.BlockSpec(shape, lambda *_: (0,) * len(shape), pipeline_mode=pl.Buffered(1))


def _layernorm(z, gain, bias):
    mu = jnp.mean(z, axis=-1, keepdims=True)
    zc = z - mu
    var = jnp.mean(zc * zc, axis=-1, keepdims=True)
    return zc * lax.rsqrt(var + LN_EPS) * gain + bias


def _gelu(x):
    return 0.5 * x * (1.0 + lax.erf(x * math.sqrt(0.5)))


def _dot(a, b):
    return jnp.dot(a, b, preferred_element_type=F32)


def _dot_nt(a, b):
    return lax.dot_general(a, b, (((1,), (1,)), ((), ())), preferred_element_type=F32)


def _split3(x):
    x1 = x.astype(BF16)
    r1 = x - x1.astype(F32)
    x2 = r1.astype(BF16)
    x3 = (r1 - x2.astype(F32)).astype(BF16)
    return x1, x2, x3


def _t5_bucket(dist):
    max_exact = NUM_BUCKETS // 2
    d = jnp.maximum(dist, 0)
    log_ratio = jnp.log(jnp.maximum(d, 1).astype(F32) / max_exact) / math.log(MAX_DISTANCE / max_exact)
    large = max_exact + (log_ratio * (NUM_BUCKETS - max_exact)).astype(I32)
    large = jnp.clip(large, max_exact, NUM_BUCKETS - 1)
    return jnp.where(d < max_exact, d, large)


def _bias_lookup(rel_bias, dist):
    onehot = (_t5_bucket(dist)[..., None] == jnp.arange(NUM_BUCKETS, dtype=I32)).astype(F32)
    out = jnp.einsum('...k,kh->...h', onehot, rel_bias.astype(F32), precision=lax.Precision.HIGHEST)
    return jnp.moveaxis(out, -1, 0) * LOG2_E


def _dilated_bias_table(rel_bias, dil):
    qi = jnp.arange(DIL_BACK, dtype=I32)[:, None]
    kj = jnp.arange(2 * DIL_BACK, dtype=I32)[None, :]
    delta = qi + DIL_BACK - kj
    band = (delta >= 0) & (delta <= DIL_BACK)
    bias = _bias_lookup(rel_bias, delta * dil)
    return jnp.where(band[None], bias, NEG_INF)


def _moba_bias_table(rel_bias, n_blk):
    pos = jnp.arange(MOBA_BLOCK, dtype=I32)
    diff = pos[:, None] - pos[None, :]
    blocks_back = jnp.arange(n_blk - 1, -1, -1, dtype=I32)
    dist = blocks_back[None, :, None] * MOBA_BLOCK + diff[:, None, :]
    bias = jnp.where((dist >= 0)[None], _bias_lookup(rel_bias, dist), NEG_INF)
    return bias.reshape(N_HEADS, MOBA_BLOCK, n_blk * MOBA_BLOCK)


def _proj_kernel(*refs, dil, n_split):
    x_refs, w_ref, o_ref = refs[:-2], refs[-2], refs[-1]
    tm = x_refs[0].shape[0]
    rows = tm // dil
    if dil == 1:
        xb = jnp.concatenate([x_ref[...] for x_ref in x_refs], axis=1).astype(BF16)
    else:
        xb = jnp.concatenate(
            [jnp.concatenate([x_ref[pl.ds(r, rows, stride=dil), :] for x_ref in x_refs], axis=1)
             for r in range(dil)], axis=0).astype(BF16)
    width = w_ref.shape[1] // n_split
    for c in range(n_split):
        y = _dot(xb, w_ref[:, c * width:(c + 1) * width]).astype(BF16)
        for r in range(dil):
            o_ref[r, :, c * width:(c + 1) * width] = y[r * rows:(r + 1) * rows]


def _project_dilated(x, w, dil):
    bsz, seq, dm = x.shape
    n = w.shape[1]
    tm = TOKEN_TILE
    return pl.pallas_call(
        functools.partial(_proj_kernel, dil=dil, n_split=3),
        grid=(bsz, seq // tm),
        in_specs=[pl.BlockSpec((None, tm, LANES), lambda b, j, c=c: (b, j, c)) for c in range(dm // LANES)]
        + [_resident((dm, n))],
        out_specs=pl.BlockSpec((None, dil, tm // dil, n), lambda b, j: (b, 0, j, 0)),
        out_shape=jax.ShapeDtypeStruct((bsz, dil, seq // dil, n), BF16),
        compiler_params=_params(("parallel", "parallel")),
        name=f"proj_dil{dil}",
    )(*([x] * (dm // LANES)), w)


def _proj_kmean_kernel(x_ref, w_ref, o_ref, km_ref):
    tm = x_ref.shape[0]
    per_step = tm // MOBA_BLOCK
    j = pl.program_id(1)
    xb = x_ref[...].astype(BF16)
    width = ATTN_WIDTH
    for c in range(3):
        y = _dot(xb, w_ref[:, c * width:(c + 1) * width])
        o_ref[:, c * width:(c + 1) * width] = y.astype(BF16)
        if c == 1:
            for jj in range(per_step):
                km_ref[pl.ds(j * per_step + jj, 1), :] = jnp.mean(
                    y[jj * MOBA_BLOCK:(jj + 1) * MOBA_BLOCK], axis=0, keepdims=True)


def _project_moba(x, w):
    bsz, seq, dm = x.shape
    n = w.shape[1]
    tm = TOKEN_TILE
    n_blk = seq // MOBA_BLOCK
    return pl.pallas_call(
        _proj_kmean_kernel,
        grid=(bsz, seq // tm),
        in_specs=[pl.BlockSpec((None, tm, dm), lambda b, j: (b, j, 0)),
                  _resident((dm, n))],
        out_specs=[pl.BlockSpec((None, tm, n), lambda b, j: (b, j, 0)),
                   pl.BlockSpec((None, n_blk, ATTN_WIDTH), lambda b, j: (b, 0, 0))],
        out_shape=[jax.ShapeDtypeStruct((bsz, seq, n), BF16),
                   jax.ShapeDtypeStruct((bsz, n_blk, ATTN_WIDTH), F32)],
        compiler_params=_params(("parallel", "arbitrary")),
        name="proj_moba",
    )(x, w)


def _two_head_partials(units, is_a):
    scores = []
    for q, kw, _, bias_a, bias_b in units:
        zero = jnp.zeros_like(q)
        scores.append((_dot_nt(jnp.where(is_a, q, zero), kw) + bias_a,
                       _dot_nt(jnp.where(is_a, zero, q), kw) + bias_b))
    stats = []
    for pair in scores:
        row = []
        for s in pair:
            m = jnp.max(s, axis=-1, keepdims=True)
            row.append((m, jnp.exp2(s - m).astype(BF16)))
        stats.append(row)
    out = []
    for (_, _, vw, _, _), ((m_a, p_a), (m_b, p_b)) in zip(units, stats):
        pv_a, pv_b = _two_head_pv(p_a, p_b, vw)
        out.append((jnp.where(is_a, m_a, m_b), pltpu.roll(jnp.where(is_a, pv_b, pv_a), HEAD_DIM, axis=1),
                    jnp.where(is_a, pv_a, pv_b)))
    return out


def _two_head_pv(p_a, p_b, vw):
    own_a = lax.broadcasted_iota(I32, vw.shape, 1) < HEAD_DIM
    one = jnp.ones_like(vw)
    return _dot(p_a, jnp.where(own_a, vw, one)), _dot(p_b, jnp.where(own_a, one, vw))


def _dilated_attn_kernel(q0, k0, v0, q1, k1, v1, q2, k2, v2, b0, b1, b2, o_ref, m_s, l_s, a_s, tmp_s):
    nb = DIL_BACK
    is_a = lax.broadcasted_iota(I32, (nb, LANES), 1) < HEAD_DIM
    n_res2, n_res1 = q2.shape[0], q1.shape[0]
    n_blk1 = q1.shape[1] // nb
    n_blk0 = q0.shape[0] // nb

    def merge(rows, m, l, o):
        m_old, l_old, a_old = m_s[rows, :], l_s[rows, :], a_s[rows, :]
        m_new = jnp.maximum(m_old, m)
        w_old, w_new = jnp.exp2(m_old - m_new), jnp.exp2(m - m_new)
        return m_new, w_old * l_old + w_new * l, w_old * a_old + w_new * o

    unroll = DIL_UNROLL

    pitch = tmp_s.shape[1] // n_res2

    def group2(i, carry):
        res = [i * unroll + u for u in range(unroll)]
        parts = _two_head_partials([(q2[r], k2[r], v2[r], b2[0], b2[1]) for r in res], is_a)
        for r, part in zip(res, parts):
            rows = pl.ds(pl.multiple_of(r * pitch, 8), nb)
            for c in range(3):
                tmp_s[c, rows, :] = part[c]
        return carry

    lax.fori_loop(0, n_res2 // unroll, group2, 0)

    def to_token_major(s, carry):
        dst = pl.ds(pl.multiple_of(s * n_res2, n_res2), n_res2)
        for c, ref in enumerate((m_s, l_s, a_s)):
            ref[dst, :] = tmp_s[c, pl.ds(s, n_res2, stride=pitch), :]
        return carry

    lax.fori_loop(0, nb, to_token_major, 0, unroll=8)

    res_per_body = max(1, unroll // n_blk1)

    def group1(i, carry):
        units, where = [], []
        for rr in range(res_per_body):
            r = i * res_per_body + rr
            units.append((q1[r, 0:nb, :], k1[r, 0:nb, :], v1[r, 0:nb, :], b1[0, :, nb:], b1[1, :, nb:]))
            where.append((r, 0))
            for n in range(1, n_blk1):
                keys = slice((n - 1) * nb, (n + 1) * nb)
                units.append((q1[r, n * nb:(n + 1) * nb, :], k1[r, keys, :], v1[r, keys, :], b1[0], b1[1]))
                where.append((r, n))
        for (r, n), (m, l, o) in zip(where, _two_head_partials(units, is_a)):
            rows = pl.ds(n * nb * n_res1 + r, nb, stride=n_res1)
            m, l, o = merge(rows, m, l, o)
            m_s[rows, :] = m
            l_s[rows, :] = l
            a_s[rows, :] = o
        return carry

    lax.fori_loop(0, n_res1 // res_per_body, group1, 0)

    def finish(rows, m, l, o):
        _, l, o = merge(rows, m, l, o)
        o_ref[rows, :] = (o / l).astype(o_ref.dtype)

    def unit0(n):
        aligned = (lambda v: v) if isinstance(n, int) else (lambda v: pl.multiple_of(v, nb))
        q_rows = pl.ds(aligned(n * nb), nb)
        k_rows = pl.ds(aligned((n - 1) * nb), 2 * nb)
        return q_rows, (q0[q_rows, :], k0[k_rows, :], v0[k_rows, :], b0[0], b0[1])

    def blocks0(units):
        for (q_rows, _), (m, l, o) in zip(units, _two_head_partials([u for _, u in units], is_a)):
            finish(q_rows, m, l, o)

    first = (pl.ds(0, nb), (q0[0:nb, :], k0[0:nb, :], v0[0:nb, :], b0[0, :, nb:], b0[1, :, nb:]))
    blocks0([first] + [unit0(n) for n in range(1, unroll)])

    def group0(i, carry):
        blocks0([unit0(i * unroll + u) for u in range(unroll)])
        return carry

    lax.fori_loop(1, n_blk0 // unroll, group0, 0)


def _dilated_attention(qkv, tables):
    bsz = qkv[0].shape[0]
    seq = qkv[0].shape[1] * qkv[0].shape[2]
    nt = N_HEAD_TILES
    in_specs, args = [], []
    for g, (_, dil) in enumerate(DIL_GROUPS):
        sub = seq // dil
        for part in range(3):
            if dil == 1:
                spec = pl.BlockSpec((None, None, sub, LANES), lambda t, b, part=part: (b, 0, 0, part * nt + t))
            else:
                spec = pl.BlockSpec((None, dil, sub, LANES), lambda t, b, part=part: (b, 0, 0, part * nt + t))
            in_specs.append(spec)
            args.append(qkv[g])
    for g in range(3):
        width = tables[g].shape[2]
        in_specs.append(pl.BlockSpec((HEADS_PER_TILE, DIL_BACK, width), lambda t, b: (t, 0, 0)))
        args.append(tables[g])
    return pl.pallas_call(
        _dilated_attn_kernel,
        grid=(nt, bsz),
        in_specs=in_specs,
        out_specs=pl.BlockSpec((None, seq, LANES), lambda t, b: (b, 0, t)),
        out_shape=jax.ShapeDtypeStruct((bsz, seq, ATTN_WIDTH), BF16),
        scratch_shapes=[pltpu.VMEM((seq, LANES), F32)] * 3
        + [pltpu.VMEM((3, DIL_GROUPS[2][1] * (DIL_BACK + 8), LANES), F32)],
        compiler_params=_params(("parallel", "parallel")),
        name="dilated_attn",
    )(*args)


def _moba_kernel(q_ref, k_ref, v_ref, km_ref, bias_ref, o_ref, kaug_s, nsel_s):
    seq = q_ref.shape[0]
    blk = MOBA_BLOCK
    n_blk = seq // blk
    is_a_row = lax.broadcasted_iota(I32, (n_blk, LANES), 1) < HEAD_DIM
    is_a = lax.broadcasted_iota(I32, (blk, LANES), 1) < HEAD_DIM

    kaug_s[:, :LANES] = k_ref[...]
    key_blk = lax.broadcasted_iota(I32, (seq, LANES), 0) // blk
    col = lax.broadcasted_iota(I32, (seq, LANES), 1)
    kaug_s[:, LANES:] = jnp.where(col == key_blk, NEG_INF, 0.0).astype(BF16)

    q_all = q_ref[...]
    km = km_ref[...]
    q_blk = lax.broadcasted_iota(I32, (n_blk, seq), 1) // blk
    m_idx = lax.broadcasted_iota(I32, (n_blk, seq), 0)
    past = m_idx < q_blk
    n_sel = min(MOBA_TOPK, n_blk - 1)
    pad_rows = jnp.zeros((LANES - n_blk, LANES), F32)
    for hd in range(HEADS_PER_TILE):
        kmh = jnp.where(is_a_row if hd == 0 else jnp.logical_not(is_a_row), km, 0.0)
        gate = sum(_dot_nt(piece, q_all) for piece in _split3(kmh))
        gate = jnp.where(past, gate, NEG_INF)
        rank = jnp.zeros((n_blk, seq), F32)
        for mp in range(n_blk - 1):
            row = gate[mp:mp + 1, :]
            beats = jnp.where(row > gate, 1.0, jnp.where(row == gate, jnp.where(mp < m_idx, 1.0, 0.0), 0.0))
            rank = rank + jnp.where(mp < q_blk, beats, 0.0)
        unselected = jnp.where(past, jnp.where(rank < n_sel, 0.0, 1.0), 1.0)
        unselected = jnp.where(m_idx == q_blk, 0.0, unselected)
        for c in range(seq // LANES):
            tile = jnp.concatenate([unselected[:, c * LANES:(c + 1) * LANES], pad_rows], axis=0)
            nsel_s[hd, c * LANES:(c + 1) * LANES, :] = tile.T.astype(BF16)

    def block_scores(n):
        rows = slice(n * blk, (n + 1) * blk)
        qn = q_ref[rows, :]
        zero = jnp.zeros_like(qn)
        scores = []
        for hd in range(HEADS_PER_TILE):
            qh = jnp.where(is_a, qn, zero) if hd == 0 else jnp.where(is_a, zero, qn)
            qa = jnp.concatenate([qh, nsel_s[hd, rows, :]], axis=1)
            scores.append(_dot_nt(qa, kaug_s[0:(n + 1) * blk, :]) + bias_ref[hd, :, (n_blk - 1 - n) * blk:])
        return scores

    def block_finish(n, scores):
        probs = [jnp.exp2(s - jnp.max(s, axis=-1, keepdims=True)).astype(BF16) for s in scores]
        pv_a, pv_b = _two_head_pv(probs[0], probs[1], v_ref[0:(n + 1) * blk, :])
        denom = pltpu.roll(jnp.where(is_a, pv_b, pv_a), HEAD_DIM, axis=1)
        o_ref[n * blk:(n + 1) * blk, :] = (jnp.where(is_a, pv_a, pv_b) / denom).astype(o_ref.dtype)

    for n in range(n_blk):
        block_finish(n, block_scores(n))


def _moba_attention(qkv, kmean, table):
    bsz, seq, _ = qkv.shape
    nt = N_HEAD_TILES
    n_blk = seq // MOBA_BLOCK
    return pl.pallas_call(
        _moba_kernel,
        grid=(nt, bsz),
        in_specs=[pl.BlockSpec((None, seq, LANES), lambda t, b: (b, 0, t)),
                  pl.BlockSpec((None, seq, LANES), lambda t, b: (b, 0, nt + t)),
                  pl.BlockSpec((None, seq, LANES), lambda t, b: (b, 0, 2 * nt + t)),
                  pl.BlockSpec((None, n_blk, LANES), lambda t, b: (b, 0, t)),
                  pl.BlockSpec((HEADS_PER_TILE, MOBA_BLOCK, seq), lambda t, b: (t, 0, 0))],
        out_specs=pl.BlockSpec((None, seq, LANES), lambda t, b: (b, 0, t)),
        out_shape=jax.ShapeDtypeStruct((bsz, seq, ATTN_WIDTH), BF16),
        scratch_shapes=[pltpu.VMEM((seq, 2 * LANES), BF16),
                        pltpu.VMEM((HEADS_PER_TILE, seq, LANES), BF16)],
        compiler_params=_params(("parallel", "parallel")),
        name="moba_attn",
    )(qkv, qkv, qkv, kmean, table)


def _attn_out_ffn_kernel(x_ref, a_ref, wo_ref, wg_ref, wu_ref, wd_ref, ln_ref, o_ref, *, n_split, n_part):
    rows = x_ref.shape[0] // n_part
    parts = [slice(p * rows, (p + 1) * rows) for p in range(n_part)]
    width = wg_ref.shape[1] // n_split
    proj = [_dot(a_ref[r, :], wo_ref[...]) for r in parts]
    for r, t in zip(parts, proj):
        h = _layernorm(ALPHA * x_ref[r, :] + t, ln_ref[0:1, :], ln_ref[1:2, :])
        hb = h.astype(BF16)
        y = None
        for c in range(n_split):
            cols = slice(c * width, (c + 1) * width)
            g = _dot(hb, wg_ref[:, cols])
            u = _dot(hb, wu_ref[:, cols])
            part = _dot((g * jax.nn.sigmoid(g) * u).astype(BF16), wd_ref[cols, :])
            y = part if y is None else y + part
        o_ref[r, :] = _layernorm(ALPHA * h + y, ln_ref[2:3, :], ln_ref[3:4, :])


def _attn_out_ffn(x, attn, w_out, w_gate, w_up, w_down, ln):
    t, dm = x.shape
    ff = w_gate.shape[1]
    tm = FFN_TILE
    n_split = ff // FFN_SPLIT
    return pl.pallas_call(
        functools.partial(_attn_out_ffn_kernel, n_split=n_split, n_part=2),
        grid=(t // tm,),
        in_specs=[pl.BlockSpec((tm, dm), lambda i: (i, 0)),
                  pl.BlockSpec((tm, attn.shape[1]), lambda i: (i, 0)),
                  _resident(w_out.shape), _resident(w_gate.shape), _resident(w_up.shape),
                  _resident(w_down.shape), _resident(ln.shape)],
        out_specs=pl.BlockSpec((tm, dm), lambda i: (i, 0)),
        out_shape=jax.ShapeDtypeStruct((t, dm), F32),
        compiler_params=_params(("parallel",)),
        name="attn_out_ffn",
    )(x, attn, w_out, w_gate, w_up, w_down, ln)


def _route(h, wr_ref, tri_ref, cnt_s, rec_ref, cnt_ref, cols):
    tm = h.shape[0]
    h1, h2, _ = _split3(h)
    logits = _dot_nt(wr_ref[0], h1) + _dot_nt(wr_ref[1], h1) + _dot_nt(wr_ref[0], h2)
    e_idx = lax.broadcasted_iota(I32, (N_EXPERTS, tm), 0).astype(F32)
    top1 = jnp.max(logits, axis=0, keepdims=True)
    idx1 = jnp.min(jnp.where(logits == top1, e_idx, float(N_EXPERTS)), axis=0, keepdims=True)
    rest = jnp.where(e_idx == idx1, -jnp.inf, logits)
    top2 = jnp.max(rest, axis=0, keepdims=True)
    idx2 = jnp.min(jnp.where(rest == top2, e_idx, float(N_EXPERTS)), axis=0, keepdims=True)
    z = jnp.exp(top2 - top1)
    gate1 = 1.0 / (1.0 + z)
    gate2 = z / (1.0 + z)
    hot1 = jnp.where(e_idx == idx1, 1.0, 0.0)
    hot2 = jnp.where(e_idx == idx2, 1.0, 0.0)
    hot = hot1 + hot2
    before = _dot(hot.astype(BF16), tri_ref[...]) + cnt_s[:, 0:1]
    pos1 = jnp.sum(hot1 * before, axis=0, keepdims=True)
    pos2 = jnp.sum(hot2 * before, axis=0, keepdims=True)
    cnt_s[...] = cnt_s[...] + jnp.sum(hot, axis=1, keepdims=True)
    cnt_ref[...] = cnt_s[...]
    rec_ref[:, cols] = jnp.concatenate([gate1, gate2, idx1, idx2, pos1, pos2, jnp.zeros((2, tm), F32)], axis=0)


def _conv_mixer_kernel(x_ref, wi_ref, cw_ref, wo_ref, ln_ref, wr_ref, tri_ref,
                       o_ref, rec_ref, cnt_ref, y_s, cnt_s):
    tm, dm = x_ref.shape
    first = (pl.program_id(0) == 0) & (pl.program_id(1) == 0)

    @pl.when(first)
    def _():
        cnt_s[...] = jnp.zeros_like(cnt_s)

    @pl.when(pl.program_id(1) == 0)
    def _():
        y_s[0:8, :] = jnp.zeros((8, dm), F32)

    rows = tri_ref.shape[0]
    parts = [slice(p * rows, (p + 1) * rows) for p in range(tm // rows)]
    xbs = [x_ref[r, :].astype(BF16) for r in parts]
    for r, xb in zip(parts, xbs):
        y_s[8 + r.start:8 + r.stop, :] = _dot(xb, wi_ref[:, dm:2 * dm]) * _dot(xb, wi_ref[:, 2 * dm:])
    gate_bs = [_dot(xb, wi_ref[:, :dm]) for xb in xbs]
    mixed = []
    for r, gate_b in zip(parts, gate_bs):
        z = (cw_ref[0:1, :] * y_s[6 + r.start:6 + r.stop, :] + cw_ref[1:2, :] * y_s[7 + r.start:7 + r.stop, :]
             + cw_ref[2:3, :] * y_s[8 + r.start:8 + r.stop, :])
        mixed.append(_dot((gate_b * z).astype(BF16), wo_ref[...]))
    y_s[0:8, :] = y_s[tm:, :]
    hs = [_layernorm(ALPHA * x_ref[r, :] + m, ln_ref[0:1, :], ln_ref[1:2, :]) for r, m in zip(parts, mixed)]
    for r, h in zip(parts, hs):
        o_ref[r, :] = h
        _route(h, wr_ref, tri_ref, cnt_s, rec_ref, cnt_ref, r)


def _conv_mixer(x, w_in, conv_w, w_out, ln, wr, tri):
    bsz, seq, dm = x.shape
    tm = MIXER_TILE
    nj = seq // tm
    return pl.pallas_call(
        _conv_mixer_kernel,
        grid=(bsz, nj),
        in_specs=[pl.BlockSpec((None, tm, dm), lambda b, j: (b, j, 0)),
                  _resident(w_in.shape), _resident(conv_w.shape), _resident(w_out.shape),
                  _resident(ln.shape), _resident(wr.shape), _resident(tri.shape)],
        out_specs=[pl.BlockSpec((None, tm, dm), lambda b, j: (b, j, 0)),
                   pl.BlockSpec((8, tm), lambda b, j: (0, b * nj + j)),
                   pl.BlockSpec((N_EXPERTS, LANES), lambda b, j: (0, 0))],
        out_shape=[jax.ShapeDtypeStruct((bsz, seq, dm), F32),
                   jax.ShapeDtypeStruct((8, bsz * seq), F32),
                   jax.ShapeDtypeStruct((N_EXPERTS, LANES), F32)],
        scratch_shapes=[pltpu.VMEM((tm + 8, dm), F32), pltpu.VMEM((N_EXPERTS, LANES), F32)],
        compiler_params=_params(("arbitrary", "arbitrary")),
        name="conv_mixer",
    )(x, w_in, conv_w, w_out, ln, wr, tri)


def _gmlp_mixer_kernel(x_ref, wi_ref, dln_ref, ws_ref, bs_ref, wo_ref, ln_ref, wr_ref, tri_ref,
                       o_ref, rec_ref, cnt_ref, mix_s, cnt_s):
    tm, dm = x_ref.shape
    ch = GMLP_CHUNK
    gw = dm // GMLP_GROUPS

    @pl.when(pl.program_id(0) == 0)
    def _():
        cnt_s[...] = jnp.zeros_like(cnt_s)

    rows = tri_ref.shape[0]
    parts = [slice(p * rows, (p + 1) * rows) for p in range(tm // rows)]
    lower = lax.broadcasted_iota(I32, (ch, ch), 0) >= lax.broadcasted_iota(I32, (ch, ch), 1)
    ws = [jnp.where(lower, ws_ref[g], 0.0).astype(BF16) for g in range(GMLP_GROUPS)]
    pre = []
    for r in parts:
        xb = x_ref[r, :].astype(BF16)
        pre.append((_dot(xb, wi_ref[:, :dm]), _dot(xb, wi_ref[:, dm:])))
    mixed = []
    for r, (u_pre, v_pre) in zip(parts, pre):
        vb = _layernorm(_gelu(v_pre), dln_ref[0:1, :], dln_ref[1:2, :]).astype(BF16)
        for g in range(GMLP_GROUPS):
            for c in range(rows // ch):
                blk = _dot(ws[g], vb[c * ch:(c + 1) * ch, g * gw:(g + 1) * gw]) + bs_ref[:, g:g + 1]
                mix_s[r.start + c * ch:r.start + (c + 1) * ch, g * gw:(g + 1) * gw] = blk
        mixed.append(_dot((_gelu(u_pre) * mix_s[r, :]).astype(BF16), wo_ref[...]))
    hs = [_layernorm(ALPHA * x_ref[r, :] + m, ln_ref[0:1, :], ln_ref[1:2, :]) for r, m in zip(parts, mixed)]
    for r, h in zip(parts, hs):
        o_ref[r, :] = h
        _route(h, wr_ref, tri_ref, cnt_s, rec_ref, cnt_ref, r)


def _gmlp_mixer(x, w_in, dln, w_s, b_s_t, w_out, ln, wr, tri):
    t, dm = x.shape
    tm = MIXER_TILE
    return pl.pallas_call(
        _gmlp_mixer_kernel,
        grid=(t // tm,),
        in_specs=[pl.BlockSpec((tm, dm), lambda i: (i, 0)),
                  _resident(w_in.shape), _resident(dln.shape), _resident(w_s.shape),
                  _resident(b_s_t.shape), _resident(w_out.shape), _resident(ln.shape),
                  _resident(wr.shape), _resident(tri.shape)],
        out_specs=[pl.BlockSpec((tm, dm), lambda i: (i, 0)),
                   pl.BlockSpec((8, tm), lambda i: (0, i)),
                   pl.BlockSpec((N_EXPERTS, LANES), lambda i: (0, 0))],
        out_shape=[jax.ShapeDtypeStruct((t, dm), F32),
                   jax.ShapeDtypeStruct((8, t), F32),
                   jax.ShapeDtypeStruct((N_EXPERTS, LANES), F32)],
        scratch_shapes=[pltpu.VMEM((tm, dm), F32), pltpu.VMEM((N_EXPERTS, LANES), F32)],
        compiler_params=_params(("arbitrary",)),
        name="gmlp_mixer",
    )(x, w_in, dln, w_s, b_s_t, w_out, ln, wr, tri)


def _load_slots(slot_hbm, slot_s, sem, i):
    copies = [pltpu.make_async_copy(slot_hbm.at[c, i], slot_s[c], sem.at[c]) for c in range(2)]
    for cp in copies:
        cp.start()
    for cp in copies:
        cp.wait()


def _start_row_copies(n_rows, row_copy, slot_s):
    for r in range(n_rows):
        for c in range(2):
            row_copy(r, c, slot_s[c][r]).start(priority=(r + c) % 2)


def _moe_dispatch_kernel(fs_ref, fc_ref, slot_hbm, h_ref, x_hbm, slot0_s, slot1_s, zero_s, idx_sem, row_sem):
    i = pl.program_id(0)
    tm = h_ref.shape[0]
    slot_s = (slot0_s, slot1_s)
    _load_slots(slot_hbm, slot_s, idx_sem, i)

    def row_copy(r, c, slot):
        return pltpu.make_async_copy(h_ref.at[pl.ds(r, 1), :], x_hbm.at[pl.ds(slot, 1), :], row_sem.at[c])

    _start_row_copies(tm, row_copy, slot_s)

    @pl.when(i == 0)
    def _():
        zero_s[...] = jnp.zeros_like(zero_s)

        def fill_copy(row):
            return pltpu.make_async_copy(zero_s.at[pl.ds(0, 1), :], x_hbm.at[pl.ds(row, 1), :], row_sem.at[2])

        for e in range(N_EXPERTS):
            def fill(j, carry, e=e):
                fill_copy(fs_ref[e] + j).start()
                return carry

            lax.fori_loop(0, fc_ref[e], fill, 0)

            def drain(j, carry):
                fill_copy(0).wait()
                return carry

            lax.fori_loop(0, fc_ref[e], drain, 0)

    for c in range(2):
        pltpu.make_async_copy(h_ref, x_hbm.at[pl.ds(0, tm), :], row_sem.at[c]).wait()


def _moe_dispatch(h, slots, fill_start, fill_count, n_slots):
    t, dm = h.shape
    tm = slots.shape[2]
    grid_spec = pltpu.PrefetchScalarGridSpec(
        num_scalar_prefetch=2,
        grid=(t // tm,),
        in_specs=[pl.BlockSpec(memory_space=pl.ANY),
                  pl.BlockSpec((tm, dm), lambda i, fs, fc: (i, 0))],
        out_specs=pl.BlockSpec(memory_space=pl.ANY),
        scratch_shapes=[pltpu.SMEM((tm,), I32), pltpu.SMEM((tm,), I32), pltpu.VMEM((8, dm), F32),
                        pltpu.SemaphoreType.DMA((2,)), pltpu.SemaphoreType.DMA((3,))],
    )
    return pl.pallas_call(
        _moe_dispatch_kernel,
        grid_spec=grid_spec,
        out_shape=jax.ShapeDtypeStruct((n_slots, dm), F32),
        compiler_params=_params(("arbitrary",)),
        name="moe_dispatch",
    )(fill_start, fill_count, slots, h)


def _moe_expert_kernel(te_ref, nu_ref, x_ref, wg_ref, wu_ref, wd_ref, y_ref, xb_s):
    i, f = pl.program_id(0), pl.program_id(1)

    @pl.when(i < nu_ref[0])
    def _():
        @pl.when(f == 0)
        def _():
            xb_s[...] = x_ref[...].astype(BF16)
            y_ref[...] = jnp.zeros_like(y_ref)

        xb = xb_s[...]
        g = _dot(xb, wg_ref[...])
        u = _dot(xb, wu_ref[...])
        y_ref[...] += _dot((g * jax.nn.sigmoid(g) * u).astype(BF16), wd_ref[...])


def _moe_experts(x_sorted, tile_expert, n_used, w_gate, w_up, w_down):
    n_slots, dm = x_sorted.shape
    tm = MOE_TILE
    ff = w_gate.shape[2]
    tf = MOE_FF_TILE

    def tile_block(i, f, te, nu):
        return jnp.minimum(i, nu[0] - 1), 0

    def col_block(i, f, te, nu):
        return te[i], 0, jnp.where(i < nu[0], f, ff // tf - 1)

    def row_block(i, f, te, nu):
        return te[i], jnp.where(i < nu[0], f, ff // tf - 1), 0

    grid_spec = pltpu.PrefetchScalarGridSpec(
        num_scalar_prefetch=2,
        grid=(n_slots // tm, ff // tf),
        in_specs=[pl.BlockSpec((tm, dm), tile_block),
                  pl.BlockSpec((None, dm, tf), col_block),
                  pl.BlockSpec((None, dm, tf), col_block),
                  pl.BlockSpec((None, tf, dm), row_block)],
        out_specs=pl.BlockSpec((tm, dm), tile_block),
        scratch_shapes=[pltpu.VMEM((tm, dm), BF16)],
    )
    return pl.pallas_call(
        _moe_expert_kernel,
        grid_spec=grid_spec,
        out_shape=jax.ShapeDtypeStruct((n_slots, dm), F32),
        compiler_params=_params(("arbitrary", "arbitrary")),
        name="moe_experts",
    )(tile_expert, n_used, x_sorted, w_gate, w_up, w_down)


def _moe_combine_kernel(slot_hbm, y_hbm, h_ref, rec_ref, ln_ref, o_ref, slot0_s, slot1_s, y_s, idx_sem, row_sem):
    i = pl.program_id(0)
    tm = h_ref.shape[0]
    slot_s = (slot0_s, slot1_s)
    _load_slots(slot_hbm, slot_s, idx_sem, i)

    def row_copy(r, c, slot):
        return pltpu.make_async_copy(y_hbm.at[pl.ds(slot, 1), :], y_s.at[c, pl.ds(r, 1), :], row_sem.at[c])

    _start_row_copies(tm, row_copy, slot_s)
    for c in range(2):
        pltpu.make_async_copy(y_hbm.at[pl.ds(0, tm), :], y_s.at[c], row_sem.at[c]).wait()

    pad_rows = jnp.zeros((LANES - 8, LANES), F32)
    for c in range(tm // LANES):
        rows = slice(c * LANES, (c + 1) * LANES)
        gates = jnp.concatenate([rec_ref[:, rows], pad_rows], axis=0).T
        y = gates[:, 0:1] * y_s[0, rows, :] + gates[:, 1:2] * y_s[1, rows, :]
        o_ref[rows, :] = _layernorm(ALPHA * h_ref[rows, :] + y, ln_ref[0:1, :], ln_ref[1:2, :])


def _moe_combine(h, y_sorted, slots, rec, ln):
    t, dm = h.shape
    tm = slots.shape[2]
    return pl.pallas_call(
        _moe_combine_kernel,
        grid=(t // tm,),
        in_specs=[pl.BlockSpec(memory_space=pl.ANY),
                  pl.BlockSpec(memory_space=pl.ANY),
                  pl.BlockSpec((tm, dm), lambda i: (i, 0)),
                  pl.BlockSpec((8, tm), lambda i: (0, i)),
                  _resident(ln.shape)],
        out_specs=pl.BlockSpec((tm, dm), lambda i: (i, 0)),
        out_shape=jax.ShapeDtypeStruct((t, dm), F32),
        scratch_shapes=[pltpu.SMEM((tm,), I32), pltpu.SMEM((tm,), I32), pltpu.VMEM((2, tm, dm), F32),
                        pltpu.SemaphoreType.DMA((2,)), pltpu.SemaphoreType.DMA((2,))],
        compiler_params=_params(("arbitrary",)),
        name="moe_combine",
    )(slots, y_sorted, h, rec, ln)


def _moe(h, rec, counts, w_gate, w_up, w_down, ln):
    t, dm = h.shape
    tm = MOE_TILE
    n_tiles = (2 * t) // tm + N_EXPERTS
    cnt = counts[:, 0].astype(I32)
    padded = (cnt + tm - 1) // tm * tm
    pad_end = jnp.cumsum(padded)
    pad_start = pad_end - padded
    experts = jnp.arange(N_EXPERTS, dtype=I32)

    def slot_of(e_row, pos_row):
        e = e_row.astype(I32)
        start = jnp.sum(jnp.where(e[None, :] == experts[:, None], pad_start[:, None], 0), axis=0)
        return start + pos_row.astype(I32)

    slots = jnp.stack([slot_of(rec[2], rec[4]), slot_of(rec[3], rec[5])]).reshape(2, t // MOE_ROW_TILE, MOE_ROW_TILE)
    n_used = pad_end[-1] // tm
    tile_start = jnp.minimum(jnp.arange(n_tiles, dtype=I32), n_used - 1) * tm
    tile_expert = jnp.sum(tile_start[:, None] >= pad_end[None, :], axis=1).astype(I32)
    x_sorted = _moe_dispatch(h, slots, pad_start + cnt, padded - cnt, n_tiles * tm)
    y_sorted = _moe_experts(x_sorted, tile_expert, n_used.astype(I32).reshape(1), w_gate, w_up, w_down)
    return _moe_combine(h, y_sorted, slots, rec, ln)


def kernel(x, rel_bias, ln_gain, ln_bias, a_w_in, a_w_out, b_w_in, b_conv, b_w_out, c_w_in, c_w_out,
           d_w_in, d_ln_gain, d_ln_bias, d_w_s, d_b_s, d_w_out, f_w_gate, f_w_up, f_w_down,
           e_w_router, e_w_gate, e_w_up, e_w_down):
    bsz, seq, dm = x.shape
    t = bsz * seq
    assert seq % (8 * MOBA_BLOCK) == 0 and seq == DIL_GROUPS[-1][0] and ln_gain.shape[0] == DEPTH
    assert seq % TOKEN_TILE == 0 and t % MOE_TILE == 0
    q_scale = HEAD_DIM ** -0.5 * LOG2_E

    def ln_rows(i, *pairs):
        return jnp.concatenate([jnp.stack([ln_gain[i, p], ln_bias[i, p]]) for p in pairs], axis=0)

    def router_weights(w):
        w1, w2, _ = _split3(w.T.astype(F32))
        return jnp.stack([w1, w2])

    tri = jnp.triu(jnp.ones((TOKEN_TILE, TOKEN_TILE), BF16), k=1)

    def scaled_qkv(w):
        scale = jnp.concatenate([jnp.full((ATTN_WIDTH,), q_scale, F32), jnp.ones((2 * ATTN_WIDTH,), F32)])
        return (w * scale).astype(BF16)

    h = x
    group_cols = 3 * ATTN_WIDTH
    qkv = [_project_dilated(h, scaled_qkv(a_w_in[0][:, g * group_cols:(g + 1) * group_cols]), dil)
           for g, (_, dil) in enumerate(DIL_GROUPS)]
    tables = [_dilated_bias_table(rel_bias, dil) for _, dil in DIL_GROUPS]
    tables[2] = tables[2][:, :, DIL_BACK:]
    attn = _dilated_attention(qkv, tables)
    h = _attn_out_ffn(h.reshape(t, dm), attn.reshape(t, ATTN_WIDTH), a_w_out[0].astype(BF16),
                      f_w_gate[0].astype(BF16), f_w_up[0].astype(BF16), f_w_down[0].astype(BF16),
                      ln_rows(0, 0, 1))
    h, rec, counts = _conv_mixer(h.reshape(bsz, seq, dm), b_w_in[0].astype(BF16), b_conv[0],
                                 b_w_out[0].astype(BF16), ln_rows(1, 0), router_weights(e_w_router[0]), tri)
    h = _moe(h.reshape(t, dm), rec, counts, e_w_gate[0].astype(BF16), e_w_up[0].astype(BF16),
             e_w_down[0].astype(BF16), ln_rows(1, 1))
    qkv_c, kmean = _project_moba(h.reshape(bsz, seq, dm), scaled_qkv(c_w_in[0]))
    attn = _moba_attention(qkv_c, kmean, _moba_bias_table(rel_bias, seq // MOBA_BLOCK))
    h = _attn_out_ffn(h, attn.reshape(t, ATTN_WIDTH), c_w_out[0].astype(BF16),
                      f_w_gate[1].astype(BF16), f_w_up[1].astype(BF16), f_w_down[1].astype(BF16),
                      ln_rows(2, 0, 1))
    h, rec, counts = _gmlp_mixer(h, d_w_in[0].astype(BF16), jnp.stack([d_ln_gain[0], d_ln_bias[0]]),
                                 d_w_s[0], d_b_s[0].T, d_w_out[0].astype(BF16), ln_rows(3, 0),
                                 router_weights(e_w_router[1]), tri)
    h = _moe(h, rec, counts, e_w_gate[1].astype(BF16), e_w_up[1].astype(BF16),
             e_w_down[1].astype(BF16), ln_rows(3, 1))
    return h.reshape(bsz, seq, dm)
```

```python
import functools
import math

import jax
import jax.numpy as jnp
from jax import lax
from jax.experimental import pallas as pl
from jax.experimental.pallas import tpu as pltpu

F32 = jnp.float32
BF16 = jnp.bfloat16
I32 = jnp.int32

N_HEADS = 16
HEAD_DIM = 64
LANES = 128
HEADS_PER_TILE = LANES // HEAD_DIM
N_HEAD_TILES = N_HEADS // HEADS_PER_TILE
ATTN_WIDTH = N_HEADS * HEAD_DIM
NUM_BUCKETS = 32
MAX_DISTANCE = 2048
DIL_GROUPS = ((128, 1), (512, 4), (2048, 16))
DIL_BACK = 128
DIL_UNROLL = 16
MOBA_BLOCK = 256
MOBA_TOPK = 3
GMLP_CHUNK = 128
GMLP_GROUPS = 8
N_EXPERTS = 8
DEPTH = 4
ALPHA = (2 * DEPTH) ** 0.25
LN_EPS = 1e-5
NEG_INF = -1e30
LOG2_E = math.log2(math.e)

TOKEN_TILE = 512
PROJ_TILE = 1024
MIXER_TILE = 1024
FFN_TILE = 1024
FFN_SPLIT = 256
MOE_TILE = 512
MOE_FF_TILE = 1792
MOE_ROW_TILE = 1024
VMEM_LIMIT = 56 * 1024 * 1024
CAST_BLOCK_BYTES = 8 * 1024 * 1024


def _params(semantics, **kw):
    return pltpu.CompilerParams(dimension_semantics=semantics, vmem_limit_bytes=VMEM_LIMIT, **kw)


def _resident(shape):
    return pl.BlockSpec(shape, lambda *_: (0,) * len(shape), pipeline_mode=pl.Buffered(1))


def _layernorm(z, gain, bias):
    mu = jnp.mean(z, axis=-1, keepdims=True)
    zc = z - mu
    var = jnp.mean(zc * zc, axis=-1, keepdims=True)
    return zc * lax.rsqrt(var + LN_EPS) * gain + bias


def _gelu(x):
    return 0.5 * x * (1.0 + lax.erf(x * math.sqrt(0.5)))


def _dot(a, b):
    return jnp.dot(a, b, preferred_element_type=F32)


def _dot_nt(a, b):
    return lax.dot_general(a, b, (((1,), (1,)), ((), ())), preferred_element_type=F32)


def _split3(x):
    x1 = x.astype(BF16)
    r1 = x - x1.astype(F32)
    x2 = r1.astype(BF16)
    x3 = (r1 - x2.astype(F32)).astype(BF16)
    return x1, x2, x3


def _cast_kernel(w_ref, o_ref):
    o_ref[...] = w_ref[...].astype(o_ref.dtype)


def _expert_weights_bf16(w, layer):
    _, n_e, k, n = w.shape
    bk = max(8, min(k, CAST_BLOCK_BYTES // (4 * n) // 8 * 8))
    while k % bk:
        bk -= 8
    return pl.pallas_call(
        _cast_kernel,
        grid=(n_e, k // bk),
        in_specs=[pl.BlockSpec((None, None, bk, n), lambda e, j: (layer, e, j, 0))],
        out_specs=pl.BlockSpec((None, bk, n), lambda e, j: (e, j, 0)),
        out_shape=jax.ShapeDtypeStruct((n_e, k, n), BF16),
        compiler_params=_params(("parallel", "parallel")),
        name="cast_expert_weights",
    )(w)


def _t5_bucket(dist):
    max_exact = NUM_BUCKETS // 2
    d = jnp.maximum(dist, 0)
    log_ratio = jnp.log(jnp.maximum(d, 1).astype(F32) / max_exact) / math.log(MAX_DISTANCE / max_exact)
    large = max_exact + (log_ratio * (NUM_BUCKETS - max_exact)).astype(I32)
    large = jnp.clip(large, max_exact, NUM_BUCKETS - 1)
    return jnp.where(d < max_exact, d, large)


def _bias_lookup(rel_bias, dist):
    onehot = (_t5_bucket(dist)[..., None] == jnp.arange(NUM_BUCKETS, dtype=I32)).astype(F32)
    out = jnp.einsum('...k,kh->...h', onehot, rel_bias.astype(F32), precision=lax.Precision.HIGHEST)
    return jnp.moveaxis(out, -1, 0) * LOG2_E


def _dilated_bias_table(rel_bias, dil):
    qi = jnp.arange(DIL_BACK, dtype=I32)[:, None]
    kj = jnp.arange(2 * DIL_BACK, dtype=I32)[None, :]
    delta = qi + DIL_BACK - kj
    band = (delta >= 0) & (delta <= DIL_BACK)
    bias = _bias_lookup(rel_bias, delta * dil)
    return jnp.where(band[None], bias, NEG_INF)


def _moba_bias_table(rel_bias, n_blk):
    pos = jnp.arange(MOBA_BLOCK, dtype=I32)
    diff = pos[:, None] - pos[None, :]
    blocks_back = jnp.arange(n_blk - 1, -1, -1, dtype=I32)
    dist = blocks_back[None, :, None] * MOBA_BLOCK + diff[:, None, :]
    bias = jnp.where((dist >= 0)[None], _bias_lookup(rel_bias, dist), NEG_INF)
    return bias.reshape(N_HEADS, MOBA_BLOCK, n_blk * MOBA_BLOCK)


def _proj_kernel(*refs, dil, n_split):
    x_refs, w_ref, o_ref = refs[:-2], refs[-2], refs[-1]
    tm = x_refs[0].shape[0]
    rows = tm // dil
    if dil == 1:
        xb = jnp.concatenate([x_ref[...] for x_ref in x_refs], axis=1).astype(BF16)
    else:
        xb = jnp.concatenate(
            [jnp.concatenate([x_ref[pl.ds(r, rows, stride=dil), :] for x_ref in x_refs], axis=1)
             for r in range(dil)], axis=0).astype(BF16)
    width = w_ref.shape[1] // n_split
    for c in range(n_split):
        y = _dot(xb, w_ref[:, c * width:(c + 1) * width]).astype(BF16)
        for r in range(dil):
            o_ref[r, :, c * width:(c + 1) * width] = y[r * rows:(r + 1) * rows]


def _project_dilated(x, w, group, dil):
    bsz, seq, dm = x.shape
    n = 3 * ATTN_WIDTH
    tm = PROJ_TILE
    return pl.pallas_call(
        functools.partial(_proj_kernel, dil=dil, n_split=3),
        grid=(bsz, seq // tm),
        in_specs=[pl.BlockSpec((None, tm, LANES), lambda b, j, c=c: (b, j, c)) for c in range(dm // LANES)]
        + [pl.BlockSpec((dm, n), lambda b, j: (0, group), pipeline_mode=pl.Buffered(1))],
        out_specs=pl.BlockSpec((None, dil, tm // dil, n), lambda b, j: (b, 0, j, 0)),
        out_shape=jax.ShapeDtypeStruct((bsz, dil, seq // dil, n), BF16),
        compiler_params=_params(("parallel", "parallel")),
        name=f"proj_dil{dil}",
    )(*([x] * (dm // LANES)), w)


def _proj_kmean_kernel(x_ref, w_ref, o_ref, km_ref):
    tm = x_ref.shape[0]
    per_step = tm // MOBA_BLOCK
    j = pl.program_id(1)
    xb = x_ref[...].astype(BF16)
    width = ATTN_WIDTH
    for c in range(3):
        y = _dot(xb, w_ref[:, c * width:(c + 1) * width])
        o_ref[:, c * width:(c + 1) * width] = y.astype(BF16)
        if c == 1:
            for jj in range(per_step):
                km_ref[pl.ds(j * per_step + jj, 1), :] = jnp.mean(
                    y[jj * MOBA_BLOCK:(jj + 1) * MOBA_BLOCK], axis=0, keepdims=True)


def _project_moba(x, w):
    bsz, seq, dm = x.shape
    n = w.shape[1]
    tm = PROJ_TILE
    n_blk = seq // MOBA_BLOCK
    return pl.pallas_call(
        _proj_kmean_kernel,
        grid=(bsz, seq // tm),
        in_specs=[pl.BlockSpec((None, tm, dm), lambda b, j: (b, j, 0)),
                  _resident((dm, n))],
        out_specs=[pl.BlockSpec((None, tm, n), lambda b, j: (b, j, 0)),
                   pl.BlockSpec((None, n_blk, ATTN_WIDTH), lambda b, j: (b, 0, 0))],
        out_shape=[jax.ShapeDtypeStruct((bsz, seq, n), BF16),
                   jax.ShapeDtypeStruct((bsz, n_blk, ATTN_WIDTH), F32)],
        compiler_params=_params(("parallel", "arbitrary")),
        name="proj_moba",
    )(x, w)


def _two_head_partials(units, is_a):
    scores = []
    for q, kw, _, bias_a, bias_b in units:
        zero = jnp.zeros_like(q)
        scores.append((_dot_nt(jnp.where(is_a, q, zero), kw) + bias_a,
                       _dot_nt(jnp.where(is_a, zero, q), kw) + bias_b))
    stats = []
    for pair in scores:
        row = []
        for s in pair:
            m = jnp.max(s, axis=-1, keepdims=True)
            row.append((m, jnp.exp2(s - m).astype(BF16)))
        stats.append(row)
    out = []
    for (_, _, vw, _, _), ((m_a, p_a), (m_b, p_b)) in zip(units, stats):
        pv_a, pv_b = _two_head_pv(p_a, p_b, vw)
        out.append((jnp.where(is_a, m_a, m_b), pltpu.roll(jnp.where(is_a, pv_b, pv_a), HEAD_DIM, axis=1),
                    jnp.where(is_a, pv_a, pv_b)))
    return out


def _two_head_pv(p_a, p_b, vw):
    own_a = lax.broadcasted_iota(I32, vw.shape, 1) < HEAD_DIM
    one = jnp.ones_like(vw)
    return _dot(p_a, jnp.where(own_a, vw, one)), _dot(p_b, jnp.where(own_a, one, vw))


def _dilated_attn_kernel(q0, k0, v0, q1, k1, v1, q2, k2, v2, b0, b1, b2, o_ref, m_s, l_s, a_s, tmp_s):
    nb = DIL_BACK
    is_a = lax.broadcasted_iota(I32, (nb, LANES), 1) < HEAD_DIM
    n_res2, n_res1 = q2.shape[0], q1.shape[0]
    n_blk1 = q1.shape[1] // nb
    n_blk0 = q0.shape[0] // nb

    def merge(rows, m, l, o):
        m_old, l_old, a_old = m_s[rows, :], l_s[rows, :], a_s[rows, :]
        m_new = jnp.maximum(m_old, m)
        w_old, w_new = jnp.exp2(m_old - m_new), jnp.exp2(m - m_new)
        return m_new, w_old * l_old + w_new * l, w_old * a_old + w_new * o

    unroll = DIL_UNROLL

    pitch = tmp_s.shape[1] // n_res2

    def group2(i, carry):
        res = [i * unroll + u for u in range(unroll)]
        parts = _two_head_partials([(q2[r], k2[r], v2[r], b2[0], b2[1]) for r in res], is_a)
        for r, part in zip(res, parts):
            rows = pl.ds(pl.multiple_of(r * pitch, 8), nb)
            for c in range(3):
                tmp_s[c, rows, :] = part[c]
        return carry

    lax.fori_loop(0, n_res2 // unroll, group2, 0)

    def to_token_major(s, carry):
        dst = pl.ds(pl.multiple_of(s * n_res2, n_res2), n_res2)
        for c, ref in enumerate((m_s, l_s, a_s)):
            ref[dst, :] = tmp_s[c, pl.ds(s, n_res2, stride=pitch), :]
        return carry

    lax.fori_loop(0, nb, to_token_major, 0, unroll=8)

    res_per_body = max(1, unroll // n_blk1)

    def group1(i, carry):
        units, where = [], []
        for rr in range(res_per_body):
            r = i * res_per_body + rr
            units.append((q1[r, 0:nb, :], k1[r, 0:nb, :], v1[r, 0:nb, :], b1[0, :, nb:], b1[1, :, nb:]))
            where.append((r, 0))
            for n in range(1, n_blk1):
                keys = slice((n - 1) * nb, (n + 1) * nb)
                units.append((q1[r, n * nb:(n + 1) * nb, :], k1[r, keys, :], v1[r, keys, :], b1[0], b1[1]))
                where.append((r, n))
        for (r, n), (m, l, o) in zip(where, _two_head_partials(units, is_a)):
            rows = pl.ds(n * nb * n_res1 + r, nb, stride=n_res1)
            m, l, o = merge(rows, m, l, o)
            m_s[rows, :] = m
            l_s[rows, :] = l
            a_s[rows, :] = o
        return carry

    lax.fori_loop(0, n_res1 // res_per_body, group1, 0)

    def finish(rows, m, l, o):
        _, l, o = merge(rows, m, l, o)
        o_ref[rows, :] = (o / l).astype(o_ref.dtype)

    def unit0(n):
        aligned = (lambda v: v) if isinstance(n, int) else (lambda v: pl.multiple_of(v, nb))
        q_rows = pl.ds(aligned(n * nb), nb)
        k_rows = pl.ds(aligned((n - 1) * nb), 2 * nb)
        return q_rows, (q0[q_rows, :], k0[k_rows, :], v0[k_rows, :], b0[0], b0[1])

    def blocks0(units):
        for (q_rows, _), (m, l, o) in zip(units, _two_head_partials([u for _, u in units], is_a)):
            finish(q_rows, m, l, o)

    first = (pl.ds(0, nb), (q0[0:nb, :], k0[0:nb, :], v0[0:nb, :], b0[0, :, nb:], b0[1, :, nb:]))
    blocks0([first] + [unit0(n) for n in range(1, unroll)])

    def group0(i, carry):
        blocks0([unit0(i * unroll + u) for u in range(unroll)])
        return carry

    lax.fori_loop(1, n_blk0 // unroll, group0, 0)


def _dilated_attention(qkv, tables):
    bsz = qkv[0].shape[0]
    seq = qkv[0].shape[1] * qkv[0].shape[2]
    nt = N_HEAD_TILES
    in_specs, args = [], []
    for g, (_, dil) in enumerate(DIL_GROUPS):
        sub = seq // dil
        for part in range(3):
            if dil == 1:
                spec = pl.BlockSpec((None, None, sub, LANES), lambda t, b, part=part: (b, 0, 0, part * nt + t))
            else:
                spec = pl.BlockSpec((None, dil, sub, LANES), lambda t, b, part=part: (b, 0, 0, part * nt + t))
            in_specs.append(spec)
            args.append(qkv[g])
    for g in range(3):
        width = tables[g].shape[2]
        in_specs.append(pl.BlockSpec((HEADS_PER_TILE, DIL_BACK, width), lambda t, b: (t, 0, 0)))
        args.append(tables[g])
    return pl.pallas_call(
        _dilated_attn_kernel,
        grid=(nt, bsz),
        in_specs=in_specs,
        out_specs=pl.BlockSpec((None, seq, LANES), lambda t, b: (b, 0, t)),
        out_shape=jax.ShapeDtypeStruct((bsz, seq, ATTN_WIDTH), BF16),
        scratch_shapes=[pltpu.VMEM((seq, LANES), F32)] * 3
        + [pltpu.VMEM((3, DIL_GROUPS[2][1] * (DIL_BACK + 8), LANES), F32)],
        compiler_params=_params(("parallel", "parallel")),
        name="dilated_attn",
    )(*args)


def _moba_kernel(q_ref, k_ref, v_ref, km_ref, bias_ref, o_ref, kaug_s, nsel_s):
    seq = q_ref.shape[0]
    blk = MOBA_BLOCK
    n_blk = seq // blk
    is_a_row = lax.broadcasted_iota(I32, (n_blk, LANES), 1) < HEAD_DIM
    is_a = lax.broadcasted_iota(I32, (blk, LANES), 1) < HEAD_DIM

    kaug_s[:, :LANES] = k_ref[...]
    key_blk = lax.broadcasted_iota(I32, (seq, LANES), 0) // blk
    col = lax.broadcasted_iota(I32, (seq, LANES), 1)
    kaug_s[:, LANES:] = jnp.where(col == key_blk, NEG_INF, 0.0).astype(BF16)

    q_all = q_ref[...]
    km = km_ref[...]
    q_blk = lax.broadcasted_iota(I32, (n_blk, seq), 1) // blk
    m_idx = lax.broadcasted_iota(I32, (n_blk, seq), 0)
    past = m_idx < q_blk
    n_sel = min(MOBA_TOPK, n_blk - 1)
    pad_rows = jnp.zeros((LANES - n_blk, LANES), F32)
    for hd in range(HEADS_PER_TILE):
        kmh = jnp.where(is_a_row if hd == 0 else jnp.logical_not(is_a_row), km, 0.0)
        gate = sum(_dot_nt(piece, q_all) for piece in _split3(kmh))
        gate = jnp.where(past, gate, NEG_INF)
        rank = jnp.zeros((n_blk, seq), F32)
        for mp in range(n_blk - 1):
            row = gate[mp:mp + 1, :]
            beats = jnp.where(row > gate, 1.0, jnp.where(row == gate, jnp.where(mp < m_idx, 1.0, 0.0), 0.0))
            rank = rank + jnp.where(mp < q_blk, beats, 0.0)
        unselected = jnp.where(past, jnp.where(rank < n_sel, 0.0, 1.0), 1.0)
        unselected = jnp.where(m_idx == q_blk, 0.0, unselected)
        for c in range(seq // LANES):
            tile = jnp.concatenate([unselected[:, c * LANES:(c + 1) * LANES], pad_rows], axis=0)
            nsel_s[hd, c * LANES:(c + 1) * LANES, :] = tile.T.astype(BF16)

    def block_scores(n):
        rows = slice(n * blk, (n + 1) * blk)
        qn = q_ref[rows, :]
        zero = jnp.zeros_like(qn)
        scores = []
        for hd in range(HEADS_PER_TILE):
            qh = jnp.where(is_a, qn, zero) if hd == 0 else jnp.where(is_a, zero, qn)
            qa = jnp.concatenate([qh, nsel_s[hd, rows, :]], axis=1)
            scores.append(_dot_nt(qa, kaug_s[0:(n + 1) * blk, :]) + bias_ref[hd, :, (n_blk - 1 - n) * blk:])
        return scores

    def block_finish(n, scores):
        probs = [jnp.exp2(s - jnp.max(s, axis=-1, keepdims=True)).astype(BF16) for s in scores]
        pv_a, pv_b = _two_head_pv(probs[0], probs[1], v_ref[0:(n + 1) * blk, :])
        denom = pltpu.roll(jnp.where(is_a, pv_b, pv_a), HEAD_DIM, axis=1)
        o_ref[n * blk:(n + 1) * blk, :] = (jnp.where(is_a, pv_a, pv_b) / denom).astype(o_ref.dtype)

    for n in range(n_blk):
        block_finish(n, block_scores(n))


def _moba_attention(qkv, kmean, table):
    bsz, seq, _ = qkv.shape
    nt = N_HEAD_TILES
    n_blk = seq // MOBA_BLOCK
    return pl.pallas_call(
        _moba_kernel,
        grid=(nt, bsz),
        in_specs=[pl.BlockSpec((None, seq, LANES), lambda t, b: (b, 0, t)),
                  pl.BlockSpec((None, seq, LANES), lambda t, b: (b, 0, nt + t)),
                  pl.BlockSpec((None, seq, LANES), lambda t, b: (b, 0, 2 * nt + t)),
                  pl.BlockSpec((None, n_blk, LANES), lambda t, b: (b, 0, t)),
                  pl.BlockSpec((HEADS_PER_TILE, MOBA_BLOCK, seq), lambda t, b: (t, 0, 0))],
        out_specs=pl.BlockSpec((None, seq, LANES), lambda t, b: (b, 0, t)),
        out_shape=jax.ShapeDtypeStruct((bsz, seq, ATTN_WIDTH), BF16),
        scratch_shapes=[pltpu.VMEM((seq, 2 * LANES), BF16),
                        pltpu.VMEM((HEADS_PER_TILE, seq, LANES), BF16)],
        compiler_params=_params(("parallel", "parallel")),
        name="moba_attn",
    )(qkv, qkv, qkv, kmean, table)


def _attn_out_ffn_kernel(x_ref, a_ref, wo_ref, wg_ref, wu_ref, wd_ref, ln_ref, o_ref, *, n_split, n_part):
    rows = x_ref.shape[0] // n_part
    parts = [slice(p * rows, (p + 1) * rows) for p in range(n_part)]
    width = wg_ref.shape[1] // n_split
    proj = [_dot(a_ref[r, :], wo_ref[...]) for r in parts]
    for r, t in zip(parts, proj):
        h = _layernorm(ALPHA * x_ref[r, :] + t, ln_ref[0:1, :], ln_ref[1:2, :])
        hb = h.astype(BF16)
        y = None
        for c in range(n_split):
            cols = slice(c * width, (c + 1) * width)
            g = _dot(hb, wg_ref[:, cols])
            u = _dot(hb, wu_ref[:, cols])
            part = _dot((g * jax.nn.sigmoid(g) * u).astype(BF16), wd_ref[cols, :])
            y = part if y is None else y + part
        o_ref[r, :] = _layernorm(ALPHA * h + y, ln_ref[2:3, :], ln_ref[3:4, :])


def _attn_out_ffn(x, attn, w_out, w_gate, w_up, w_down, ln):
    t, dm = x.shape
    ff = w_gate.shape[1]
    tm = FFN_TILE
    n_split = ff // FFN_SPLIT
    return pl.pallas_call(
        functools.partial(_attn_out_ffn_kernel, n_split=n_split, n_part=2),
        grid=(t // tm,),
        in_specs=[pl.BlockSpec((tm, dm), lambda i: (i, 0)),
                  pl.BlockSpec((tm, attn.shape[1]), lambda i: (i, 0)),
                  _resident(w_out.shape), _resident(w_gate.shape), _resident(w_up.shape),
                  _resident(w_down.shape), _resident(ln.shape)],
        out_specs=pl.BlockSpec((tm, dm), lambda i: (i, 0)),
        out_shape=jax.ShapeDtypeStruct((t, dm), F32),
        compiler_params=_params(("parallel",)),
        name="attn_out_ffn",
    )(x, attn, w_out, w_gate, w_up, w_down, ln)


def _route(h, wr_ref, tri_ref, cnt_s, rec_ref, cnt_ref, cols):
    tm = h.shape[0]
    h1, h2, _ = _split3(h)
    logits = _dot_nt(wr_ref[0], h1) + _dot_nt(wr_ref[1], h1) + _dot_nt(wr_ref[0], h2)
    e_idx = lax.broadcasted_iota(I32, (N_EXPERTS, tm), 0).astype(F32)
    top1 = jnp.max(logits, axis=0, keepdims=True)
    idx1 = jnp.min(jnp.where(logits == top1, e_idx, float(N_EXPERTS)), axis=0, keepdims=True)
    rest = jnp.where(e_idx == idx1, -jnp.inf, logits)
    top2 = jnp.max(rest, axis=0, keepdims=True)
    idx2 = jnp.min(jnp.where(rest == top2, e_idx, float(N_EXPERTS)), axis=0, keepdims=True)
    z = jnp.exp(top2 - top1)
    gate1 = 1.0 / (1.0 + z)
    gate2 = z / (1.0 + z)
    hot1 = jnp.where(e_idx == idx1, 1.0, 0.0)
    hot2 = jnp.where(e_idx == idx2, 1.0, 0.0)
    hot = hot1 + hot2
    before = _dot(hot.astype(BF16), tri_ref[...]) + cnt_s[:, 0:1]
    pos1 = jnp.sum(hot1 * before, axis=0, keepdims=True)
    pos2 = jnp.sum(hot2 * before, axis=0, keepdims=True)
    cnt_s[...] = cnt_s[...] + jnp.sum(hot, axis=1, keepdims=True)
    cnt_ref[...] = cnt_s[...]
    rec_ref[:, cols] = jnp.concatenate([gate1, gate2, idx1, idx2, pos1, pos2, jnp.zeros((2, tm), F32)], axis=0)


def _conv_mixer_kernel(x_ref, wi_ref, cw_ref, wo_ref, ln_ref, wr_ref, tri_ref,
                       o_ref, rec_ref, cnt_ref, y_s, cnt_s):
    tm, dm = x_ref.shape
    first = (pl.program_id(0) == 0) & (pl.program_id(1) == 0)

    @pl.when(first)
    def _():
        cnt_s[...] = jnp.zeros_like(cnt_s)

    @pl.when(pl.program_id(1) == 0)
    def _():
        y_s[0:8, :] = jnp.zeros((8, dm), F32)

    rows = tri_ref.shape[0]
    parts = [slice(p * rows, (p + 1) * rows) for p in range(tm // rows)]
    xbs = [x_ref[r, :].astype(BF16) for r in parts]
    for r, xb in zip(parts, xbs):
        y_s[8 + r.start:8 + r.stop, :] = _dot(xb, wi_ref[:, dm:2 * dm]) * _dot(xb, wi_ref[:, 2 * dm:])
    gate_bs = [_dot(xb, wi_ref[:, :dm]) for xb in xbs]
    mixed = []
    for r, gate_b in zip(parts, gate_bs):
        z = (cw_ref[0:1, :] * y_s[6 + r.start:6 + r.stop, :] + cw_ref[1:2, :] * y_s[7 + r.start:7 + r.stop, :]
             + cw_ref[2:3, :] * y_s[8 + r.start:8 + r.stop, :])
        mixed.append(_dot((gate_b * z).astype(BF16), wo_ref[...]))
    y_s[0:8, :] = y_s[tm:, :]
    hs = [_layernorm(ALPHA * x_ref[r, :] + m, ln_ref[0:1, :], ln_ref[1:2, :]) for r, m in zip(parts, mixed)]
    for r, h in zip(parts, hs):
        o_ref[r, :] = h
        _route(h, wr_ref, tri_ref, cnt_s, rec_ref, cnt_ref, r)


def _conv_mixer(x, w_in, conv_w, w_out, ln, wr, tri):
    bsz, seq, dm = x.shape
    tm = MIXER_TILE
    nj = seq // tm
    return pl.pallas_call(
        _conv_mixer_kernel,
        grid=(bsz, nj),
        in_specs=[pl.BlockSpec((None, tm, dm), lambda b, j: (b, j, 0)),
                  _resident(w_in.shape), _resident(conv_w.shape), _resident(w_out.shape),
                  _resident(ln.shape), _resident(wr.shape), _resident(tri.shape)],
        out_specs=[pl.BlockSpec((None, tm, dm), lambda b, j: (b, j, 0)),
                   pl.BlockSpec((8, tm), lambda b, j: (0, b * nj + j)),
                   pl.BlockSpec((N_EXPERTS, LANES), lambda b, j: (0, 0))],
        out_shape=[jax.ShapeDtypeStruct((bsz, seq, dm), F32),
                   jax.ShapeDtypeStruct((8, bsz * seq), F32),
                   jax.ShapeDtypeStruct((N_EXPERTS, LANES), F32)],
        scratch_shapes=[pltpu.VMEM((tm + 8, dm), F32), pltpu.VMEM((N_EXPERTS, LANES), F32)],
        compiler_params=_params(("arbitrary", "arbitrary")),
        name="conv_mixer",
    )(x, w_in, conv_w, w_out, ln, wr, tri)


def _gmlp_mixer_kernel(x_ref, wi_ref, dln_ref, ws_ref, bs_ref, wo_ref, ln_ref, wr_ref, tri_ref,
                       o_ref, rec_ref, cnt_ref, mix_s, cnt_s):
    tm, dm = x_ref.shape
    ch = GMLP_CHUNK
    gw = dm // GMLP_GROUPS

    @pl.when(pl.program_id(0) == 0)
    def _():
        cnt_s[...] = jnp.zeros_like(cnt_s)

    rows = tri_ref.shape[0]
    parts = [slice(p * rows, (p + 1) * rows) for p in range(tm // rows)]
    lower = lax.broadcasted_iota(I32, (ch, ch), 0) >= lax.broadcasted_iota(I32, (ch, ch), 1)
    ws = [jnp.where(lower, ws_ref[g], 0.0).astype(BF16) for g in range(GMLP_GROUPS)]
    pre = []
    for r in parts:
        xb = x_ref[r, :].astype(BF16)
        pre.append((_dot(xb, wi_ref[:, :dm]), _dot(xb, wi_ref[:, dm:])))
    mixed = []
    for r, (u_pre, v_pre) in zip(parts, pre):
        vb = _layernorm(_gelu(v_pre), dln_ref[0:1, :], dln_ref[1:2, :]).astype(BF16)
        for g in range(GMLP_GROUPS):
            for c in range(rows // ch):
                blk = _dot(ws[g], vb[c * ch:(c + 1) * ch, g * gw:(g + 1) * gw]) + bs_ref[:, g:g + 1]
                mix_s[r.start + c * ch:r.start + (c + 1) * ch, g * gw:(g + 1) * gw] = blk
        mixed.append(_dot((_gelu(u_pre) * mix_s[r, :]).astype(BF16), wo_ref[...]))
    hs = [_layernorm(ALPHA * x_ref[r, :] + m, ln_ref[0:1, :], ln_ref[1:2, :]) for r, m in zip(parts, mixed)]
    for r, h in zip(parts, hs):
        o_ref[r, :] = h
        _route(h, wr_ref, tri_ref, cnt_s, rec_ref, cnt_ref, r)


def _gmlp_mixer(x, w_in, dln, w_s, b_s_t, w_out, ln, wr, tri):
    t, dm = x.shape
    tm = MIXER_TILE
    return pl.pallas_call(
        _gmlp_mixer_kernel,
        grid=(t // tm,),
        in_specs=[pl.BlockSpec((tm, dm), lambda i: (i, 0)),
                  _resident(w_in.shape), _resident(dln.shape), _resident(w_s.shape),
                  _resident(b_s_t.shape), _resident(w_out.shape), _resident(ln.shape),
                  _resident(wr.shape), _resident(tri.shape)],
        out_specs=[pl.BlockSpec((tm, dm), lambda i: (i, 0)),
                   pl.BlockSpec((8, tm), lambda i: (0, i)),
                   pl.BlockSpec((N_EXPERTS, LANES), lambda i: (0, 0))],
        out_shape=[jax.ShapeDtypeStruct((t, dm), F32),
                   jax.ShapeDtypeStruct((8, t), F32),
                   jax.ShapeDtypeStruct((N_EXPERTS, LANES), F32)],
        scratch_shapes=[pltpu.VMEM((tm, dm), F32), pltpu.VMEM((N_EXPERTS, LANES), F32)],
        compiler_params=_params(("arbitrary",)),
        name="gmlp_mixer",
    )(x, w_in, dln, w_s, b_s_t, w_out, ln, wr, tri)


def _load_slots(slot_hbm, slot_s, sem, i):
    copies = [pltpu.make_async_copy(slot_hbm.at[c, i], slot_s[c], sem.at[c]) for c in range(2)]
    for cp in copies:
        cp.start()
    for cp in copies:
        cp.wait()


def _start_row_copies(n_rows, row_copy, slot_s):
    for r in range(n_rows):
        for c in range(2):
            row_copy(r, c, slot_s[c][r]).start(priority=(r + c) % 2)


def _moe_dispatch_kernel(fs_ref, fc_ref, slot_hbm, h_ref, x_hbm, slot0_s, slot1_s, zero_s, idx_sem, row_sem):
    i = pl.program_id(0)
    tm = h_ref.shape[0]
    slot_s = (slot0_s, slot1_s)
    _load_slots(slot_hbm, slot_s, idx_sem, i)

    def row_copy(r, c, slot):
        return pltpu.make_async_copy(h_ref.at[pl.ds(r, 1), :], x_hbm.at[pl.ds(slot, 1), :], row_sem.at[c])

    _start_row_copies(tm, row_copy, slot_s)

    @pl.when(i == 0)
    def _():
        zero_s[...] = jnp.zeros_like(zero_s)

        def fill_copy(row):
            return pltpu.make_async_copy(zero_s.at[pl.ds(0, 1), :], x_hbm.at[pl.ds(row, 1), :], row_sem.at[2])

        for e in range(N_EXPERTS):
            def fill(j, carry, e=e):
                fill_copy(fs_ref[e] + j).start()
                return carry

            lax.fori_loop(0, fc_ref[e], fill, 0)

            def drain(j, carry):
                fill_copy(0).wait()
                return carry

            lax.fori_loop(0, fc_ref[e], drain, 0)

    for c in range(2):
        pltpu.make_async_copy(h_ref, x_hbm.at[pl.ds(0, tm), :], row_sem.at[c]).wait()


def _moe_dispatch(h, slots, fill_start, fill_count, n_slots):
    t, dm = h.shape
    tm = slots.shape[2]
    grid_spec = pltpu.PrefetchScalarGridSpec(
        num_scalar_prefetch=2,
        grid=(t // tm,),
        in_specs=[pl.BlockSpec(memory_space=pl.ANY),
                  pl.BlockSpec((tm, dm), lambda i, fs, fc: (i, 0))],
        out_specs=pl.BlockSpec(memory_space=pl.ANY),
        scratch_shapes=[pltpu.SMEM((tm,), I32), pltpu.SMEM((tm,), I32), pltpu.VMEM((8, dm), F32),
                        pltpu.SemaphoreType.DMA((2,)), pltpu.SemaphoreType.DMA((3,))],
    )
    return pl.pallas_call(
        _moe_dispatch_kernel,
        grid_spec=grid_spec,
        out_shape=jax.ShapeDtypeStruct((n_slots, dm), F32),
        compiler_params=_params(("arbitrary",)),
        name="moe_dispatch",
    )(fill_start, fill_count, slots, h)


def _moe_expert_kernel(te_ref, nu_ref, x_ref, wg_ref, wu_ref, wd_ref, y_ref, xb_s):
    i, f = pl.program_id(0), pl.program_id(1)

    @pl.when(i < nu_ref[0])
    def _():
        @pl.when(f == 0)
        def _():
            xb_s[...] = x_ref[...].astype(BF16)
            y_ref[...] = jnp.zeros_like(y_ref)

        xb = xb_s[...]
        g = _dot(xb, wg_ref[...])
        u = _dot(xb, wu_ref[...])
        y_ref[...] += _dot((g * jax.nn.sigmoid(g) * u).astype(BF16), wd_ref[...])


def _moe_experts(x_sorted, tile_expert, n_used, w_gate, w_up, w_down):
    n_slots, dm = x_sorted.shape
    tm = MOE_TILE
    ff = w_gate.shape[2]
    tf = MOE_FF_TILE

    def tile_block(i, f, te, nu):
        return jnp.minimum(i, nu[0] - 1), 0

    def col_block(i, f, te, nu):
        return te[i], 0, jnp.where(i < nu[0], f, ff // tf - 1)

    def row_block(i, f, te, nu):
        return te[i], jnp.where(i < nu[0], f, ff // tf - 1), 0

    grid_spec = pltpu.PrefetchScalarGridSpec(
        num_scalar_prefetch=2,
        grid=(n_slots // tm, ff // tf),
        in_specs=[pl.BlockSpec((tm, dm), tile_block),
                  pl.BlockSpec((None, dm, tf), col_block),
                  pl.BlockSpec((None, dm, tf), col_block),
                  pl.BlockSpec((None, tf, dm), row_block)],
        out_specs=pl.BlockSpec((tm, dm), tile_block),
        scratch_shapes=[pltpu.VMEM((tm, dm), BF16)],
    )
    return pl.pallas_call(
        _moe_expert_kernel,
        grid_spec=grid_spec,
        out_shape=jax.ShapeDtypeStruct((n_slots, dm), F32),
        compiler_params=_params(("arbitrary", "arbitrary")),
        name="moe_experts",
    )(tile_expert, n_used, x_sorted, w_gate, w_up, w_down)


def _moe_combine_kernel(slot_hbm, y_hbm, h_ref, rec_ref, ln_ref, o_ref, slot0_s, slot1_s, y_s, idx_sem, row_sem):
    i = pl.program_id(0)
    tm = h_ref.shape[0]
    slot_s = (slot0_s, slot1_s)
    _load_slots(slot_hbm, slot_s, idx_sem, i)

    def row_copy(r, c, slot):
        return pltpu.make_async_copy(y_hbm.at[pl.ds(slot, 1), :], y_s.at[c, pl.ds(r, 1), :], row_sem.at[c])

    _start_row_copies(tm, row_copy, slot_s)
    for c in range(2):
        pltpu.make_async_copy(y_hbm.at[pl.ds(0, tm), :], y_s.at[c], row_sem.at[c]).wait()

    pad_rows = jnp.zeros((LANES - 8, LANES), F32)
    for c in range(tm // LANES):
        rows = slice(c * LANES, (c + 1) * LANES)
        gates = jnp.concatenate([rec_ref[:, rows], pad_rows], axis=0).T
        y = gates[:, 0:1] * y_s[0, rows, :] + gates[:, 1:2] * y_s[1, rows, :]
        o_ref[rows, :] = _layernorm(ALPHA * h_ref[rows, :] + y, ln_ref[0:1, :], ln_ref[1:2, :])


def _moe_combine(h, y_sorted, slots, rec, ln):
    t, dm = h.shape
    tm = slots.shape[2]
    return pl.pallas_call(
        _moe_combine_kernel,
        grid=(t // tm,),
        in_specs=[pl.BlockSpec(memory_space=pl.ANY),
                  pl.BlockSpec(memory_space=pl.ANY),
                  pl.BlockSpec((tm, dm), lambda i: (i, 0)),
                  pl.BlockSpec((8, tm), lambda i: (0, i)),
                  _resident(ln.shape)],
        out_specs=pl.BlockSpec((tm, dm), lambda i: (i, 0)),
        out_shape=jax.ShapeDtypeStruct((t, dm), F32),
        scratch_shapes=[pltpu.SMEM((tm,), I32), pltpu.SMEM((tm,), I32), pltpu.VMEM((2, tm, dm), F32),
                        pltpu.SemaphoreType.DMA((2,)), pltpu.SemaphoreType.DMA((2,))],
        compiler_params=_params(("arbitrary",)),
        name="moe_combine",
    )(slots, y_sorted, h, rec, ln)


def _moe(h, rec, counts, w_gate, w_up, w_down, ln):
    t, dm = h.shape
    tm = MOE_TILE
    n_tiles = (2 * t) // tm + N_EXPERTS
    cnt = counts[:, 0].astype(I32)
    padded = (cnt + tm - 1) // tm * tm
    pad_end = jnp.cumsum(padded)
    pad_start = pad_end - padded
    experts = jnp.arange(N_EXPERTS, dtype=I32)

    def slot_of(e_row, pos_row):
        e = e_row.astype(I32)
        start = jnp.sum(jnp.where(e[None, :] == experts[:, None], pad_start[:, None], 0), axis=0)
        return start + pos_row.astype(I32)

    slots = jnp.stack([slot_of(rec[2], rec[4]), slot_of(rec[3], rec[5])]).reshape(2, t // MOE_ROW_TILE, MOE_ROW_TILE)
    n_used = pad_end[-1] // tm
    tile_start = jnp.minimum(jnp.arange(n_tiles, dtype=I32), n_used - 1) * tm
    tile_expert = jnp.sum(tile_start[:, None] >= pad_end[None, :], axis=1).astype(I32)
    x_sorted = _moe_dispatch(h, slots, pad_start + cnt, padded - cnt, n_tiles * tm)
    y_sorted = _moe_experts(x_sorted, tile_expert, n_used.astype(I32).reshape(1), w_gate, w_up, w_down)
    return _moe_combine(h, y_sorted, slots, rec, ln)


def kernel(x, rel_bias, ln_gain, ln_bias, a_w_in, a_w_out, b_w_in, b_conv, b_w_out, c_w_in, c_w_out,
           d_w_in, d_ln_gain, d_ln_bias, d_w_s, d_b_s, d_w_out, f_w_gate, f_w_up, f_w_down,
           e_w_router, e_w_gate, e_w_up, e_w_down):
    bsz, seq, dm = x.shape
    t = bsz * seq
    assert seq % (8 * MOBA_BLOCK) == 0 and seq == DIL_GROUPS[-1][0] and ln_gain.shape[0] == DEPTH
    assert seq % PROJ_TILE == 0 and seq % MIXER_TILE == 0 and t % MOE_TILE == 0
    q_scale = HEAD_DIM ** -0.5 * LOG2_E

    def ln_rows(i, *pairs):
        return jnp.concatenate([jnp.stack([ln_gain[i, p], ln_bias[i, p]]) for p in pairs], axis=0)

    def router_weights(w):
        w1, w2, _ = _split3(w.T.astype(F32))
        return jnp.stack([w1, w2])

    tri = jnp.triu(jnp.ones((TOKEN_TILE, TOKEN_TILE), BF16), k=1)

    def scaled_qkv(w):
        scale = jnp.concatenate([jnp.full((ATTN_WIDTH,), q_scale, F32), jnp.ones((2 * ATTN_WIDTH,), F32)])
        return (w * jnp.tile(scale, w.shape[1] // scale.shape[0])).astype(BF16)

    h = x
    a_w = scaled_qkv(a_w_in[0])
    qkv = [_project_dilated(h, a_w, g, dil) for g, (_, dil) in enumerate(DIL_GROUPS)]
    tables = [_dilated_bias_table(rel_bias, dil) for _, dil in DIL_GROUPS]
    tables[2] = tables[2][:, :, DIL_BACK:]
    attn = _dilated_attention(qkv, tables)
    h = _attn_out_ffn(h.reshape(t, dm), attn.reshape(t, ATTN_WIDTH), a_w_out[0].astype(BF16),
                      f_w_gate[0].astype(BF16), f_w_up[0].astype(BF16), f_w_down[0].astype(BF16),
                      ln_rows(0, 0, 1))
    h, rec, counts = _conv_mixer(h.reshape(bsz, seq, dm), b_w_in[0].astype(BF16), b_conv[0],
                                 b_w_out[0].astype(BF16), ln_rows(1, 0), router_weights(e_w_router[0]), tri)
    h = _moe(h.reshape(t, dm), rec, counts, _expert_weights_bf16(e_w_gate, 0), _expert_weights_bf16(e_w_up, 0),
             _expert_weights_bf16(e_w_down, 0), ln_rows(1, 1))
    qkv_c, kmean = _project_moba(h.reshape(bsz, seq, dm), scaled_qkv(c_w_in[0]))
    attn = _moba_attention(qkv_c, kmean, _moba_bias_table(rel_bias, seq // MOBA_BLOCK))
    h = _attn_out_ffn(h, attn.reshape(t, ATTN_WIDTH), c_w_out[0].astype(BF16),
                      f_w_gate[1].astype(BF16), f_w_up[1].astype(BF16), f_w_down[1].astype(BF16),
                      ln_rows(2, 0, 1))
    h, rec, counts = _gmlp_mixer(h, d_w_in[0].astype(BF16), jnp.stack([d_ln_gain[0], d_ln_bias[0]]),
                                 d_w_s[0], d_b_s[0].T, d_w_out[0].astype(BF16), ln_rows(3, 0),
                                 router_weights(e_w_router[1]), tri)
    h = _moe(h, rec, counts, _expert_weights_bf16(e_w_gate, 1), _expert_weights_bf16(e_w_up, 1),
             _expert_weights_bf16(e_w_down, 1), ln_rows(3, 1))
    return h.reshape(bsz, seq, dm)
```

```python
import functools
import math

import jax
import jax.numpy as jnp
from jax import lax
from jax.experimental import pallas as pl
from jax.experimental.pallas import tpu as pltpu

F32 = jnp.float32
BF16 = jnp.bfloat16
I32 = jnp.int32

N_HEADS = 16
HEAD_DIM = 64
LANES = 128
HEADS_PER_TILE = LANES // HEAD_DIM
N_HEAD_TILES = N_HEADS // HEADS_PER_TILE
ATTN_WIDTH = N_HEADS * HEAD_DIM
NUM_BUCKETS = 32
MAX_DISTANCE = 2048
DIL_GROUPS = ((128, 1), (512, 4), (2048, 16))
DIL_BACK = 128
DIL_UNROLL = 16
MOBA_BLOCK = 256
MOBA_TOPK = 3
GMLP_CHUNK = 128
GMLP_GROUPS = 8
N_EXPERTS = 8
DEPTH = 4
ALPHA = (2 * DEPTH) ** 0.25
LN_EPS = 1e-5
NEG_INF = -1e30
LOG2_E = math.log2(math.e)

TOKEN_TILE = 512
PROJ_TILE = 1024
MIXER_TILE = 1024
FFN_TILE = 1024
FFN_SPLIT = 256
MOE_TILE = 512
MOE_FF_TILE = 1792
MOE_ROW_TILE = 1024
VMEM_LIMIT = 56 * 1024 * 1024
CAST_BLOCK_BYTES = 8 * 1024 * 1024


def _params(semantics, **kw):
    return pltpu.CompilerParams(dimension_semantics=semantics, vmem_limit_bytes=VMEM_LIMIT, **kw)


def _resident(shape):
    return pl.BlockSpec(shape, lambda *_: (0,) * len(shape), pipeline_mode=pl.Buffered(1))


def _layernorm(z, gain, bias):
    mu = jnp.mean(z, axis=-1, keepdims=True)
    zc = z - mu
    var = jnp.mean(zc * zc, axis=-1, keepdims=True)
    return zc * lax.rsqrt(var + LN_EPS) * gain + bias


def _gelu(x):
    return 0.5 * x * (1.0 + lax.erf(x * math.sqrt(0.5)))


def _dot(a, b):
    return jnp.dot(a, b, preferred_element_type=F32)


def _dot_nt(a, b):
    return lax.dot_general(a, b, (((1,), (1,)), ((), ())), preferred_element_type=F32)


def _split3(x):
    x1 = x.astype(BF16)
    r1 = x - x1.astype(F32)
    x2 = r1.astype(BF16)
    x3 = (r1 - x2.astype(F32)).astype(BF16)
    return x1, x2, x3


def _cast_kernel(w_ref, o_ref):
    o_ref[...] = w_ref[...].astype(o_ref.dtype)


def _expert_weights_bf16(w, layer):
    _, n_e, k, n = w.shape
    bk = max(8, min(k, CAST_BLOCK_BYTES // (4 * n) // 8 * 8))
    while k % bk:
        bk -= 8
    return pl.pallas_call(
        _cast_kernel,
        grid=(n_e, k // bk),
        in_specs=[pl.BlockSpec((None, None, bk, n), lambda e, j: (layer, e, j, 0))],
        out_specs=pl.BlockSpec((None, bk, n), lambda e, j: (e, j, 0)),
        out_shape=jax.ShapeDtypeStruct((n_e, k, n), BF16),
        compiler_params=_params(("parallel", "parallel")),
        name="cast_expert_weights",
    )(w)


def _t5_bucket(dist):
    max_exact = NUM_BUCKETS // 2
    d = jnp.maximum(dist, 0)
    log_ratio = jnp.log(jnp.maximum(d, 1).astype(F32) / max_exact) / math.log(MAX_DISTANCE / max_exact)
    large = max_exact + (log_ratio * (NUM_BUCKETS - max_exact)).astype(I32)
    large = jnp.clip(large, max_exact, NUM_BUCKETS - 1)
    return jnp.where(d < max_exact, d, large)


def _bias_lookup(rel_bias, dist):
    onehot = (_t5_bucket(dist)[..., None] == jnp.arange(NUM_BUCKETS, dtype=I32)).astype(F32)
    out = jnp.einsum('...k,kh->...h', onehot, rel_bias.astype(F32), precision=lax.Precision.HIGHEST)
    return jnp.moveaxis(out, -1, 0) * LOG2_E


def _dilated_bias_table(rel_bias, dil):
    qi = jnp.arange(DIL_BACK, dtype=I32)[:, None]
    kj = jnp.arange(2 * DIL_BACK, dtype=I32)[None, :]
    delta = qi + DIL_BACK - kj
    band = (delta >= 0) & (delta <= DIL_BACK)
    bias = _bias_lookup(rel_bias, delta * dil)
    return jnp.where(band[None], bias, NEG_INF)


def _moba_bias_table(rel_bias, n_blk):
    pos = jnp.arange(MOBA_BLOCK, dtype=I32)
    diff = pos[:, None] - pos[None, :]
    blocks_back = jnp.arange(n_blk - 1, -1, -1, dtype=I32)
    dist = blocks_back[None, :, None] * MOBA_BLOCK + diff[:, None, :]
    bias = jnp.where((dist >= 0)[None], _bias_lookup(rel_bias, dist), NEG_INF)
    return bias.reshape(N_HEADS, MOBA_BLOCK, n_blk * MOBA_BLOCK)


def _proj_kernel(*refs, dil, n_split):
    x_refs, w_ref, o_ref = refs[:-2], refs[-2], refs[-1]
    tm = x_refs[0].shape[0]
    rows = tm // dil
    if dil == 1:
        xb = jnp.concatenate([x_ref[...] for x_ref in x_refs], axis=1).astype(BF16)
    else:
        xb = jnp.concatenate(
            [jnp.concatenate([x_ref[pl.ds(r, rows, stride=dil), :] for x_ref in x_refs], axis=1)
             for r in range(dil)], axis=0).astype(BF16)
    width = w_ref.shape[1] // n_split
    for c in range(n_split):
        y = _dot(xb, w_ref[:, c * width:(c + 1) * width]).astype(BF16)
        for r in range(dil):
            o_ref[r, :, c * width:(c + 1) * width] = y[r * rows:(r + 1) * rows]


def _project_dilated(x, w, group, dil):
    bsz, seq, dm = x.shape
    n = 3 * ATTN_WIDTH
    tm = PROJ_TILE
    return pl.pallas_call(
        functools.partial(_proj_kernel, dil=dil, n_split=3),
        grid=(bsz, seq // tm),
        in_specs=[pl.BlockSpec((None, tm, LANES), lambda b, j, c=c: (b, j, c)) for c in range(dm // LANES)]
        + [pl.BlockSpec((dm, n), lambda b, j: (0, group), pipeline_mode=pl.Buffered(1))],
        out_specs=pl.BlockSpec((None, dil, tm // dil, n), lambda b, j: (b, 0, j, 0)),
        out_shape=jax.ShapeDtypeStruct((bsz, dil, seq // dil, n), BF16),
        compiler_params=_params(("parallel", "parallel")),
        name=f"proj_dil{dil}",
    )(*([x] * (dm // LANES)), w)


def _proj_kmean_kernel(x_ref, w_ref, o_ref, km_ref):
    tm = x_ref.shape[0]
    per_step = tm // MOBA_BLOCK
    j = pl.program_id(1)
    xb = x_ref[...].astype(BF16)
    width = ATTN_WIDTH
    for c in range(3):
        y = _dot(xb, w_ref[:, c * width:(c + 1) * width])
        o_ref[:, c * width:(c + 1) * width] = y.astype(BF16)
        if c == 1:
            for jj in range(per_step):
                km_ref[pl.ds(j * per_step + jj, 1), :] = jnp.mean(
                    y[jj * MOBA_BLOCK:(jj + 1) * MOBA_BLOCK], axis=0, keepdims=True)


def _project_moba(x, w):
    bsz, seq, dm = x.shape
    n = w.shape[1]
    tm = PROJ_TILE
    n_blk = seq // MOBA_BLOCK
    return pl.pallas_call(
        _proj_kmean_kernel,
        grid=(bsz, seq // tm),
        in_specs=[pl.BlockSpec((None, tm, dm), lambda b, j: (b, j, 0)),
                  _resident((dm, n))],
        out_specs=[pl.BlockSpec((None, tm, n), lambda b, j: (b, j, 0)),
                   pl.BlockSpec((None, n_blk, ATTN_WIDTH), lambda b, j: (b, 0, 0))],
        out_shape=[jax.ShapeDtypeStruct((bsz, seq, n), BF16),
                   jax.ShapeDtypeStruct((bsz, n_blk, ATTN_WIDTH), F32)],
        compiler_params=_params(("parallel", "arbitrary")),
        name="proj_moba",
    )(x, w)


def _two_head_partials(units, is_a):
    scores = []
    for q, kw, _, bias_a, bias_b in units:
        zero = jnp.zeros_like(q)
        scores.append((_dot_nt(jnp.where(is_a, q, zero), kw) + bias_a,
                       _dot_nt(jnp.where(is_a, zero, q), kw) + bias_b))
    stats = []
    for pair in scores:
        row = []
        for s in pair:
            m = jnp.max(s, axis=-1, keepdims=True)
            row.append((m, jnp.exp2(s - m).astype(BF16)))
        stats.append(row)
    out = []
    for (_, _, vw, _, _), ((m_a, p_a), (m_b, p_b)) in zip(units, stats):
        pv_a, pv_b = _two_head_pv(p_a, p_b, vw)
        out.append((jnp.where(is_a, m_a, m_b), pltpu.roll(jnp.where(is_a, pv_b, pv_a), HEAD_DIM, axis=1),
                    jnp.where(is_a, pv_a, pv_b)))
    return out


def _two_head_pv(p_a, p_b, vw):
    own_a = lax.broadcasted_iota(I32, vw.shape, 1) < HEAD_DIM
    one = jnp.ones_like(vw)
    return _dot(p_a, jnp.where(own_a, vw, one)), _dot(p_b, jnp.where(own_a, one, vw))


def _dilated_attn_kernel(q0, k0, v0, q1, k1, v1, q2, k2, v2, b0, b1, b2, o_ref, m_s, l_s, a_s, tmp_s):
    nb = DIL_BACK
    is_a = lax.broadcasted_iota(I32, (nb, LANES), 1) < HEAD_DIM
    n_res2, n_res1 = q2.shape[0], q1.shape[0]
    n_blk1 = q1.shape[1] // nb
    n_blk0 = q0.shape[0] // nb

    def merge(rows, m, l, o):
        m_old, l_old, a_old = m_s[rows, :], l_s[rows, :], a_s[rows, :]
        m_new = jnp.maximum(m_old, m)
        w_old, w_new = jnp.exp2(m_old - m_new), jnp.exp2(m - m_new)
        return m_new, w_old * l_old + w_new * l, w_old * a_old + w_new * o

    unroll = DIL_UNROLL

    pitch = tmp_s.shape[1] // n_res2

    def group2(i, carry):
        res = [i * unroll + u for u in range(unroll)]
        parts = _two_head_partials([(q2[r], k2[r], v2[r], b2[0], b2[1]) for r in res], is_a)
        for r, part in zip(res, parts):
            rows = pl.ds(pl.multiple_of(r * pitch, 8), nb)
            for c in range(3):
                tmp_s[c, rows, :] = part[c]
        return carry

    lax.fori_loop(0, n_res2 // unroll, group2, 0)

    def to_token_major(s, carry):
        dst = pl.ds(pl.multiple_of(s * n_res2, n_res2), n_res2)
        for c, ref in enumerate((m_s, l_s, a_s)):
            ref[dst, :] = tmp_s[c, pl.ds(s, n_res2, stride=pitch), :]
        return carry

    lax.fori_loop(0, nb, to_token_major, 0, unroll=8)

    res_per_body = max(1, unroll // n_blk1)

    def group1(i, carry):
        units, where = [], []
        for rr in range(res_per_body):
            r = i * res_per_body + rr
            units.append((q1[r, 0:nb, :], k1[r, 0:nb, :], v1[r, 0:nb, :], b1[0, :, nb:], b1[1, :, nb:]))
            where.append((r, 0))
            for n in range(1, n_blk1):
                keys = slice((n - 1) * nb, (n + 1) * nb)
                units.append((q1[r, n * nb:(n + 1) * nb, :], k1[r, keys, :], v1[r, keys, :], b1[0], b1[1]))
                where.append((r, n))
        for (r, n), (m, l, o) in zip(where, _two_head_partials(units, is_a)):
            rows = pl.ds(n * nb * n_res1 + r, nb, stride=n_res1)
            m, l, o = merge(rows, m, l, o)
            m_s[rows, :] = m
            l_s[rows, :] = l
            a_s[rows, :] = o
        return carry

    lax.fori_loop(0, n_res1 // res_per_body, group1, 0)

    def finish(rows, m, l, o):
        _, l, o = merge(rows, m, l, o)
        o_ref[rows, :] = (o / l).astype(o_ref.dtype)

    def unit0(n):
        aligned = (lambda v: v) if isinstance(n, int) else (lambda v: pl.multiple_of(v, nb))
        q_rows = pl.ds(aligned(n * nb), nb)
        k_rows = pl.ds(aligned((n - 1) * nb), 2 * nb)
        return q_rows, (q0[q_rows, :], k0[k_rows, :], v0[k_rows, :], b0[0], b0[1])

    def blocks0(units):
        for (q_rows, _), (m, l, o) in zip(units, _two_head_partials([u for _, u in units], is_a)):
            finish(q_rows, m, l, o)

    first = (pl.ds(0, nb), (q0[0:nb, :], k0[0:nb, :], v0[0:nb, :], b0[0, :, nb:], b0[1, :, nb:]))
    blocks0([first] + [unit0(n) for n in range(1, unroll)])

    def group0(i, carry):
        blocks0([unit0(i * unroll + u) for u in range(unroll)])
        return carry

    lax.fori_loop(1, n_blk0 // unroll, group0, 0)


def _dilated_attention(qkv, tables):
    bsz = qkv[0].shape[0]
    seq = qkv[0].shape[1] * qkv[0].shape[2]
    nt = N_HEAD_TILES
    in_specs, args = [], []
    for g, (_, dil) in enumerate(DIL_GROUPS):
        sub = seq // dil
        for part in range(3):
            if dil == 1:
                spec = pl.BlockSpec((None, None, sub, LANES), lambda t, b, part=part: (b, 0, 0, part * nt + t))
            else:
                spec = pl.BlockSpec((None, dil, sub, LANES), lambda t, b, part=part: (b, 0, 0, part * nt + t))
            in_specs.append(spec)
            args.append(qkv[g])
    for g in range(3):
        width = tables[g].shape[2]
        in_specs.append(pl.BlockSpec((HEADS_PER_TILE, DIL_BACK, width), lambda t, b: (t, 0, 0)))
        args.append(tables[g])
    return pl.pallas_call(
        _dilated_attn_kernel,
        grid=(nt, bsz),
        in_specs=in_specs,
        out_specs=pl.BlockSpec((None, seq, LANES), lambda t, b: (b, 0, t)),
        out_shape=jax.ShapeDtypeStruct((bsz, seq, ATTN_WIDTH), BF16),
        scratch_shapes=[pltpu.VMEM((seq, LANES), F32)] * 3
        + [pltpu.VMEM((3, DIL_GROUPS[2][1] * (DIL_BACK + 8), LANES), F32)],
        compiler_params=_params(("parallel", "parallel")),
        name="dilated_attn",
    )(*args)


def _moba_kernel(q_ref, k_ref, v_ref, km_ref, bias_ref, o_ref, kaug_s, nsel_s):
    seq = q_ref.shape[0]
    blk = MOBA_BLOCK
    n_blk = seq // blk
    is_a_row = lax.broadcasted_iota(I32, (n_blk, LANES), 1) < HEAD_DIM
    is_a = lax.broadcasted_iota(I32, (blk, LANES), 1) < HEAD_DIM

    kaug_s[:, :LANES] = k_ref[...]
    key_blk = lax.broadcasted_iota(I32, (seq, LANES), 0) // blk
    col = lax.broadcasted_iota(I32, (seq, LANES), 1)
    kaug_s[:, LANES:] = jnp.where(col == key_blk, NEG_INF, 0.0).astype(BF16)

    q_all = q_ref[...]
    km = km_ref[...]
    q_blk = lax.broadcasted_iota(I32, (n_blk, seq), 1) // blk
    m_idx = lax.broadcasted_iota(I32, (n_blk, seq), 0)
    past = m_idx < q_blk
    n_sel = min(MOBA_TOPK, n_blk - 1)
    pad_rows = jnp.zeros((LANES - n_blk, LANES), F32)
    for hd in range(HEADS_PER_TILE):
        kmh = jnp.where(is_a_row if hd == 0 else jnp.logical_not(is_a_row), km, 0.0)
        gate = sum(_dot_nt(piece, q_all) for piece in _split3(kmh))
        gate = jnp.where(past, gate, NEG_INF)
        rank = jnp.zeros((n_blk, seq), F32)
        for mp in range(n_blk - 1):
            row = gate[mp:mp + 1, :]
            beats = jnp.where(row > gate, 1.0, jnp.where(row == gate, jnp.where(mp < m_idx, 1.0, 0.0), 0.0))
            rank = rank + jnp.where(mp < q_blk, beats, 0.0)
        unselected = jnp.where(past, jnp.where(rank < n_sel, 0.0, 1.0), 1.0)
        unselected = jnp.where(m_idx == q_blk, 0.0, unselected)
        for c in range(seq // LANES):
            tile = jnp.concatenate([unselected[:, c * LANES:(c + 1) * LANES], pad_rows], axis=0)
            nsel_s[hd, c * LANES:(c + 1) * LANES, :] = tile.T.astype(BF16)

    def block_scores(n):
        rows = slice(n * blk, (n + 1) * blk)
        qn = q_ref[rows, :]
        zero = jnp.zeros_like(qn)
        scores = []
        for hd in range(HEADS_PER_TILE):
            qh = jnp.where(is_a, qn, zero) if hd == 0 else jnp.where(is_a, zero, qn)
            qa = jnp.concatenate([qh, nsel_s[hd, rows, :]], axis=1)
            scores.append(_dot_nt(qa, kaug_s[0:(n + 1) * blk, :]) + bias_ref[hd, :, (n_blk - 1 - n) * blk:])
        return scores

    def block_finish(n, scores):
        probs = [jnp.exp2(s - jnp.max(s, axis=-1, keepdims=True)).astype(BF16) for s in scores]
        pv_a, pv_b = _two_head_pv(probs[0], probs[1], v_ref[0:(n + 1) * blk, :])
        denom = pltpu.roll(jnp.where(is_a, pv_b, pv_a), HEAD_DIM, axis=1)
        o_ref[n * blk:(n + 1) * blk, :] = (jnp.where(is_a, pv_a, pv_b) / denom).astype(o_ref.dtype)

    for n in range(n_blk):
        block_finish(n, block_scores(n))


def _moba_attention(qkv, kmean, table):
    bsz, seq, _ = qkv.shape
    nt = N_HEAD_TILES
    n_blk = seq // MOBA_BLOCK
    return pl.pallas_call(
        _moba_kernel,
        grid=(nt, bsz),
        in_specs=[pl.BlockSpec((None, seq, LANES), lambda t, b: (b, 0, t)),
                  pl.BlockSpec((None, seq, LANES), lambda t, b: (b, 0, nt + t)),
                  pl.BlockSpec((None, seq, LANES), lambda t, b: (b, 0, 2 * nt + t)),
                  pl.BlockSpec((None, n_blk, LANES), lambda t, b: (b, 0, t)),
                  pl.BlockSpec((HEADS_PER_TILE, MOBA_BLOCK, seq), lambda t, b: (t, 0, 0))],
        out_specs=pl.BlockSpec((None, seq, LANES), lambda t, b: (b, 0, t)),
        out_shape=jax.ShapeDtypeStruct((bsz, seq, ATTN_WIDTH), BF16),
        scratch_shapes=[pltpu.VMEM((seq, 2 * LANES), BF16),
                        pltpu.VMEM((HEADS_PER_TILE, seq, LANES), BF16)],
        compiler_params=_params(("parallel", "parallel")),
        name="moba_attn",
    )(qkv, qkv, qkv, kmean, table)


def _attn_out_ffn_kernel(x_ref, a_ref, wo_ref, wg_ref, wu_ref, wd_ref, ln_ref, o_ref, *, n_split, n_part):
    rows = x_ref.shape[0] // n_part
    parts = [slice(p * rows, (p + 1) * rows) for p in range(n_part)]
    width = wg_ref.shape[1] // n_split
    proj = [_dot(a_ref[r, :], wo_ref[...]) for r in parts]
    for r, t in zip(parts, proj):
        h = _layernorm(ALPHA * x_ref[r, :] + t, ln_ref[0:1, :], ln_ref[1:2, :])
        hb = h.astype(BF16)
        y = None
        for c in range(n_split):
            cols = slice(c * width, (c + 1) * width)
            g = _dot(hb, wg_ref[:, cols])
            u = _dot(hb, wu_ref[:, cols])
            part = _dot((g * jax.nn.sigmoid(g) * u).astype(BF16), wd_ref[cols, :])
            y = part if y is None else y + part
        o_ref[r, :] = _layernorm(ALPHA * h + y, ln_ref[2:3, :], ln_ref[3:4, :])


def _attn_out_ffn(x, attn, w_out, w_gate, w_up, w_down, ln):
    t, dm = x.shape
    ff = w_gate.shape[1]
    tm = FFN_TILE
    n_split = ff // FFN_SPLIT
    return pl.pallas_call(
        functools.partial(_attn_out_ffn_kernel, n_split=n_split, n_part=2),
        grid=(t // tm,),
        in_specs=[pl.BlockSpec((tm, dm), lambda i: (i, 0)),
                  pl.BlockSpec((tm, attn.shape[1]), lambda i: (i, 0)),
                  _resident(w_out.shape), _resident(w_gate.shape), _resident(w_up.shape),
                  _resident(w_down.shape), _resident(ln.shape)],
        out_specs=pl.BlockSpec((tm, dm), lambda i: (i, 0)),
        out_shape=jax.ShapeDtypeStruct((t, dm), F32),
        compiler_params=_params(("parallel",)),
        name="attn_out_ffn",
    )(x, attn, w_out, w_gate, w_up, w_down, ln)


def _route(h, wr_ref, tri_ref, cnt_s, rec_ref, cnt_ref, cols):
    tm = h.shape[0]
    h1, h2, _ = _split3(h)
    logits = _dot_nt(wr_ref[0], h1) + _dot_nt(wr_ref[1], h1) + _dot_nt(wr_ref[0], h2)
    e_idx = lax.broadcasted_iota(I32, (N_EXPERTS, tm), 0).astype(F32)
    top1 = jnp.max(logits, axis=0, keepdims=True)
    idx1 = jnp.min(jnp.where(logits == top1, e_idx, float(N_EXPERTS)), axis=0, keepdims=True)
    rest = jnp.where(e_idx == idx1, -jnp.inf, logits)
    top2 = jnp.max(rest, axis=0, keepdims=True)
    idx2 = jnp.min(jnp.where(rest == top2, e_idx, float(N_EXPERTS)), axis=0, keepdims=True)
    z = jnp.exp(top2 - top1)
    gate1 = 1.0 / (1.0 + z)
    gate2 = z / (1.0 + z)
    hot1 = jnp.where(e_idx == idx1, 1.0, 0.0)
    hot2 = jnp.where(e_idx == idx2, 1.0, 0.0)
    hot = hot1 + hot2
    before = _dot(hot.astype(BF16), tri_ref[...]) + cnt_s[:, 0:1]
    pos1 = jnp.sum(hot1 * before, axis=0, keepdims=True)
    pos2 = jnp.sum(hot2 * before, axis=0, keepdims=True)
    cnt_s[...] = cnt_s[...] + jnp.sum(hot, axis=1, keepdims=True)
    cnt_ref[...] = cnt_s[...]
    rec_ref[:, cols] = jnp.concatenate([gate1, gate2, idx1, idx2, pos1, pos2, jnp.zeros((2, tm), F32)], axis=0)


def _conv_mixer_kernel(x_ref, wi_ref, cw_ref, wo_ref, ln_ref, wr_ref, tri_ref,
                       o_ref, rec_ref, cnt_ref, y_s, cnt_s):
    tm, dm = x_ref.shape
    first = (pl.program_id(0) == 0) & (pl.program_id(1) == 0)

    @pl.when(first)
    def _():
        cnt_s[...] = jnp.zeros_like(cnt_s)

    @pl.when(pl.program_id(1) == 0)
    def _():
        y_s[0:8, :] = jnp.zeros((8, dm), F32)

    rows = tri_ref.shape[0]
    parts = [slice(p * rows, (p + 1) * rows) for p in range(tm // rows)]
    xbs = [x_ref[r, :].astype(BF16) for r in parts]
    for r, xb in zip(parts, xbs):
        y_s[8 + r.start:8 + r.stop, :] = _dot(xb, wi_ref[:, dm:2 * dm]) * _dot(xb, wi_ref[:, 2 * dm:])
    gate_bs = [_dot(xb, wi_ref[:, :dm]) for xb in xbs]
    mixed = []
    for r, gate_b in zip(parts, gate_bs):
        z = (cw_ref[0:1, :] * y_s[6 + r.start:6 + r.stop, :] + cw_ref[1:2, :] * y_s[7 + r.start:7 + r.stop, :]
             + cw_ref[2:3, :] * y_s[8 + r.start:8 + r.stop, :])
        mixed.append(_dot((gate_b * z).astype(BF16), wo_ref[...]))
    y_s[0:8, :] = y_s[tm:, :]
    hs = [_layernorm(ALPHA * x_ref[r, :] + m, ln_ref[0:1, :], ln_ref[1:2, :]) for r, m in zip(parts, mixed)]
    for r, h in zip(parts, hs):
        o_ref[r, :] = h
        _route(h, wr_ref, tri_ref, cnt_s, rec_ref, cnt_ref, r)


def _conv_mixer(x, w_in, conv_w, w_out, ln, wr, tri):
    bsz, seq, dm = x.shape
    tm = MIXER_TILE
    nj = seq // tm
    return pl.pallas_call(
        _conv_mixer_kernel,
        grid=(bsz, nj),
        in_specs=[pl.BlockSpec((None, tm, dm), lambda b, j: (b, j, 0)),
                  _resident(w_in.shape), _resident(conv_w.shape), _resident(w_out.shape),
                  _resident(ln.shape), _resident(wr.shape), _resident(tri.shape)],
        out_specs=[pl.BlockSpec((None, tm, dm), lambda b, j: (b, j, 0)),
                   pl.BlockSpec((8, tm), lambda b, j: (0, b * nj + j)),
                   pl.BlockSpec((N_EXPERTS, LANES), lambda b, j: (0, 0))],
        out_shape=[jax.ShapeDtypeStruct((bsz, seq, dm), F32),
                   jax.ShapeDtypeStruct((8, bsz * seq), F32),
                   jax.ShapeDtypeStruct((N_EXPERTS, LANES), F32)],
        scratch_shapes=[pltpu.VMEM((tm + 8, dm), F32), pltpu.VMEM((N_EXPERTS, LANES), F32)],
        compiler_params=_params(("arbitrary", "arbitrary")),
        name="conv_mixer",
    )(x, w_in, conv_w, w_out, ln, wr, tri)


def _gmlp_mixer_kernel(x_ref, wi_ref, dln_ref, ws_ref, bs_ref, wo_ref, ln_ref, wr_ref, tri_ref,
                       o_ref, rec_ref, cnt_ref, mix_s, cnt_s):
    tm, dm = x_ref.shape
    ch = GMLP_CHUNK
    gw = dm // GMLP_GROUPS

    @pl.when(pl.program_id(0) == 0)
    def _():
        cnt_s[...] = jnp.zeros_like(cnt_s)

    rows = tri_ref.shape[0]
    parts = [slice(p * rows, (p + 1) * rows) for p in range(tm // rows)]
    lower = lax.broadcasted_iota(I32, (ch, ch), 0) >= lax.broadcasted_iota(I32, (ch, ch), 1)
    ws = [jnp.where(lower, ws_ref[g], 0.0).astype(BF16) for g in range(GMLP_GROUPS)]
    pre = []
    for r in parts:
        xb = x_ref[r, :].astype(BF16)
        pre.append((_dot(xb, wi_ref[:, :dm]), _dot(xb, wi_ref[:, dm:])))
    mixed = []
    for r, (u_pre, v_pre) in zip(parts, pre):
        vb = _layernorm(_gelu(v_pre), dln_ref[0:1, :], dln_ref[1:2, :]).astype(BF16)
        for g in range(GMLP_GROUPS):
            for c in range(rows // ch):
                blk = _dot(ws[g], vb[c * ch:(c + 1) * ch, g * gw:(g + 1) * gw]) + bs_ref[:, g:g + 1]
                mix_s[r.start + c * ch:r.start + (c + 1) * ch, g * gw:(g + 1) * gw] = blk
        mixed.append(_dot((_gelu(u_pre) * mix_s[r, :]).astype(BF16), wo_ref[...]))
    hs = [_layernorm(ALPHA * x_ref[r, :] + m, ln_ref[0:1, :], ln_ref[1:2, :]) for r, m in zip(parts, mixed)]
    for r, h in zip(parts, hs):
        o_ref[r, :] = h
        _route(h, wr_ref, tri_ref, cnt_s, rec_ref, cnt_ref, r)


def _gmlp_mixer(x, w_in, dln, w_s, b_s_t, w_out, ln, wr, tri):
    t, dm = x.shape
    tm = MIXER_TILE
    return pl.pallas_call(
        _gmlp_mixer_kernel,
        grid=(t // tm,),
        in_specs=[pl.BlockSpec((tm, dm), lambda i: (i, 0)),
                  _resident(w_in.shape), _resident(dln.shape), _resident(w_s.shape),
                  _resident(b_s_t.shape), _resident(w_out.shape), _resident(ln.shape),
                  _resident(wr.shape), _resident(tri.shape)],
        out_specs=[pl.BlockSpec((tm, dm), lambda i: (i, 0)),
                   pl.BlockSpec((8, tm), lambda i: (0, i)),
                   pl.BlockSpec((N_EXPERTS, LANES), lambda i: (0, 0))],
        out_shape=[jax.ShapeDtypeStruct((t, dm), F32),
                   jax.ShapeDtypeStruct((8, t), F32),
                   jax.ShapeDtypeStruct((N_EXPERTS, LANES), F32)],
        scratch_shapes=[pltpu.VMEM((tm, dm), F32), pltpu.VMEM((N_EXPERTS, LANES), F32)],
        compiler_params=_params(("arbitrary",)),
        name="gmlp_mixer",
    )(x, w_in, dln, w_s, b_s_t, w_out, ln, wr, tri)


def _load_slots(slot_hbm, slot_s, sem, i, tm):
    half = lax.rem(i, 2)

    def copies(step, into):
        return [pltpu.make_async_copy(slot_hbm.at[c, step], slot_s[c].at[pl.ds(pl.multiple_of(into * tm, tm), tm)],
                                      sem.at[c, into]) for c in range(2)]

    @pl.when(i == 0)
    def _():
        for cp in copies(0, 0):
            cp.start()

    @pl.when(i + 1 < pl.num_programs(0))
    def _():
        for cp in copies(i + 1, 1 - half):
            cp.start()

    for cp in copies(i, half):
        cp.wait()
    return half * tm


def _start_row_copies(n_rows, row_copy, slot_s, base):
    for r in range(n_rows):
        for c in range(2):
            row_copy(r, c, slot_s[c][base + r]).start(priority=(r + c) % 2)


def _moe_dispatch_kernel(fs_ref, fc_ref, slot_hbm, h_ref, x_hbm, slot0_s, slot1_s, zero_s, idx_sem, row_sem):
    i = pl.program_id(0)
    tm = h_ref.shape[0]
    slot_s = (slot0_s, slot1_s)
    base = _load_slots(slot_hbm, slot_s, idx_sem, i, tm)

    def row_copy(r, c, slot):
        return pltpu.make_async_copy(h_ref.at[pl.ds(r, 1), :], x_hbm.at[pl.ds(slot, 1), :], row_sem.at[c])

    _start_row_copies(tm, row_copy, slot_s, base)

    @pl.when(i == 0)
    def _():
        zero_s[...] = jnp.zeros_like(zero_s)

        def fill_copy(row):
            return pltpu.make_async_copy(zero_s.at[pl.ds(0, 1), :], x_hbm.at[pl.ds(row, 1), :], row_sem.at[2])

        for e in range(N_EXPERTS):
            def fill(j, carry, e=e):
                fill_copy(fs_ref[e] + j).start()
                return carry

            lax.fori_loop(0, fc_ref[e], fill, 0)

            def drain(j, carry):
                fill_copy(0).wait()
                return carry

            lax.fori_loop(0, fc_ref[e], drain, 0)

    for c in range(2):
        pltpu.make_async_copy(h_ref, x_hbm.at[pl.ds(0, tm), :], row_sem.at[c]).wait()


def _moe_dispatch(h, slots, fill_start, fill_count, n_slots):
    t, dm = h.shape
    tm = slots.shape[2]
    grid_spec = pltpu.PrefetchScalarGridSpec(
        num_scalar_prefetch=2,
        grid=(t // tm,),
        in_specs=[pl.BlockSpec(memory_space=pl.ANY),
                  pl.BlockSpec((tm, dm), lambda i, fs, fc: (i, 0))],
        out_specs=pl.BlockSpec(memory_space=pl.ANY),
        scratch_shapes=[pltpu.SMEM((2 * tm,), I32), pltpu.SMEM((2 * tm,), I32), pltpu.VMEM((8, dm), F32),
                        pltpu.SemaphoreType.DMA((2, 2)), pltpu.SemaphoreType.DMA((3,))],
    )
    return pl.pallas_call(
        _moe_dispatch_kernel,
        grid_spec=grid_spec,
        out_shape=jax.ShapeDtypeStruct((n_slots, dm), F32),
        compiler_params=_params(("arbitrary",)),
        name="moe_dispatch",
    )(fill_start, fill_count, slots, h)


def _moe_expert_kernel(te_ref, nu_ref, x_ref, wg_ref, wu_ref, wd_ref, y_ref, xb_s):
    i, f = pl.program_id(0), pl.program_id(1)

    @pl.when(i < nu_ref[0])
    def _():
        @pl.when(f == 0)
        def _():
            xb_s[...] = x_ref[...].astype(BF16)
            y_ref[...] = jnp.zeros_like(y_ref)

        xb = xb_s[...]
        g = _dot(xb, wg_ref[...])
        u = _dot(xb, wu_ref[...])
        y_ref[...] += _dot((g * jax.nn.sigmoid(g) * u).astype(BF16), wd_ref[...])


def _moe_experts(x_sorted, tile_expert, n_used, w_gate, w_up, w_down):
    n_slots, dm = x_sorted.shape
    tm = MOE_TILE
    ff = w_gate.shape[2]
    tf = MOE_FF_TILE

    def tile_block(i, f, te, nu):
        return jnp.minimum(i, nu[0] - 1), 0

    def col_block(i, f, te, nu):
        return te[i], 0, jnp.where(i < nu[0], f, ff // tf - 1)

    def row_block(i, f, te, nu):
        return te[i], jnp.where(i < nu[0], f, ff // tf - 1), 0

    grid_spec = pltpu.PrefetchScalarGridSpec(
        num_scalar_prefetch=2,
        grid=(n_slots // tm, ff // tf),
        in_specs=[pl.BlockSpec((tm, dm), tile_block),
                  pl.BlockSpec((None, dm, tf), col_block),
                  pl.BlockSpec((None, dm, tf), col_block),
                  pl.BlockSpec((None, tf, dm), row_block)],
        out_specs=pl.BlockSpec((tm, dm), tile_block),
        scratch_shapes=[pltpu.VMEM((tm, dm), BF16)],
    )
    return pl.pallas_call(
        _moe_expert_kernel,
        grid_spec=grid_spec,
        out_shape=jax.ShapeDtypeStruct((n_slots, dm), F32),
        compiler_params=_params(("arbitrary", "arbitrary")),
        name="moe_experts",
    )(tile_expert, n_used, x_sorted, w_gate, w_up, w_down)


def _moe_combine_kernel(slot_hbm, y_hbm, h_ref, rec_ref, ln_ref, o_ref, slot0_s, slot1_s, y_s, idx_sem, row_sem):
    i = pl.program_id(0)
    tm = h_ref.shape[0]
    slot_s = (slot0_s, slot1_s)
    base = _load_slots(slot_hbm, slot_s, idx_sem, i, tm)

    def row_copy(r, c, slot):
        return pltpu.make_async_copy(y_hbm.at[pl.ds(slot, 1), :], y_s.at[c, pl.ds(r, 1), :], row_sem.at[c])

    _start_row_copies(tm, row_copy, slot_s, base)
    for c in range(2):
        pltpu.make_async_copy(y_hbm.at[pl.ds(0, tm), :], y_s.at[c], row_sem.at[c]).wait()

    pad_rows = jnp.zeros((LANES - 8, LANES), F32)
    for c in range(tm // LANES):
        rows = slice(c * LANES, (c + 1) * LANES)
        gates = jnp.concatenate([rec_ref[:, rows], pad_rows], axis=0).T
        y = gates[:, 0:1] * y_s[0, rows, :] + gates[:, 1:2] * y_s[1, rows, :]
        o_ref[rows, :] = _layernorm(ALPHA * h_ref[rows, :] + y, ln_ref[0:1, :], ln_ref[1:2, :])


def _moe_combine(h, y_sorted, slots, rec, ln):
    t, dm = h.shape
    tm = slots.shape[2]
    return pl.pallas_call(
        _moe_combine_kernel,
        grid=(t // tm,),
        in_specs=[pl.BlockSpec(memory_space=pl.ANY),
                  pl.BlockSpec(memory_space=pl.ANY),
                  pl.BlockSpec((tm, dm), lambda i: (i, 0)),
                  pl.BlockSpec((8, tm), lambda i: (0, i)),
                  _resident(ln.shape)],
        out_specs=pl.BlockSpec((tm, dm), lambda i: (i, 0)),
        out_shape=jax.ShapeDtypeStruct((t, dm), F32),
        scratch_shapes=[pltpu.SMEM((2 * tm,), I32), pltpu.SMEM((2 * tm,), I32), pltpu.VMEM((2, tm, dm), F32),
                        pltpu.SemaphoreType.DMA((2, 2)), pltpu.SemaphoreType.DMA((2,))],
        compiler_params=_params(("arbitrary",)),
        name="moe_combine",
    )(slots, y_sorted, h, rec, ln)


def _moe(h, rec, counts, w_gate, w_up, w_down, ln):
    t, dm = h.shape
    tm = MOE_TILE
    n_tiles = (2 * t) // tm + N_EXPERTS
    cnt = counts[:, 0].astype(I32)
    padded = (cnt + tm - 1) // tm * tm
    pad_end = jnp.cumsum(padded)
    pad_start = pad_end - padded
    experts = jnp.arange(N_EXPERTS, dtype=I32)

    def slot_of(e_row, pos_row):
        e = e_row.astype(I32)
        start = jnp.sum(jnp.where(e[None, :] == experts[:, None], pad_start[:, None], 0), axis=0)
        return start + pos_row.astype(I32)

    slots = jnp.stack([slot_of(rec[2], rec[4]), slot_of(rec[3], rec[5])]).reshape(2, t // MOE_ROW_TILE, MOE_ROW_TILE)
    n_used = pad_end[-1] // tm
    tile_start = jnp.minimum(jnp.arange(n_tiles, dtype=I32), n_used - 1) * tm
    tile_expert = jnp.sum(tile_start[:, None] >= pad_end[None, :], axis=1).astype(I32)
    x_sorted = _moe_dispatch(h, slots, pad_start + cnt, padded - cnt, n_tiles * tm)
    y_sorted = _moe_experts(x_sorted, tile_expert, n_used.astype(I32).reshape(1), w_gate, w_up, w_down)
    return _moe_combine(h, y_sorted, slots, rec, ln)


def kernel(x, rel_bias, ln_gain, ln_bias, a_w_in, a_w_out, b_w_in, b_conv, b_w_out, c_w_in, c_w_out,
           d_w_in, d_ln_gain, d_ln_bias, d_w_s, d_b_s, d_w_out, f_w_gate, f_w_up, f_w_down,
           e_w_router, e_w_gate, e_w_up, e_w_down):
    bsz, seq, dm = x.shape
    t = bsz * seq
    assert seq % (8 * MOBA_BLOCK) == 0 and seq == DIL_GROUPS[-1][0] and ln_gain.shape[0] == DEPTH
    assert seq % PROJ_TILE == 0 and seq % MIXER_TILE == 0 and t % MOE_TILE == 0
    q_scale = HEAD_DIM ** -0.5 * LOG2_E

    def ln_rows(i, *pairs):
        return jnp.concatenate([jnp.stack([ln_gain[i, p], ln_bias[i, p]]) for p in pairs], axis=0)

    def router_weights(w):
        w1, w2, _ = _split3(w.T.astype(F32))
        return jnp.stack([w1, w2])

    tri = jnp.triu(jnp.ones((TOKEN_TILE, TOKEN_TILE), BF16), k=1)

    def scaled_qkv(w):
        scale = jnp.concatenate([jnp.full((ATTN_WIDTH,), q_scale, F32), jnp.ones((2 * ATTN_WIDTH,), F32)])
        return (w * jnp.tile(scale, w.shape[1] // scale.shape[0])).astype(BF16)

    h = x
    a_w = scaled_qkv(a_w_in[0])
    qkv = [_project_dilated(h, a_w, g, dil) for g, (_, dil) in enumerate(DIL_GROUPS)]
    tables = [_dilated_bias_table(rel_bias, dil) for _, dil in DIL_GROUPS]
    tables[2] = tables[2][:, :, DIL_BACK:]
    attn = _dilated_attention(qkv, tables)
    h = _attn_out_ffn(h.reshape(t, dm), attn.reshape(t, ATTN_WIDTH), a_w_out[0].astype(BF16),
                      f_w_gate[0].astype(BF16), f_w_up[0].astype(BF16), f_w_down[0].astype(BF16),
                      ln_rows(0, 0, 1))
    h, rec, counts = _conv_mixer(h.reshape(bsz, seq, dm), b_w_in[0].astype(BF16), b_conv[0],
                                 b_w_out[0].astype(BF16), ln_rows(1, 0), router_weights(e_w_router[0]), tri)
    h = _moe(h.reshape(t, dm), rec, counts, _expert_weights_bf16(e_w_gate, 0), _expert_weights_bf16(e_w_up, 0),
             _expert_weights_bf16(e_w_down, 0), ln_rows(1, 1))
    qkv_c, kmean = _project_moba(h.reshape(bsz, seq, dm), scaled_qkv(c_w_in[0]))
    attn = _moba_attention(qkv_c, kmean, _moba_bias_table(rel_bias, seq // MOBA_BLOCK))
    h = _attn_out_ffn(h, attn.reshape(t, ATTN_WIDTH), c_w_out[0].astype(BF16),
                      f_w_gate[1].astype(BF16), f_w_up[1].astype(BF16), f_w_down[1].astype(BF16),
                      ln_rows(2, 0, 1))
    h, rec, counts = _gmlp_mixer(h, d_w_in[0].astype(BF16), jnp.stack([d_ln_gain[0], d_ln_bias[0]]),
                                 d_w_s[0], d_b_s[0].T, d_w_out[0].astype(BF16), ln_rows(3, 0),
                                 router_weights(e_w_router[1]), tri)
    h = _moe(h, rec, counts, _expert_weights_bf16(e_w_gate, 1), _expert_weights_bf16(e_w_up, 1),
             _expert_weights_bf16(e_w_down, 1), ln_rows(3, 1))
    return h.reshape(bsz, seq, dm)
```

```python
import functools
import math

import jax
import jax.numpy as jnp
from jax import lax
from jax.experimental import pallas as pl
from jax.experimental.pallas import tpu as pltpu

F32 = jnp.float32
BF16 = jnp.bfloat16
I32 = jnp.int32

N_HEADS = 16
HEAD_DIM = 64
LANES = 128
HEADS_PER_TILE = LANES // HEAD_DIM
N_HEAD_TILES = N_HEADS // HEADS_PER_TILE
ATTN_WIDTH = N_HEADS * HEAD_DIM
NUM_BUCKETS = 32
MAX_DISTANCE = 2048
DIL_GROUPS = ((128, 1), (512, 4), (2048, 16))
DIL_BACK = 128
DIL_UNROLL = 16
MOBA_BLOCK = 256
MOBA_TOPK = 3
GMLP_CHUNK = 128
GMLP_GROUPS = 8
N_EXPERTS = 8
DEPTH = 4
ALPHA = (2 * DEPTH) ** 0.25
LN_EPS = 1e-5
NEG_INF = -1e30
LOG2_E = math.log2(math.e)

TOKEN_TILE = 512
PROJ_TILE = 1024
MIXER_TILE = 1024
FFN_TILE = 1024
FFN_SPLIT = 256
MOE_TILE = 512
MOE_FF_TILE = 1792
MOE_ROW_TILE = 1024
VMEM_LIMIT = 56 * 1024 * 1024
CAST_BLOCK_BYTES = 8 * 1024 * 1024


def _params(semantics, **kw):
    return pltpu.CompilerParams(dimension_semantics=semantics, vmem_limit_bytes=VMEM_LIMIT, **kw)


def _resident(shape):
    return pl.BlockSpec(shape, lambda *_: (0,) * len(shape), pipeline_mode=pl.Buffered(1))


def _layernorm(z, gain, bias):
    mu = jnp.mean(z, axis=-1, keepdims=True)
    zc = z - mu
    var = jnp.mean(zc * zc, axis=-1, keepdims=True)
    return zc * lax.rsqrt(var + LN_EPS) * gain + bias


def _gelu(x):
    return 0.5 * x * (1.0 + lax.erf(x * math.sqrt(0.5)))


def _dot(a, b):
    return jnp.dot(a, b, preferred_element_type=F32)


def _dot_nt(a, b):
    return lax.dot_general(a, b, (((1,), (1,)), ((), ())), preferred_element_type=F32)


def _split3(x):
    x1 = x.astype(BF16)
    r1 = x - x1.astype(F32)
    x2 = r1.astype(BF16)
    x3 = (r1 - x2.astype(F32)).astype(BF16)
    return x1, x2, x3


def _cast_kernel(w_ref, o_ref):
    o_ref[...] = w_ref[...].astype(o_ref.dtype)


def _expert_weights_bf16(w, layer):
    _, n_e, k, n = w.shape
    bk = max(8, min(k, CAST_BLOCK_BYTES // (4 * n) // 8 * 8))
    while k % bk:
        bk -= 8
    return pl.pallas_call(
        _cast_kernel,
        grid=(n_e, k // bk),
        in_specs=[pl.BlockSpec((None, None, bk, n), lambda e, j: (layer, e, j, 0))],
        out_specs=pl.BlockSpec((None, bk, n), lambda e, j: (e, j, 0)),
        out_shape=jax.ShapeDtypeStruct((n_e, k, n), BF16),
        compiler_params=_params(("parallel", "parallel")),
        name="cast_expert_weights",
    )(w)


def _t5_bucket(dist):
    max_exact = NUM_BUCKETS // 2
    d = jnp.maximum(dist, 0)
    log_ratio = jnp.log(jnp.maximum(d, 1).astype(F32) / max_exact) / math.log(MAX_DISTANCE / max_exact)
    large = max_exact + (log_ratio * (NUM_BUCKETS - max_exact)).astype(I32)
    large = jnp.clip(large, max_exact, NUM_BUCKETS - 1)
    return jnp.where(d < max_exact, d, large)


def _bias_lookup(rel_bias, dist):
    onehot = (_t5_bucket(dist)[..., None] == jnp.arange(NUM_BUCKETS, dtype=I32)).astype(F32)
    out = jnp.einsum('...k,kh->...h', onehot, rel_bias.astype(F32), precision=lax.Precision.HIGHEST)
    return jnp.moveaxis(out, -1, 0) * LOG2_E


def _dilated_bias_table(rel_bias, dil):
    qi = jnp.arange(DIL_BACK, dtype=I32)[:, None]
    kj = jnp.arange(2 * DIL_BACK, dtype=I32)[None, :]
    delta = qi + DIL_BACK - kj
    band = (delta >= 0) & (delta <= DIL_BACK)
    bias = _bias_lookup(rel_bias, delta * dil)
    return jnp.where(band[None], bias, NEG_INF)


def _moba_bias_table(rel_bias, n_blk):
    pos = jnp.arange(MOBA_BLOCK, dtype=I32)
    diff = pos[:, None] - pos[None, :]
    blocks_back = jnp.arange(n_blk - 1, -1, -1, dtype=I32)
    dist = blocks_back[None, :, None] * MOBA_BLOCK + diff[:, None, :]
    bias = jnp.where((dist >= 0)[None], _bias_lookup(rel_bias, dist), NEG_INF)
    return bias.reshape(N_HEADS, MOBA_BLOCK, n_blk * MOBA_BLOCK)


def _proj_kernel(*refs, dil, n_split):
    x_refs, w_ref, o_ref = refs[:-2], refs[-2], refs[-1]
    tm = x_refs[0].shape[0]
    rows = tm // dil
    if dil == 1:
        xb = jnp.concatenate([x_ref[...] for x_ref in x_refs], axis=1).astype(BF16)
    else:
        xb = jnp.concatenate(
            [jnp.concatenate([x_ref[pl.ds(r, rows, stride=dil), :] for x_ref in x_refs], axis=1)
             for r in range(dil)], axis=0).astype(BF16)
    width = w_ref.shape[1] // n_split
    for c in range(n_split):
        y = _dot(xb, w_ref[:, c * width:(c + 1) * width]).astype(BF16)
        for r in range(dil):
            o_ref[r, :, c * width:(c + 1) * width] = y[r * rows:(r + 1) * rows]


def _project_dilated(x, w, group, dil):
    bsz, seq, dm = x.shape
    n = 3 * ATTN_WIDTH
    tm = PROJ_TILE
    return pl.pallas_call(
        functools.partial(_proj_kernel, dil=dil, n_split=3),
        grid=(bsz, seq // tm),
        in_specs=[pl.BlockSpec((None, tm, LANES), lambda b, j, c=c: (b, j, c)) for c in range(dm // LANES)]
        + [pl.BlockSpec((dm, n), lambda b, j: (0, group), pipeline_mode=pl.Buffered(1))],
        out_specs=pl.BlockSpec((None, dil, tm // dil, n), lambda b, j: (b, 0, j, 0)),
        out_shape=jax.ShapeDtypeStruct((bsz, dil, seq // dil, n), BF16),
        compiler_params=_params(("parallel", "parallel")),
        name=f"proj_dil{dil}",
    )(*([x] * (dm // LANES)), w)


def _proj_kmean_kernel(x_ref, w_ref, o_ref, km_ref):
    tm = x_ref.shape[0]
    per_step = tm // MOBA_BLOCK
    j = pl.program_id(1)
    xb = x_ref[...].astype(BF16)
    width = ATTN_WIDTH
    for c in range(3):
        y = _dot(xb, w_ref[:, c * width:(c + 1) * width])
        o_ref[:, c * width:(c + 1) * width] = y.astype(BF16)
        if c == 1:
            for jj in range(per_step):
                km_ref[pl.ds(j * per_step + jj, 1), :] = jnp.mean(
                    y[jj * MOBA_BLOCK:(jj + 1) * MOBA_BLOCK], axis=0, keepdims=True)


def _project_moba(x, w):
    bsz, seq, dm = x.shape
    n = w.shape[1]
    tm = PROJ_TILE
    n_blk = seq // MOBA_BLOCK
    return pl.pallas_call(
        _proj_kmean_kernel,
        grid=(bsz, seq // tm),
        in_specs=[pl.BlockSpec((None, tm, dm), lambda b, j: (b, j, 0)),
                  _resident((dm, n))],
        out_specs=[pl.BlockSpec((None, tm, n), lambda b, j: (b, j, 0)),
                   pl.BlockSpec((None, n_blk, ATTN_WIDTH), lambda b, j: (b, 0, 0))],
        out_shape=[jax.ShapeDtypeStruct((bsz, seq, n), BF16),
                   jax.ShapeDtypeStruct((bsz, n_blk, ATTN_WIDTH), F32)],
        compiler_params=_params(("parallel", "arbitrary")),
        name="proj_moba",
    )(x, w)


def _two_head_partials(units, is_a):
    scores = []
    for q, kw, _, bias_a, bias_b in units:
        zero = jnp.zeros_like(q)
        scores.append((_dot_nt(jnp.where(is_a, q, zero), kw) + bias_a,
                       _dot_nt(jnp.where(is_a, zero, q), kw) + bias_b))
    stats = []
    for pair in scores:
        row = []
        for s in pair:
            m = jnp.max(s, axis=-1, keepdims=True)
            row.append((m, jnp.exp2(s - m).astype(BF16)))
        stats.append(row)
    out = []
    for (_, _, vw, _, _), ((m_a, p_a), (m_b, p_b)) in zip(units, stats):
        pv_a, pv_b = _two_head_pv(p_a, p_b, vw)
        out.append((jnp.where(is_a, m_a, m_b), pltpu.roll(jnp.where(is_a, pv_b, pv_a), HEAD_DIM, axis=1),
                    jnp.where(is_a, pv_a, pv_b)))
    return out


def _two_head_pv(p_a, p_b, vw):
    own_a = lax.broadcasted_iota(I32, vw.shape, 1) < HEAD_DIM
    one = jnp.ones_like(vw)
    return _dot(p_a, jnp.where(own_a, vw, one)), _dot(p_b, jnp.where(own_a, one, vw))


def _dilated_attn_kernel(q0, k0, v0, q1, k1, v1, q2, k2, v2, b0, b1, b2, o_ref, m_s, l_s, a_s, tmp_s):
    nb = DIL_BACK
    is_a = lax.broadcasted_iota(I32, (nb, LANES), 1) < HEAD_DIM
    n_res2, n_res1 = q2.shape[0], q1.shape[0]
    n_blk1 = q1.shape[1] // nb
    n_blk0 = q0.shape[0] // nb

    def merge(rows, m, l, o):
        m_old, l_old, a_old = m_s[rows, :], l_s[rows, :], a_s[rows, :]
        m_new = jnp.maximum(m_old, m)
        w_old, w_new = jnp.exp2(m_old - m_new), jnp.exp2(m - m_new)
        return m_new, w_old * l_old + w_new * l, w_old * a_old + w_new * o

    unroll = DIL_UNROLL

    pitch = tmp_s.shape[1] // n_res2

    def group2(i, carry):
        res = [i * unroll + u for u in range(unroll)]
        parts = _two_head_partials([(q2[r], k2[r], v2[r], b2[0], b2[1]) for r in res], is_a)
        for r, part in zip(res, parts):
            rows = pl.ds(pl.multiple_of(r * pitch, 8), nb)
            for c in range(3):
                tmp_s[c, rows, :] = part[c]
        return carry

    lax.fori_loop(0, n_res2 // unroll, group2, 0)

    def to_token_major(s, carry):
        dst = pl.ds(pl.multiple_of(s * n_res2, n_res2), n_res2)
        for c, ref in enumerate((m_s, l_s, a_s)):
            ref[dst, :] = tmp_s[c, pl.ds(s, n_res2, stride=pitch), :]
        return carry

    lax.fori_loop(0, nb, to_token_major, 0, unroll=8)

    res_per_body = max(1, unroll // n_blk1)

    def group1(i, carry):
        units, where = [], []
        for rr in range(res_per_body):
            r = i * res_per_body + rr
            units.append((q1[r, 0:nb, :], k1[r, 0:nb, :], v1[r, 0:nb, :], b1[0, :, nb:], b1[1, :, nb:]))
            where.append((r, 0))
            for n in range(1, n_blk1):
                keys = slice((n - 1) * nb, (n + 1) * nb)
                units.append((q1[r, n * nb:(n + 1) * nb, :], k1[r, keys, :], v1[r, keys, :], b1[0], b1[1]))
                where.append((r, n))
        for (r, n), (m, l, o) in zip(where, _two_head_partials(units, is_a)):
            rows = pl.ds(n * nb * n_res1 + r, nb, stride=n_res1)
            m, l, o = merge(rows, m, l, o)
            m_s[rows, :] = m
            l_s[rows, :] = l
            a_s[rows, :] = o
        return carry

    lax.fori_loop(0, n_res1 // res_per_body, group1, 0)

    def finish(rows, m, l, o):
        _, l, o = merge(rows, m, l, o)
        o_ref[rows, :] = (o / l).astype(o_ref.dtype)

    def unit0(n):
        aligned = (lambda v: v) if isinstance(n, int) else (lambda v: pl.multiple_of(v, nb))
        q_rows = pl.ds(aligned(n * nb), nb)
        k_rows = pl.ds(aligned((n - 1) * nb), 2 * nb)
        return q_rows, (q0[q_rows, :], k0[k_rows, :], v0[k_rows, :], b0[0], b0[1])

    def blocks0(units):
        for (q_rows, _), (m, l, o) in zip(units, _two_head_partials([u for _, u in units], is_a)):
            finish(q_rows, m, l, o)

    first = (pl.ds(0, nb), (q0[0:nb, :], k0[0:nb, :], v0[0:nb, :], b0[0, :, nb:], b0[1, :, nb:]))
    blocks0([first] + [unit0(n) for n in range(1, unroll)])

    def group0(i, carry):
        blocks0([unit0(i * unroll + u) for u in range(unroll)])
        return carry

    lax.fori_loop(1, n_blk0 // unroll, group0, 0)


def _dilated_attention(qkv, tables):
    bsz = qkv[0].shape[0]
    seq = qkv[0].shape[1] * qkv[0].shape[2]
    nt = N_HEAD_TILES
    in_specs, args = [], []
    for g, (_, dil) in enumerate(DIL_GROUPS):
        sub = seq // dil
        for part in range(3):
            if dil == 1:
                spec = pl.BlockSpec((None, None, sub, LANES), lambda t, b, part=part: (b, 0, 0, part * nt + t))
            else:
                spec = pl.BlockSpec((None, dil, sub, LANES), lambda t, b, part=part: (b, 0, 0, part * nt + t))
            in_specs.append(spec)
            args.append(qkv[g])
    for g in range(3):
        width = tables[g].shape[2]
        in_specs.append(pl.BlockSpec((HEADS_PER_TILE, DIL_BACK, width), lambda t, b: (t, 0, 0)))
        args.append(tables[g])
    return pl.pallas_call(
        _dilated_attn_kernel,
        grid=(nt, bsz),
        in_specs=in_specs,
        out_specs=pl.BlockSpec((None, seq, LANES), lambda t, b: (b, 0, t)),
        out_shape=jax.ShapeDtypeStruct((bsz, seq, ATTN_WIDTH), BF16),
        scratch_shapes=[pltpu.VMEM((seq, LANES), F32)] * 3
        + [pltpu.VMEM((3, DIL_GROUPS[2][1] * (DIL_BACK + 8), LANES), F32)],
        compiler_params=_params(("parallel", "parallel")),
        name="dilated_attn",
    )(*args)


def _moba_kernel(q_ref, k_ref, v_ref, km_ref, bias_ref, o_ref, kaug_s, nsel_s):
    seq = q_ref.shape[0]
    blk = MOBA_BLOCK
    n_blk = seq // blk
    is_a_row = lax.broadcasted_iota(I32, (n_blk, LANES), 1) < HEAD_DIM
    is_a = lax.broadcasted_iota(I32, (blk, LANES), 1) < HEAD_DIM

    kaug_s[:, :LANES] = k_ref[...]
    key_blk = lax.broadcasted_iota(I32, (seq, LANES), 0) // blk
    col = lax.broadcasted_iota(I32, (seq, LANES), 1)
    kaug_s[:, LANES:] = jnp.where(col == key_blk, NEG_INF, 0.0).astype(BF16)

    q_all = q_ref[...]
    km = km_ref[...]
    q_blk = lax.broadcasted_iota(I32, (n_blk, seq), 1) // blk
    m_idx = lax.broadcasted_iota(I32, (n_blk, seq), 0)
    past = m_idx < q_blk
    n_sel = min(MOBA_TOPK, n_blk - 1)
    pad_rows = jnp.zeros((LANES - n_blk, LANES), F32)
    for hd in range(HEADS_PER_TILE):
        kmh = jnp.where(is_a_row if hd == 0 else jnp.logical_not(is_a_row), km, 0.0)
        gate = sum(_dot_nt(piece, q_all) for piece in _split3(kmh))
        gate = jnp.where(past, gate, NEG_INF)
        rank = jnp.zeros((n_blk, seq), F32)
        for mp in range(n_blk - 1):
            row = gate[mp:mp + 1, :]
            beats = jnp.where(row > gate, 1.0, jnp.where(row == gate, jnp.where(mp < m_idx, 1.0, 0.0), 0.0))
            rank = rank + jnp.where(mp < q_blk, beats, 0.0)
        unselected = jnp.where(past, jnp.where(rank < n_sel, 0.0, 1.0), 1.0)
        unselected = jnp.where(m_idx == q_blk, 0.0, unselected)
        for c in range(seq // LANES):
            tile = jnp.concatenate([unselected[:, c * LANES:(c + 1) * LANES], pad_rows], axis=0)
            nsel_s[hd, c * LANES:(c + 1) * LANES, :] = tile.T.astype(BF16)

    def block_scores(n):
        rows = slice(n * blk, (n + 1) * blk)
        qn = q_ref[rows, :]
        zero = jnp.zeros_like(qn)
        scores = []
        for hd in range(HEADS_PER_TILE):
            qh = jnp.where(is_a, qn, zero) if hd == 0 else jnp.where(is_a, zero, qn)
            qa = jnp.concatenate([qh, nsel_s[hd, rows, :]], axis=1)
            scores.append(_dot_nt(qa, kaug_s[0:(n + 1) * blk, :]) + bias_ref[hd, :, (n_blk - 1 - n) * blk:])
        return scores

    def block_finish(n, scores):
        probs = [jnp.exp2(s - jnp.max(s, axis=-1, keepdims=True)).astype(BF16) for s in scores]
        pv_a, pv_b = _two_head_pv(probs[0], probs[1], v_ref[0:(n + 1) * blk, :])
        denom = pltpu.roll(jnp.where(is_a, pv_b, pv_a), HEAD_DIM, axis=1)
        o_ref[n * blk:(n + 1) * blk, :] = (jnp.where(is_a, pv_a, pv_b) / denom).astype(o_ref.dtype)

    for n in range(n_blk):
        block_finish(n, block_scores(n))


def _moba_attention(qkv, kmean, table):
    bsz, seq, _ = qkv.shape
    nt = N_HEAD_TILES
    n_blk = seq // MOBA_BLOCK
    return pl.pallas_call(
        _moba_kernel,
        grid=(nt, bsz),
        in_specs=[pl.BlockSpec((None, seq, LANES), lambda t, b: (b, 0, t)),
                  pl.BlockSpec((None, seq, LANES), lambda t, b: (b, 0, nt + t)),
                  pl.BlockSpec((None, seq, LANES), lambda t, b: (b, 0, 2 * nt + t)),
                  pl.BlockSpec((None, n_blk, LANES), lambda t, b: (b, 0, t)),
                  pl.BlockSpec((HEADS_PER_TILE, MOBA_BLOCK, seq), lambda t, b: (t, 0, 0))],
        out_specs=pl.BlockSpec((None, seq, LANES), lambda t, b: (b, 0, t)),
        out_shape=jax.ShapeDtypeStruct((bsz, seq, ATTN_WIDTH), BF16),
        scratch_shapes=[pltpu.VMEM((seq, 2 * LANES), BF16),
                        pltpu.VMEM((HEADS_PER_TILE, seq, LANES), BF16)],
        compiler_params=_params(("parallel", "parallel")),
        name="moba_attn",
    )(qkv, qkv, qkv, kmean, table)


def _attn_out_ffn_kernel(x_ref, a_ref, wo_ref, wg_ref, wu_ref, wd_ref, ln_ref, o_ref, *, n_split, n_part):
    rows = x_ref.shape[0] // n_part
    parts = [slice(p * rows, (p + 1) * rows) for p in range(n_part)]
    width = wg_ref.shape[1] // n_split
    proj = [_dot(a_ref[r, :], wo_ref[...]) for r in parts]
    for r, t in zip(parts, proj):
        h = _layernorm(ALPHA * x_ref[r, :] + t, ln_ref[0:1, :], ln_ref[1:2, :])
        hb = h.astype(BF16)
        y = None
        for c in range(n_split):
            cols = slice(c * width, (c + 1) * width)
            g = _dot(hb, wg_ref[:, cols])
            u = _dot(hb, wu_ref[:, cols])
            part = _dot((g * jax.nn.sigmoid(g) * u).astype(BF16), wd_ref[cols, :])
            y = part if y is None else y + part
        o_ref[r, :] = _layernorm(ALPHA * h + y, ln_ref[2:3, :], ln_ref[3:4, :])


def _attn_out_ffn(x, attn, w_out, w_gate, w_up, w_down, ln):
    t, dm = x.shape
    ff = w_gate.shape[1]
    tm = FFN_TILE
    n_split = ff // FFN_SPLIT
    return pl.pallas_call(
        functools.partial(_attn_out_ffn_kernel, n_split=n_split, n_part=2),
        grid=(t // tm,),
        in_specs=[pl.BlockSpec((tm, dm), lambda i: (i, 0)),
                  pl.BlockSpec((tm, attn.shape[1]), lambda i: (i, 0)),
                  _resident(w_out.shape), _resident(w_gate.shape), _resident(w_up.shape),
                  _resident(w_down.shape), _resident(ln.shape)],
        out_specs=pl.BlockSpec((tm, dm), lambda i: (i, 0)),
        out_shape=jax.ShapeDtypeStruct((t, dm), F32),
        compiler_params=_params(("parallel",)),
        name="attn_out_ffn",
    )(x, attn, w_out, w_gate, w_up, w_down, ln)


def _route(h, wr_ref, tri_ref, cnt_s, rec_ref, cnt_ref, cols):
    tm = h.shape[0]
    h1, h2, _ = _split3(h)
    logits = _dot_nt(wr_ref[0], h1) + _dot_nt(wr_ref[1], h1) + _dot_nt(wr_ref[0], h2)
    e_idx = lax.broadcasted_iota(I32, (N_EXPERTS, tm), 0).astype(F32)
    top1 = jnp.max(logits, axis=0, keepdims=True)
    idx1 = jnp.min(jnp.where(logits == top1, e_idx, float(N_EXPERTS)), axis=0, keepdims=True)
    rest = jnp.where(e_idx == idx1, -jnp.inf, logits)
    top2 = jnp.max(rest, axis=0, keepdims=True)
    idx2 = jnp.min(jnp.where(rest == top2, e_idx, float(N_EXPERTS)), axis=0, keepdims=True)
    z = jnp.exp(top2 - top1)
    gate1 = 1.0 / (1.0 + z)
    gate2 = z / (1.0 + z)
    hot1 = jnp.where(e_idx == idx1, 1.0, 0.0)
    hot2 = jnp.where(e_idx == idx2, 1.0, 0.0)
    hot = hot1 + hot2
    before = _dot(hot.astype(BF16), tri_ref[...]) + cnt_s[:, 0:1]
    pos1 = jnp.sum(hot1 * before, axis=0, keepdims=True)
    pos2 = jnp.sum(hot2 * before, axis=0, keepdims=True)
    cnt_s[...] = cnt_s[...] + jnp.sum(hot, axis=1, keepdims=True)
    cnt_ref[...] = cnt_s[...]
    rec_ref[:, cols] = jnp.concatenate([gate1, gate2, idx1, idx2, pos1, pos2, jnp.zeros((2, tm), F32)], axis=0)


def _conv_mixer_kernel(x_ref, wi_ref, cw_ref, wo_ref, ln_ref, wr_ref, tri_ref,
                       o_ref, rec_ref, cnt_ref, y_s, cnt_s):
    tm, dm = x_ref.shape
    first = (pl.program_id(0) == 0) & (pl.program_id(1) == 0)

    @pl.when(first)
    def _():
        cnt_s[...] = jnp.zeros_like(cnt_s)

    @pl.when(pl.program_id(1) == 0)
    def _():
        y_s[0:8, :] = jnp.zeros((8, dm), F32)

    rows = tri_ref.shape[0]
    parts = [slice(p * rows, (p + 1) * rows) for p in range(tm // rows)]
    xbs = [x_ref[r, :].astype(BF16) for r in parts]
    for r, xb in zip(parts, xbs):
        y_s[8 + r.start:8 + r.stop, :] = _dot(xb, wi_ref[:, dm:2 * dm]) * _dot(xb, wi_ref[:, 2 * dm:])
    gate_bs = [_dot(xb, wi_ref[:, :dm]) for xb in xbs]
    mixed = []
    for r, gate_b in zip(parts, gate_bs):
        z = (cw_ref[0:1, :] * y_s[6 + r.start:6 + r.stop, :] + cw_ref[1:2, :] * y_s[7 + r.start:7 + r.stop, :]
             + cw_ref[2:3, :] * y_s[8 + r.start:8 + r.stop, :])
        mixed.append(_dot((gate_b * z).astype(BF16), wo_ref[...]))
    y_s[0:8, :] = y_s[tm:, :]
    hs = [_layernorm(ALPHA * x_ref[r, :] + m, ln_ref[0:1, :], ln_ref[1:2, :]) for r, m in zip(parts, mixed)]
    for r, h in zip(parts, hs):
        o_ref[r, :] = h
        _route(h, wr_ref, tri_ref, cnt_s, rec_ref, cnt_ref, r)


def _conv_mixer(x, w_in, conv_w, w_out, ln, wr, tri):
    bsz, seq, dm = x.shape
    tm = MIXER_TILE
    nj = seq // tm
    return pl.pallas_call(
        _conv_mixer_kernel,
        grid=(bsz, nj),
        in_specs=[pl.BlockSpec((None, tm, dm), lambda b, j: (b, j, 0)),
                  _resident(w_in.shape), _resident(conv_w.shape), _resident(w_out.shape),
                  _resident(ln.shape), _resident(wr.shape), _resident(tri.shape)],
        out_specs=[pl.BlockSpec((None, tm, dm), lambda b, j: (b, j, 0)),
                   pl.BlockSpec((8, tm), lambda b, j: (0, b * nj + j)),
                   pl.BlockSpec((N_EXPERTS, LANES), lambda b, j: (0, 0))],
        out_shape=[jax.ShapeDtypeStruct((bsz, seq, dm), F32),
                   jax.ShapeDtypeStruct((8, bsz * seq), F32),
                   jax.ShapeDtypeStruct((N_EXPERTS, LANES), F32)],
        scratch_shapes=[pltpu.VMEM((tm + 8, dm), F32), pltpu.VMEM((N_EXPERTS, LANES), F32)],
        compiler_params=_params(("arbitrary", "arbitrary")),
        name="conv_mixer",
    )(x, w_in, conv_w, w_out, ln, wr, tri)


def _gmlp_mixer_kernel(x_ref, wi_ref, dln_ref, ws_ref, bs_ref, wo_ref, ln_ref, wr_ref, tri_ref,
                       o_ref, rec_ref, cnt_ref, mix_s, cnt_s):
    tm, dm = x_ref.shape
    ch = GMLP_CHUNK
    gw = dm // GMLP_GROUPS

    @pl.when(pl.program_id(0) == 0)
    def _():
        cnt_s[...] = jnp.zeros_like(cnt_s)

    rows = tri_ref.shape[0]
    parts = [slice(p * rows, (p + 1) * rows) for p in range(tm // rows)]
    lower = lax.broadcasted_iota(I32, (ch, ch), 0) >= lax.broadcasted_iota(I32, (ch, ch), 1)
    ws = [jnp.where(lower, ws_ref[g], 0.0).astype(BF16) for g in range(GMLP_GROUPS)]
    pre = []
    for r in parts:
        xb = x_ref[r, :].astype(BF16)
        pre.append((_dot(xb, wi_ref[:, :dm]), _dot(xb, wi_ref[:, dm:])))
    mixed = []
    for r, (u_pre, v_pre) in zip(parts, pre):
        vb = _layernorm(_gelu(v_pre), dln_ref[0:1, :], dln_ref[1:2, :]).astype(BF16)
        for g in range(GMLP_GROUPS):
            for c in range(rows // ch):
                blk = _dot(ws[g], vb[c * ch:(c + 1) * ch, g * gw:(g + 1) * gw]) + bs_ref[:, g:g + 1]
                mix_s[r.start + c * ch:r.start + (c + 1) * ch, g * gw:(g + 1) * gw] = blk
        mixed.append(_dot((_gelu(u_pre) * mix_s[r, :]).astype(BF16), wo_ref[...]))
    hs = [_layernorm(ALPHA * x_ref[r, :] + m, ln_ref[0:1, :], ln_ref[1:2, :]) for r, m in zip(parts, mixed)]
    for r, h in zip(parts, hs):
        o_ref[r, :] = h
        _route(h, wr_ref, tri_ref, cnt_s, rec_ref, cnt_ref, r)


def _gmlp_mixer(x, w_in, dln, w_s, b_s_t, w_out, ln, wr, tri):
    t, dm = x.shape
    tm = MIXER_TILE
    return pl.pallas_call(
        _gmlp_mixer_kernel,
        grid=(t // tm,),
        in_specs=[pl.BlockSpec((tm, dm), lambda i: (i, 0)),
                  _resident(w_in.shape), _resident(dln.shape), _resident(w_s.shape),
                  _resident(b_s_t.shape), _resident(w_out.shape), _resident(ln.shape),
                  _resident(wr.shape), _resident(tri.shape)],
        out_specs=[pl.BlockSpec((tm, dm), lambda i: (i, 0)),
                   pl.BlockSpec((8, tm), lambda i: (0, i)),
                   pl.BlockSpec((N_EXPERTS, LANES), lambda i: (0, 0))],
        out_shape=[jax.ShapeDtypeStruct((t, dm), F32),
                   jax.ShapeDtypeStruct((8, t), F32),
                   jax.ShapeDtypeStruct((N_EXPERTS, LANES), F32)],
        scratch_shapes=[pltpu.VMEM((tm, dm), F32), pltpu.VMEM((N_EXPERTS, LANES), F32)],
        compiler_params=_params(("arbitrary",)),
        name="gmlp_mixer",
    )(x, w_in, dln, w_s, b_s_t, w_out, ln, wr, tri)


def _load_slots(slot_hbm, slot_s, sem, i, tm):
    half = lax.rem(i, 2)

    def copies(step, into):
        return [pltpu.make_async_copy(slot_hbm.at[c, step], slot_s[c].at[pl.ds(pl.multiple_of(into * tm, tm), tm)],
                                      sem.at[c, into]) for c in range(2)]

    @pl.when(i == 0)
    def _():
        for cp in copies(0, 0):
            cp.start()

    @pl.when(i + 1 < pl.num_programs(0))
    def _():
        for cp in copies(i + 1, 1 - half):
            cp.start()

    for cp in copies(i, half):
        cp.wait()
    return half * tm


def _start_row_copies(n_rows, row_copy, slot_s, base):
    for r in range(n_rows):
        for c in range(2):
            row_copy(r, c, slot_s[c][base + r]).start(priority=(r + c) % 2)


def _moe_dispatch_kernel(fs_ref, fc_ref, slot_hbm, h_ref, x_hbm, slot0_s, slot1_s, zero_s, idx_sem, row_sem):
    i = pl.program_id(0)
    tm = h_ref.shape[0]
    slot_s = (slot0_s, slot1_s)
    base = _load_slots(slot_hbm, slot_s, idx_sem, i, tm)

    def row_copy(r, c, slot):
        return pltpu.make_async_copy(h_ref.at[pl.ds(r, 1), :], x_hbm.at[pl.ds(slot, 1), :], row_sem.at[c])

    _start_row_copies(tm, row_copy, slot_s, base)

    @pl.when(i == 0)
    def _():
        zero_s[...] = jnp.zeros_like(zero_s)

        def fill_copy(row):
            return pltpu.make_async_copy(zero_s.at[pl.ds(0, 1), :], x_hbm.at[pl.ds(row, 1), :], row_sem.at[2])

        for e in range(N_EXPERTS):
            def fill(j, carry, e=e):
                fill_copy(fs_ref[e] + j).start()
                return carry

            lax.fori_loop(0, fc_ref[e], fill, 0)

            def drain(j, carry):
                fill_copy(0).wait()
                return carry

            lax.fori_loop(0, fc_ref[e], drain, 0)

    for c in range(2):
        pltpu.make_async_copy(h_ref, x_hbm.at[pl.ds(0, tm), :], row_sem.at[c]).wait()


def _moe_dispatch(h, slots, fill_start, fill_count, n_slots):
    t, dm = h.shape
    tm = slots.shape[2]
    grid_spec = pltpu.PrefetchScalarGridSpec(
        num_scalar_prefetch=2,
        grid=(t // tm,),
        in_specs=[pl.BlockSpec(memory_space=pl.ANY),
                  pl.BlockSpec((tm, dm), lambda i, fs, fc: (i, 0))],
        out_specs=pl.BlockSpec(memory_space=pl.ANY),
        scratch_shapes=[pltpu.SMEM((2 * tm,), I32), pltpu.SMEM((2 * tm,), I32), pltpu.VMEM((8, dm), F32),
                        pltpu.SemaphoreType.DMA((2, 2)), pltpu.SemaphoreType.DMA((3,))],
    )
    return pl.pallas_call(
        _moe_dispatch_kernel,
        grid_spec=grid_spec,
        out_shape=jax.ShapeDtypeStruct((n_slots, dm), F32),
        compiler_params=_params(("arbitrary",)),
        name="moe_dispatch",
    )(fill_start, fill_count, slots, h)


def _moe_expert_kernel(te_ref, nu_ref, x_ref, wg_ref, wu_ref, wd_ref, y_ref, xb_s):
    i, f = pl.program_id(0), pl.program_id(1)

    @pl.when(i < nu_ref[0])
    def _():
        @pl.when(f == 0)
        def _():
            xb_s[...] = x_ref[...].astype(BF16)
            y_ref[...] = jnp.zeros_like(y_ref)

        xb = xb_s[...]
        g = _dot(xb, wg_ref[...])
        u = _dot(xb, wu_ref[...])
        y_ref[...] += _dot((g * jax.nn.sigmoid(g) * u).astype(BF16), wd_ref[...])


def _moe_experts(x_sorted, tile_expert, n_used, w_gate, w_up, w_down):
    n_slots, dm = x_sorted.shape
    tm = MOE_TILE
    ff = w_gate.shape[2]
    tf = MOE_FF_TILE

    def tile_block(i, f, te, nu):
        return jnp.minimum(i, nu[0] - 1), 0

    def col_block(i, f, te, nu):
        return te[i], 0, jnp.where(i < nu[0], f, ff // tf - 1)

    def row_block(i, f, te, nu):
        return te[i], jnp.where(i < nu[0], f, ff // tf - 1), 0

    grid_spec = pltpu.PrefetchScalarGridSpec(
        num_scalar_prefetch=2,
        grid=(n_slots // tm, ff // tf),
        in_specs=[pl.BlockSpec((tm, dm), tile_block),
                  pl.BlockSpec((None, dm, tf), col_block),
                  pl.BlockSpec((None, dm, tf), col_block),
                  pl.BlockSpec((None, tf, dm), row_block)],
        out_specs=pl.BlockSpec((tm, dm), tile_block),
        scratch_shapes=[pltpu.VMEM((tm, dm), BF16)],
    )
    return pl.pallas_call(
        _moe_expert_kernel,
        grid_spec=grid_spec,
        out_shape=jax.ShapeDtypeStruct((n_slots, dm), F32),
        compiler_params=_params(("arbitrary", "arbitrary")),
        name="moe_experts",
    )(tile_expert, n_used, x_sorted, w_gate, w_up, w_down)


def _moe_combine_kernel(slot_hbm, y_hbm, h_ref, rec_ref, ln_ref, o_ref, slot0_s, slot1_s, y_s, idx_sem, row_sem):
    i = pl.program_id(0)
    tm = h_ref.shape[0]
    slot_s = (slot0_s, slot1_s)
    base = _load_slots(slot_hbm, slot_s, idx_sem, i, tm)

    half_rows = tm // 2

    def row_copy(r, c, slot):
        return pltpu.make_async_copy(y_hbm.at[pl.ds(slot, 1), :], y_s.at[c, pl.ds(r, 1), :],
                                     row_sem.at[c, r // half_rows])

    _start_row_copies(tm, row_copy, slot_s, base)

    pad_rows = jnp.zeros((LANES - 8, LANES), F32)
    for c in range(tm // LANES):
        if (c * LANES) % half_rows == 0:
            for ch in range(2):
                pltpu.make_async_copy(y_hbm.at[pl.ds(0, half_rows), :], y_s.at[ch, pl.ds(0, half_rows), :],
                                      row_sem.at[ch, (c * LANES) // half_rows]).wait()
        rows = slice(c * LANES, (c + 1) * LANES)
        gates = jnp.concatenate([rec_ref[:, rows], pad_rows], axis=0).T
        y = gates[:, 0:1] * y_s[0, rows, :] + gates[:, 1:2] * y_s[1, rows, :]
        o_ref[rows, :] = _layernorm(ALPHA * h_ref[rows, :] + y, ln_ref[0:1, :], ln_ref[1:2, :])


def _moe_combine(h, y_sorted, slots, rec, ln):
    t, dm = h.shape
    tm = slots.shape[2]
    return pl.pallas_call(
        _moe_combine_kernel,
        grid=(t // tm,),
        in_specs=[pl.BlockSpec(memory_space=pl.ANY),
                  pl.BlockSpec(memory_space=pl.ANY),
                  pl.BlockSpec((tm, dm), lambda i: (i, 0)),
                  pl.BlockSpec((8, tm), lambda i: (0, i)),
                  _resident(ln.shape)],
        out_specs=pl.BlockSpec((tm, dm), lambda i: (i, 0)),
        out_shape=jax.ShapeDtypeStruct((t, dm), F32),
        scratch_shapes=[pltpu.SMEM((2 * tm,), I32), pltpu.SMEM((2 * tm,), I32), pltpu.VMEM((2, tm, dm), F32),
                        pltpu.SemaphoreType.DMA((2, 2)), pltpu.SemaphoreType.DMA((2, 2))],
        compiler_params=_params(("arbitrary",)),
        name="moe_combine",
    )(slots, y_sorted, h, rec, ln)


def _moe(h, rec, counts, w_gate, w_up, w_down, ln):
    t, dm = h.shape
    tm = MOE_TILE
    n_tiles = (2 * t) // tm + N_EXPERTS
    cnt = counts[:, 0].astype(I32)
    padded = (cnt + tm - 1) // tm * tm
    pad_end = jnp.cumsum(padded)
    pad_start = pad_end - padded
    experts = jnp.arange(N_EXPERTS, dtype=I32)

    def slot_of(e_row, pos_row):
        e = e_row.astype(I32)
        start = jnp.sum(jnp.where(e[None, :] == experts[:, None], pad_start[:, None], 0), axis=0)
        return start + pos_row.astype(I32)

    slots = jnp.stack([slot_of(rec[2], rec[4]), slot_of(rec[3], rec[5])]).reshape(2, t // MOE_ROW_TILE, MOE_ROW_TILE)
    n_used = pad_end[-1] // tm
    tile_start = jnp.minimum(jnp.arange(n_tiles, dtype=I32), n_used - 1) * tm
    tile_expert = jnp.sum(tile_start[:, None] >= pad_end[None, :], axis=1).astype(I32)
    x_sorted = _moe_dispatch(h, slots, pad_start + cnt, padded - cnt, n_tiles * tm)
    y_sorted = _moe_experts(x_sorted, tile_expert, n_used.astype(I32).reshape(1), w_gate, w_up, w_down)
    return _moe_combine(h, y_sorted, slots, rec, ln)


def kernel(x, rel_bias, ln_gain, ln_bias, a_w_in, a_w_out, b_w_in, b_conv, b_w_out, c_w_in, c_w_out,
           d_w_in, d_ln_gain, d_ln_bias, d_w_s, d_b_s, d_w_out, f_w_gate, f_w_up, f_w_down,
           e_w_router, e_w_gate, e_w_up, e_w_down):
    bsz, seq, dm = x.shape
    t = bsz * seq
    assert seq % (8 * MOBA_BLOCK) == 0 and seq == DIL_GROUPS[-1][0] and ln_gain.shape[0] == DEPTH
    assert seq % PROJ_TILE == 0 and seq % MIXER_TILE == 0 and t % MOE_TILE == 0
    q_scale = HEAD_DIM ** -0.5 * LOG2_E

    def ln_rows(i, *pairs):
        return jnp.concatenate([jnp.stack([ln_gain[i, p], ln_bias[i, p]]) for p in pairs], axis=0)

    def router_weights(w):
        w1, w2, _ = _split3(w.T.astype(F32))
        return jnp.stack([w1, w2])

    tri = jnp.triu(jnp.ones((TOKEN_TILE, TOKEN_TILE), BF16), k=1)

    def scaled_qkv(w):
        scale = jnp.concatenate([jnp.full((ATTN_WIDTH,), q_scale, F32), jnp.ones((2 * ATTN_WIDTH,), F32)])
        return (w * jnp.tile(scale, w.shape[1] // scale.shape[0])).astype(BF16)

    h = x
    a_w = scaled_qkv(a_w_in[0])
    qkv = [_project_dilated(h, a_w, g, dil) for g, (_, dil) in enumerate(DIL_GROUPS)]
    tables = [_dilated_bias_table(rel_bias, dil) for _, dil in DIL_GROUPS]
    tables[2] = tables[2][:, :, DIL_BACK:]
    attn = _dilated_attention(qkv, tables)
    h = _attn_out_ffn(h.reshape(t, dm), attn.reshape(t, ATTN_WIDTH), a_w_out[0].astype(BF16),
                      f_w_gate[0].astype(BF16), f_w_up[0].astype(BF16), f_w_down[0].astype(BF16),
                      ln_rows(0, 0, 1))
    h, rec, counts = _conv_mixer(h.reshape(bsz, seq, dm), b_w_in[0].astype(BF16), b_conv[0],
                                 b_w_out[0].astype(BF16), ln_rows(1, 0), router_weights(e_w_router[0]), tri)
    h = _moe(h.reshape(t, dm), rec, counts, _expert_weights_bf16(e_w_gate, 0), _expert_weights_bf16(e_w_up, 0),
             _expert_weights_bf16(e_w_down, 0), ln_rows(1, 1))
    qkv_c, kmean = _project_moba(h.reshape(bsz, seq, dm), scaled_qkv(c_w_in[0]))
    attn = _moba_attention(qkv_c, kmean, _moba_bias_table(rel_bias, seq // MOBA_BLOCK))
    h = _attn_out_ffn(h, attn.reshape(t, ATTN_WIDTH), c_w_out[0].astype(BF16),
                      f_w_gate[1].astype(BF16), f_w_up[1].astype(BF16), f_w_down[1].astype(BF16),
                      ln_rows(2, 0, 1))
    h, rec, counts = _gmlp_mixer(h, d_w_in[0].astype(BF16), jnp.stack([d_ln_gain[0], d_ln_bias[0]]),
                                 d_w_s[0], d_b_s[0].T, d_w_out[0].astype(BF16), ln_rows(3, 0),
                                 router_weights(e_w_router[1]), tri)
    h = _moe(h, rec, counts, _expert_weights_bf16(e_w_gate, 1), _expert_weights_bf16(e_w_up, 1),
             _expert_weights_bf16(e_w_down, 1), ln_rows(3, 1))
    return h.reshape(bsz, seq, dm)
```
